```python
import math
import jax, jax.numpy as jnp
from jax import lax
import numpy as np

D_MODEL = 1024
BATCH = 8
SEQ = 2048
DEPTH = 2
DEC_BATCH = 128
DEC_SEQ = 4
PAST_LEN = 16384
PAGE_SIZE = 128

D_MIX = D_MODEL
D_RNN = D_MIX // 2
RNN_HEADS = 8
RNN_HEAD_DIM = D_RNN // RNN_HEADS
RG_CONV_W = 4
LRU_C = 8.0
D_CONF = D_MIX - D_RNN
CF_CONV_W = 31
D_IN = 2 * D_RNN + 2 * D_CONF
N_MEM = 256
MEM_HEADS = 4
MEM_HEAD_DIM = D_MODEL // MEM_HEADS
D_FF = ((8 * D_MODEL // 3 + 255) // 256) * 256
EPS = 1e-6

kernel_name = "hymba_rglru_conformer_memxattn_step"


def _rmsnorm(x, g):
    xf = x.astype(jnp.float32)
    y = xf * lax.rsqrt(jnp.mean(xf * xf, axis=-1, keepdims=True) + EPS)
    return (y * g.astype(jnp.float32)).astype(x.dtype)


def _layernorm(x, g, b):
    xf = x.astype(jnp.float32)
    mu = jnp.mean(xf, axis=-1, keepdims=True)
    xc = xf - mu
    var = jnp.mean(xc * xc, axis=-1, keepdims=True)
    return (xc * lax.rsqrt(var + EPS) * g.astype(jnp.float32) + b.astype(jnp.float32)).astype(x.dtype)


def _causal_dwconv(x, buf, w, b):
    k = w.shape[0]
    xp = jnp.concatenate([buf.astype(x.dtype), x], axis=1)
    y = lax.conv_general_dilated(
        xp, w[:, None, :].astype(x.dtype), window_strides=(1,), padding='VALID',
        dimension_numbers=('NWC', 'WIO', 'NWC'), feature_group_count=x.shape[-1])
    return y + b.astype(x.dtype), xp[:, -(k - 1):]


def _rglru(x, h0, wa, ba, wx, bx, lam):
    bsz, t = x.shape[0], x.shape[1]
    f32 = jnp.float32
    xf = x.astype(f32)
    xh = xf.reshape(bsz, t, RNN_HEADS, RNN_HEAD_DIM)
    r = jax.nn.sigmoid(jnp.einsum('bthi,hij->bthj', xh, wa.astype(f32)).reshape(bsz, t, D_RNN) + ba.astype(f32))
    i = jax.nn.sigmoid(jnp.einsum('bthi,hij->bthj', xh, wx.astype(f32)).reshape(bsz, t, D_RNN) + bx.astype(f32))
    log_a = -LRU_C * r * jax.nn.softplus(-lam.astype(f32))
    a = jnp.exp(log_a)
    mult = jnp.sqrt(-jnp.expm1(2.0 * log_a))
    bterm = mult * (i * xf)
    bterm = bterm.at[:, 0].add(a[:, 0] * h0.astype(f32))

    def combine(left, right):
        a_l, b_l = left
        a_r, b_r = right
        return a_l * a_r, a_r * b_l + b_r

    _, h = lax.associative_scan(combine, (a, bterm), axis=1)
    return h.astype(x.dtype), h[:, -1].astype(h0.dtype)


def _mem_kv(mem, g, wk, wv):
    bsz = mem.shape[0]
    m = _rmsnorm(mem, g)
    k = (m @ wk).reshape(bsz, N_MEM, MEM_HEADS, MEM_HEAD_DIM)
    v = (m @ wv).reshape(bsz, N_MEM, MEM_HEADS, MEM_HEAD_DIM)
    return k, v


def _cross_attn(x, k, v, wq, wo):
    bsz, t = x.shape[0], x.shape[1]
    q = (x @ wq).reshape(bsz, t, MEM_HEADS, MEM_HEAD_DIM).astype(jnp.float32)
    s = jnp.einsum('bthd,bmhd->bhtm', q, k.astype(jnp.float32)) * (MEM_HEAD_DIM ** -0.5)
    p = jax.nn.softmax(s, axis=-1)
    o = jnp.einsum('bhtm,bmhd->bthd', p, v.astype(jnp.float32)).astype(x.dtype)
    return o.reshape(bsz, t, MEM_HEADS * MEM_HEAD_DIM) @ wo


def _layer(x, mem_k, mem_v, h0, rg_buf, cf_buf, lw):
    u = _rmsnorm(x, lw['norm_mix_g']) @ lw['w_in']
    xr = u[..., :D_RNN]
    gr = u[..., D_RNN:2 * D_RNN]
    cv = u[..., 2 * D_RNN:2 * D_RNN + D_CONF]
    cg = u[..., 2 * D_RNN + D_CONF:]
    xr_c, rg_buf_new = _causal_dwconv(xr, rg_buf, lw['rg_conv_w'], lw['rg_conv_b'])
    h, h_last = _rglru(xr_c, h0, lw['rg_wa'], lw['rg_ba'], lw['rg_wx'], lw['rg_bx'], lw['rg_lambda'])
    y_r = h * jax.nn.gelu(gr)
    c = cv * jax.nn.sigmoid(cg)
    c_c, cf_buf_new = _causal_dwconv(c, cf_buf, lw['cf_conv_w'], lw['cf_conv_b'])
    y_c = jax.nn.silu(_layernorm(c_c, lw['cf_ln_g'], lw['cf_ln_b']))
    x = x + jnp.concatenate([y_r, y_c], axis=-1) @ lw['w_out']
    x = x + _cross_attn(_rmsnorm(x, lw['norm_attn_g']), mem_k, mem_v, lw['w_q'], lw['w_o'])
    z = _rmsnorm(x, lw['norm_ffn_g'])
    x = x + (jax.nn.silu(z @ lw['w_gate']) * (z @ lw['w_up'])) @ lw['w_down']
    return x, h_last, rg_buf_new, cf_buf_new


def setup_inputs(seed: int = 0) -> dict:
    key = jax.random.key(seed)
    ks = iter(jax.random.split(key, 40))
    f32 = jnp.float32

    def nrm(shape, scale):
        return jax.random.normal(next(ks), shape, f32) * scale

    def gain(shape):
        return 1.0 + nrm(shape, 0.02)

    u = jax.random.uniform(next(ks), (DEPTH, D_RNN), f32, 0.9, 0.999)
    s = u ** (1.0 / LRU_C)
    rg_lambda = jnp.log(s) - jnp.log1p(-s)
    return {
        'x_prompt': nrm((BATCH, SEQ, D_MODEL), 1.0),
        'x_sample': nrm((DEC_BATCH, DEC_SEQ, D_MODEL), 1.0),
        'state_rglru_h': nrm((DEPTH, DEC_BATCH, D_RNN), 0.5),
        'state_rglru_conv': nrm((DEPTH, DEC_BATCH, RG_CONV_W - 1, D_RNN), 1.0),
        'state_conf_conv': nrm((DEPTH, DEC_BATCH, CF_CONV_W - 1, D_CONF), 1.0),
        'cache_mem_k': nrm((DEPTH, DEC_BATCH, N_MEM, MEM_HEADS, MEM_HEAD_DIM), 1.0),
        'cache_mem_v': nrm((DEPTH, DEC_BATCH, N_MEM, MEM_HEADS, MEM_HEAD_DIM), 1.0),
        'mem_prompt': nrm((BATCH, N_MEM, D_MODEL), 1.0),
        'norm_mix_g': gain((DEPTH, D_MODEL)),
        'w_in': nrm((DEPTH, D_MODEL, D_IN), D_MODEL ** -0.5),
        'rg_conv_w': nrm((DEPTH, RG_CONV_W, D_RNN), RG_CONV_W ** -0.5),
        'rg_conv_b': nrm((DEPTH, D_RNN), 0.02),
        'rg_wa': nrm((DEPTH, RNN_HEADS, RNN_HEAD_DIM, RNN_HEAD_DIM), RNN_HEAD_DIM ** -0.5),
        'rg_ba': nrm((DEPTH, D_RNN), 0.02),
        'rg_wx': nrm((DEPTH, RNN_HEADS, RNN_HEAD_DIM, RNN_HEAD_DIM), RNN_HEAD_DIM ** -0.5),
        'rg_bx': nrm((DEPTH, D_RNN), 0.02),
        'rg_lambda': rg_lambda,
        'cf_conv_w': nrm((DEPTH, CF_CONV_W, D_CONF), CF_CONV_W ** -0.5),
        'cf_conv_b': nrm((DEPTH, D_CONF), 0.02),
        'cf_ln_g': gain((DEPTH, D_CONF)),
        'cf_ln_b': nrm((DEPTH, D_CONF), 0.02),
        'w_out': nrm((DEPTH, D_MIX, D_MODEL), D_MIX ** -0.5),
        'norm_attn_g': gain((DEPTH, D_MODEL)),
        'norm_mem_g': gain((DEPTH, D_MODEL)),
        'w_q': nrm((DEPTH, D_MODEL, MEM_HEADS * MEM_HEAD_DIM), D_MODEL ** -0.5),
        'w_k': nrm((DEPTH, D_MODEL, MEM_HEADS * MEM_HEAD_DIM), D_MODEL ** -0.5),
        'w_v': nrm((DEPTH, D_MODEL, MEM_HEADS * MEM_HEAD_DIM), D_MODEL ** -0.5),
        'w_o': nrm((DEPTH, MEM_HEADS * MEM_HEAD_DIM, D_MODEL), (MEM_HEADS * MEM_HEAD_DIM) ** -0.5),
        'norm_ffn_g': gain((DEPTH, D_MODEL)),
        'w_gate': nrm((DEPTH, D_MODEL, D_FF), D_MODEL ** -0.5),
        'w_up': nrm((DEPTH, D_MODEL, D_FF), D_MODEL ** -0.5),
        'w_down': nrm((DEPTH, D_FF, D_MODEL), D_FF ** -0.5),
        'norm_final_g': gain((D_MODEL,)),
    }


def reference(x_prompt, x_sample, state_rglru_h, state_rglru_conv, state_conf_conv,
              cache_mem_k, cache_mem_v, mem_prompt,
              norm_mix_g, w_in, rg_conv_w, rg_conv_b, rg_wa, rg_ba, rg_wx, rg_bx, rg_lambda,
              cf_conv_w, cf_conv_b, cf_ln_g, cf_ln_b, w_out,
              norm_attn_g, norm_mem_g, w_q, w_k, w_v, w_o,
              norm_ffn_g, w_gate, w_up, w_down, norm_final_g):
    bp = x_prompt.shape[0]
    xp = x_prompt
    xs = x_sample
    p_h, p_rg, p_cf, p_mk, p_mv = [], [], [], [], []
    s_h, s_rg, s_cf = [], [], []
    for l in range(DEPTH):
        lw = {
            'norm_mix_g': norm_mix_g[l], 'w_in': w_in[l],
            'rg_conv_w': rg_conv_w[l], 'rg_conv_b': rg_conv_b[l],
            'rg_wa': rg_wa[l], 'rg_ba': rg_ba[l], 'rg_wx': rg_wx[l], 'rg_bx': rg_bx[l],
            'rg_lambda': rg_lambda[l],
            'cf_conv_w': cf_conv_w[l], 'cf_conv_b': cf_conv_b[l],
            'cf_ln_g': cf_ln_g[l], 'cf_ln_b': cf_ln_b[l], 'w_out': w_out[l],
            'norm_attn_g': norm_attn_g[l], 'w_q': w_q[l], 'w_o': w_o[l],
            'norm_ffn_g': norm_ffn_g[l], 'w_gate': w_gate[l], 'w_up': w_up[l], 'w_down': w_down[l],
        }
        mk, mv = _mem_kv(mem_prompt, norm_mem_g[l], w_k[l], w_v[l])
        h0 = jnp.zeros((bp, D_RNN), state_rglru_h.dtype)
        rb0 = jnp.zeros((bp, RG_CONV_W - 1, D_RNN), state_rglru_conv.dtype)
        cb0 = jnp.zeros((bp, CF_CONV_W - 1, D_CONF), state_conf_conv.dtype)
        xp, hp, rbp, cbp = _layer(xp, mk, mv, h0, rb0, cb0, lw)
        p_h.append(hp); p_rg.append(rbp); p_cf.append(cbp); p_mk.append(mk); p_mv.append(mv)
        xs, hs, rbs, cbs = _layer(xs, cache_mem_k[l], cache_mem_v[l], state_rglru_h[l],
                                  state_rglru_conv[l], state_conf_conv[l], lw)
        s_h.append(hs); s_rg.append(rbs); s_cf.append(cbs)
    y_prompt = _rmsnorm(xp, norm_final_g)
    y_sample = _rmsnorm(xs, norm_final_g)
    return (y_prompt, y_sample,
            jnp.stack(p_h), jnp.stack(p_rg), jnp.stack(p_cf), jnp.stack(p_mk), jnp.stack(p_mv),
            jnp.stack(s_h), jnp.stack(s_rg), jnp.stack(s_cf))
```

```python
import functools
import math

import jax
import jax.numpy as jnp
from jax import lax
from jax.experimental import pallas as pl
from jax.experimental.pallas import tpu as pltpu

F32 = jnp.float32
BF16 = jnp.bfloat16

EPS = 1e-6
LRU_C = 8.0
RG_CONV_W = 4
CF_CONV_W = 31
SUBLANES = 8
CF_HIST = 32
RG_HIST = 8
CHUNK = 16
VMEM_LIMIT = 56 * 1024 * 1024


def _dot(a, b):
    return jnp.dot(a, b, preferred_element_type=F32)


def _rms(x, g):
    return x * lax.rsqrt(jnp.mean(x * x, axis=-1, keepdims=True) + EPS) * g


def _gelu_tanh(x):
    return 0.5 * x * (1.0 + jnp.tanh(math.sqrt(2.0 / math.pi) * (x + 0.044715 * (x * x * x))))


def _silu(x):
    return x * jax.nn.sigmoid(x)


def _softplus(z):
    return jnp.maximum(z, 0.0) + jnp.log1p(jnp.exp(-jnp.abs(z)))


def _lru_coeffs(r_pre, i_pre, xc, ba, bx, nsp):
    r = jax.nn.sigmoid(r_pre + ba)
    i = jax.nn.sigmoid(i_pre + bx)
    log_a = r * nsp
    a = jnp.exp(log_a)
    t = jnp.tanh(log_a)
    mult = jnp.sqrt(-2.0 * t / (1.0 - t))
    return a, mult * (i * xc)


def _ln_silu(y, g, b):
    mu = jnp.mean(y, axis=-1, keepdims=True)
    yc = y - mu
    var = jnp.mean(yc * yc, axis=-1, keepdims=True)
    return _silu(yc * lax.rsqrt(var + EPS) * g + b)


def _memkv_kernel(mem_ref, g_ref, wk_ref, wv_ref, k_ref, v_ref, kb_ref, vb_ref):
    m = _rms(mem_ref[...], g_ref[...]).astype(BF16)
    k = _dot(m, wk_ref[...])
    v = _dot(m, wv_ref[...])
    k_ref[...] = k
    v_ref[...] = v
    kb_ref[...] = k.astype(BF16)
    vb_ref[...] = v.astype(BF16)


def _memkv(mem2d, g, wk, wv, tile):
    depth, d, _ = wk.shape
    rows = mem2d.shape[0]
    row_spec = pl.BlockSpec((None, tile, d), lambda l, i: (l, i, 0))
    w_spec = pl.BlockSpec((None, d, d), lambda l, i: (l, 0, 0))
    return pl.pallas_call(
        _memkv_kernel,
        grid=(depth, rows // tile),
        in_specs=[
            pl.BlockSpec((tile, d), lambda l, i: (i, 0)),
            pl.BlockSpec((None, 1, d), lambda l, i: (l, 0, 0)),
            w_spec, w_spec,
        ],
        out_specs=[row_spec, row_spec, row_spec, row_spec],
        out_shape=[
            jax.ShapeDtypeStruct((depth, rows, d), F32),
            jax.ShapeDtypeStruct((depth, rows, d), F32),
            jax.ShapeDtypeStruct((depth, rows, d), BF16),
            jax.ShapeDtypeStruct((depth, rows, d), BF16),
        ],
        compiler_params=pltpu.CompilerParams(
            dimension_semantics=("arbitrary", "arbitrary"), vmem_limit_bytes=VMEM_LIMIT),
        name="memkv",
    )(mem2d, g, wk, wv)


def _mix_prompt_kernel(x_ref, g_ref, win_ref, w4_ref, b4_ref, wa_ref, ba_ref, wx_ref, bx_ref,
                       lam_ref, w31_ref, b31_ref, lng_ref, lnb_ref, wout_ref,
                       xo_ref, h_ref, rg_ref, cf_ref,
                       xr_hist, c_hist, hcar, gr_s, xc_s, p_s, q_s, cat_s, *, tile, d_rnn):
    j = pl.program_id(1)
    nt = pl.num_programs(1)
    c_w = d_rnn
    n_grp = CHUNK // SUBLANES

    @pl.when(j == 0)
    def _():
        xr_hist[pl.ds(0, RG_HIST), :] = jnp.zeros((RG_HIST, c_w), F32)
        c_hist[pl.ds(0, CF_HIST), :] = jnp.zeros((CF_HIST, c_w), F32)
        hcar[...] = jnp.zeros((SUBLANES, c_w), F32)

    xn = _rms(x_ref[...], g_ref[...]).astype(BF16)
    xr_hist[pl.ds(RG_HIST, tile), :] = _dot(xn, win_ref[:, 0:c_w])
    gr_s[...] = _dot(xn, win_ref[:, c_w:2 * c_w])
    c_hist[pl.ds(CF_HIST, tile), :] = (
        _dot(xn, win_ref[:, 2 * c_w:3 * c_w]) * jax.nn.sigmoid(_dot(xn, win_ref[:, 3 * c_w:4 * c_w])))

    sub = lax.broadcasted_iota(jnp.int32, (n_grp, SUBLANES, c_w), 1)

    def conv4_chunk(ci, carry):
        s = pl.multiple_of(ci * CHUNK, CHUNK)
        xg = xr_hist[pl.ds(s, CHUNK + SUBLANES), :].reshape(n_grp + 1, SUBLANES, c_w)
        acc = xg[1:] * w4_ref[RG_CONV_W - 1] + b4_ref[...]
        for dly in range(1, RG_CONV_W):
            rolled = pltpu.roll(xg, dly, axis=1)
            shifted = jnp.where(sub >= dly, rolled[1:], rolled[:-1])
            acc = acc + shifted * w4_ref[RG_CONV_W - 1 - dly]
        xc_s[pl.ds(s, CHUNK), :] = acc.reshape(CHUNK, c_w)
        return carry

    lax.fori_loop(0, tile // CHUNK, conv4_chunk, 0)

    xcb = xc_s[...].astype(BF16)
    p_s[...] = _dot(xcb, wa_ref[...])
    q_s[...] = _dot(xcb, wx_ref[...])

    nsp = -LRU_C * _softplus(-lam_ref[...])

    def coeff_chunk(ci, carry):
        s = pl.multiple_of(ci * CHUNK, CHUNK)
        rows = pl.ds(s, CHUNK)
        grp = (n_grp, SUBLANES, c_w)
        a, b = _lru_coeffs(p_s[rows, :].reshape(grp), q_s[rows, :].reshape(grp),
                           xc_s[rows, :].reshape(grp), ba_ref[...], bx_ref[...], nsp)
        for dly in (1, 2, 4):
            ra = pltpu.roll(a, dly, axis=1)
            rb = pltpu.roll(b, dly, axis=1)
            keep = sub >= dly
            b = a * jnp.where(keep, rb, 0.0) + b
            a = a * jnp.where(keep, ra, 1.0)
        p_s[rows, :] = a.reshape(CHUNK, c_w)
        q_s[rows, :] = b.reshape(CHUNK, c_w)
        return carry

    lax.fori_loop(0, tile // CHUNK, coeff_chunk, 0)

    def carry_group(gi, hb):
        rows = pl.ds(pl.multiple_of(gi * SUBLANES, SUBLANES), SUBLANES)
        h = p_s[rows, :] * hb + q_s[rows, :]
        q_s[rows, :] = h
        return jnp.broadcast_to(h[SUBLANES - 1:SUBLANES, :], (SUBLANES, c_w))

    hb = lax.fori_loop(0, tile // SUBLANES, carry_group, hcar[...], unroll=8)
    hcar[...] = hb

    def out_chunk(ci, carry):
        s = pl.multiple_of(ci * CHUNK, CHUNK)
        rows = pl.ds(s, CHUNK)
        y_r = q_s[rows, :] * _gelu_tanh(gr_s[rows, :])
        cat_s[rows, 0:c_w] = y_r.astype(BF16)

        acc = None
        for res in range(SUBLANES):
            part = None
            for off in range(res, CF_HIST + 1, SUBLANES):
                k = off - (CF_HIST - (CF_CONV_W - 1))
                if k < 0 or k >= CF_CONV_W:
                    continue
                if res == 0:
                    blk = c_hist[pl.ds(s + off, CHUNK), :].reshape(n_grp, SUBLANES, c_w)
                else:
                    blk = c_hist[pl.ds(s + off - res, CHUNK + SUBLANES), :].reshape(
                        n_grp + 1, SUBLANES, c_w)
                term = blk * w31_ref[k]
                part = term if part is None else part + term
            if res != 0:
                rolled = pltpu.roll(part, SUBLANES - res, axis=1)
                part = jnp.where(sub < SUBLANES - res, rolled[:-1], rolled[1:])
            acc = part if acc is None else acc + part
        y_c = _ln_silu(acc + b31_ref[...], lng_ref[...], lnb_ref[...])
        cat_s[rows, c_w:2 * c_w] = y_c.reshape(CHUNK, c_w).astype(BF16)
        return carry

    lax.fori_loop(0, tile // CHUNK, out_chunk, 0)

    xo_ref[...] = x_ref[...] + _dot(cat_s[...], wout_ref[...])

    @pl.when(j == nt - 1)
    def _():
        h_ref[...] = hb[0:1, :]
        rg_ref[...] = xr_hist[pl.ds(RG_HIST + tile - (RG_CONV_W - 1), RG_CONV_W - 1), :]
        cf_ref[...] = c_hist[pl.ds(CF_HIST + tile - (CF_CONV_W - 1), CF_CONV_W - 1), :]

    xr_hist[pl.ds(0, RG_HIST), :] = xr_hist[pl.ds(tile, RG_HIST), :]
    c_hist[pl.ds(0, CF_HIST), :] = c_hist[pl.ds(tile, CF_HIST), :]


def _full(shape):
    return pl.BlockSpec(shape, lambda *_: (0,) * len(shape))


def _mix_prompt(x2d, lw, *, batch, seq, tile):
    d = x2d.shape[1]
    c_w = lw["ba"].shape[1]
    nt = seq // tile
    row_spec = pl.BlockSpec((tile, d), lambda b, j: (b * nt + j, 0))
    kern = functools.partial(_mix_prompt_kernel, tile=tile, d_rnn=c_w)
    return pl.pallas_call(
        kern,
        grid=(batch, nt),
        in_specs=[
            row_spec,
            _full((1, d)),
            _full(lw["w_in"].shape),
            _full(lw["w4"].shape), _full(lw["b4"].shape),
            _full(lw["wa"].shape), _full(lw["ba"].shape),
            _full(lw["wx"].shape), _full(lw["bx"].shape),
            _full(lw["lam"].shape),
            _full(lw["w31"].shape), _full(lw["b31"].shape),
            _full(lw["ln_g"].shape), _full(lw["ln_b"].shape),
            _full(lw["w_out"].shape),
        ],
        out_specs=[
            row_spec,
            pl.BlockSpec((None, 1, c_w), lambda b, j: (b, 0, 0)),
            pl.BlockSpec((None, RG_CONV_W - 1, c_w), lambda b, j: (b, 0, 0)),
            pl.BlockSpec((None, CF_CONV_W - 1, c_w), lambda b, j: (b, 0, 0)),
        ],
        out_shape=[
            jax.ShapeDtypeStruct(x2d.shape, F32),
            jax.ShapeDtypeStruct((batch, 1, c_w), F32),
            jax.ShapeDtypeStruct((batch, RG_CONV_W - 1, c_w), F32),
            jax.ShapeDtypeStruct((batch, CF_CONV_W - 1, c_w), F32),
        ],
        scratch_shapes=[
            pltpu.VMEM((RG_HIST + tile, c_w), F32),
            pltpu.VMEM((CF_HIST + tile, c_w), F32),
            pltpu.VMEM((SUBLANES, c_w), F32),
            pltpu.VMEM((tile, c_w), F32),
            pltpu.VMEM((tile, c_w), F32),
            pltpu.VMEM((tile, c_w), F32),
            pltpu.VMEM((tile, c_w), F32),
            pltpu.VMEM((tile, 2 * c_w), BF16),
        ],
        compiler_params=pltpu.CompilerParams(
            dimension_semantics=("arbitrary", "arbitrary"), vmem_limit_bytes=VMEM_LIMIT),
        name="mix_prompt",
    )(x2d, lw["g_mix"], lw["w_in"], lw["w4"], lw["b4"], lw["wa"], lw["ba"], lw["wx"], lw["bx"],
      lw["lam"], lw["w31"], lw["b31"], lw["ln_g"], lw["ln_b"], lw["w_out"])


def _mix_sample_kernel(x_ref, h0_ref, rb_ref, cb_ref, g_ref, win_ref, w4_ref, b4_ref, wa_ref, ba_ref,
                       wx_ref, bx_ref, lam_ref, w31_ref, b31_ref, lng_ref, lnb_ref, wout_ref,
                       xo_ref, h_ref, xr_ref, c_ref,
                       gr_s, xc_s, p_s, q_s, cat_s, *, steps, batch, d_rnn):
    c_w = d_rnn
    xn = _rms(x_ref[...], g_ref[...]).astype(BF16)
    xr_ref[...] = _dot(xn, win_ref[:, 0:c_w])
    gr_s[...] = _dot(xn, win_ref[:, c_w:2 * c_w])
    c_ref[...] = (
        _dot(xn, win_ref[:, 2 * c_w:3 * c_w]) * jax.nn.sigmoid(_dot(xn, win_ref[:, 3 * c_w:4 * c_w])))

    def rows_of(t, s):
        return pl.ds(pl.multiple_of(t * batch + s, SUBLANES), SUBLANES)

    def conv4_blk(bi, carry):
        s = bi * SUBLANES
        for t in range(steps):
            acc = b4_ref[...]
            for k in range(RG_CONV_W):
                jj = t + k
                if jj < RG_CONV_W - 1:
                    src = rb_ref[jj, pl.ds(pl.multiple_of(s, SUBLANES), SUBLANES), :]
                else:
                    src = xr_ref[rows_of(jj - (RG_CONV_W - 1), s), :]
                acc = acc + src * w4_ref[k]
            xc_s[rows_of(t, s), :] = acc
        return carry

    lax.fori_loop(0, batch // SUBLANES, conv4_blk, 0)

    xcb = xc_s[...].astype(BF16)
    p_s[...] = _dot(xcb, wa_ref[...])
    q_s[...] = _dot(xcb, wx_ref[...])
    nsp = -LRU_C * _softplus(-lam_ref[...])

    def scan_blk(bi, carry):
        s = bi * SUBLANES
        b_rows = pl.ds(pl.multiple_of(s, SUBLANES), SUBLANES)
        h = h0_ref[b_rows, :]
        for t in range(steps):
            rows = rows_of(t, s)
            a, b = _lru_coeffs(p_s[rows, :], q_s[rows, :], xc_s[rows, :], ba_ref[...], bx_ref[...], nsp)
            h = a * h + b
            cat_s[rows, 0:c_w] = h * _gelu_tanh(gr_s[rows, :])
        h_ref[b_rows, :] = h
        return carry

    lax.fori_loop(0, batch // SUBLANES, scan_blk, 0)

    def conv31_blk(bi, carry):
        s = bi * SUBLANES
        for t in range(steps):
            acc = b31_ref[...]
            for k in range(CF_CONV_W):
                jj = t + k
                if jj < CF_CONV_W - 1:
                    src = cb_ref[jj, pl.ds(pl.multiple_of(s, SUBLANES), SUBLANES), :]
                else:
                    src = c_ref[rows_of(jj - (CF_CONV_W - 1), s), :]
                acc = acc + src * w31_ref[k]
            cat_s[rows_of(t, s), c_w:2 * c_w] = _ln_silu(acc, lng_ref[...], lnb_ref[...])
        return carry

    lax.fori_loop(0, batch // SUBLANES, conv31_blk, 0)

    xo_ref[...] = x_ref[...] + _dot(cat_s[...].astype(BF16), wout_ref[...])


def _mix_sample(x_tm, h0, rb_tm, cb_tm, lw, *, steps, batch):
    rows, d = x_tm.shape
    c_w = lw["ba"].shape[1]
    kern = functools.partial(_mix_sample_kernel, steps=steps, batch=batch, d_rnn=c_w)
    operands = (x_tm, h0, rb_tm, cb_tm, lw["g_mix"], lw["w_in"], lw["w4"], lw["b4"], lw["wa"], lw["ba"],
                lw["wx"], lw["bx"], lw["lam"], lw["w31"], lw["b31"], lw["ln_g"], lw["ln_b"], lw["w_out"])
    return pl.pallas_call(
        kern,
        grid=(1,),
        in_specs=[_full(o.shape) for o in operands],
        out_specs=[_full((rows, d)), _full((batch, c_w)), _full((rows, c_w)), _full((rows, c_w))],
        out_shape=[
            jax.ShapeDtypeStruct((rows, d), F32),
            jax.ShapeDtypeStruct((batch, c_w), F32),
            jax.ShapeDtypeStruct((rows, c_w), F32),
            jax.ShapeDtypeStruct((rows, c_w), F32),
        ],
        scratch_shapes=[
            pltpu.VMEM((rows, c_w), F32),
            pltpu.VMEM((rows, c_w), F32),
            pltpu.VMEM((rows, c_w), F32),
            pltpu.VMEM((rows, c_w), F32),
            pltpu.VMEM((rows, 2 * c_w), F32),
        ],
        compiler_params=pltpu.CompilerParams(
            dimension_semantics=("arbitrary",), vmem_limit_bytes=VMEM_LIMIT),
        name="mix_sample",
    )(*operands)


def _attend(q, k, v, heads, head_dim):
    outs = []
    for h in range(heads):
        sl = slice(h * head_dim, (h + 1) * head_dim)
        s = lax.dot_general(q[:, sl], k[:, sl], (((1,), (1,)), ((), ())), preferred_element_type=F32)
        p = jnp.exp(s - jnp.max(s, axis=-1, keepdims=True))
        inv = 1.0 / jnp.sum(p, axis=-1, keepdims=True)
        outs.append(_dot(p.astype(BF16), v[:, sl]) * inv)
    return outs


def _attn_prompt_kernel(x_ref, k_ref, v_ref, g_ref, wq_ref, wo_ref, xo_ref, o_s, *, heads, head_dim):
    xn = _rms(x_ref[...], g_ref[...]).astype(BF16)
    q = (_dot(xn, wq_ref[...]) * (head_dim ** -0.5)).astype(BF16)
    outs = _attend(q, k_ref[...], v_ref[...], heads, head_dim)
    for h, o in enumerate(outs):
        o_s[:, h * head_dim:(h + 1) * head_dim] = o.astype(BF16)
    xo_ref[...] = x_ref[...] + _dot(o_s[...], wo_ref[...])


def _attn_prompt(x2d, kb, vb, layer, g, wq, wo, *, batch, seq, n_mem, heads, tile):
    d = x2d.shape[1]
    nt = seq // tile
    row_spec = pl.BlockSpec((tile, d), lambda b, j: (b * nt + j, 0))
    kv_spec = pl.BlockSpec((None, n_mem, d), lambda b, j: (layer, b, 0))
    kern = functools.partial(_attn_prompt_kernel, heads=heads, head_dim=d // heads)
    return pl.pallas_call(
        kern,
        grid=(batch, nt),
        in_specs=[row_spec, kv_spec, kv_spec, _full((1, d)), _full(wq.shape), _full(wo.shape)],
        out_specs=row_spec,
        out_shape=jax.ShapeDtypeStruct(x2d.shape, F32),
        scratch_shapes=[pltpu.VMEM((tile, d), BF16)],
        compiler_params=pltpu.CompilerParams(
            dimension_semantics=("arbitrary", "arbitrary"), vmem_limit_bytes=VMEM_LIMIT),
        name="attn_prompt",
    )(x2d, kb, vb, g, wq, wo)


def _attn_sample_kernel(x_ref, k_ref, v_ref, g_ref, wq_ref, wo_ref, xo_ref, q_s, o_s,
                        *, heads, head_dim, nb, pad_t):
    i = pl.program_id(0)

    @pl.when(i == 0)
    def _():
        xn = _rms(x_ref[...], g_ref[...]).astype(BF16)
        q_s[...] = _dot(xn, wq_ref[...]) * (head_dim ** -0.5)

    for bb in range(nb):
        rows = pl.ds(pl.multiple_of((i * nb + bb) * pad_t, pad_t), pad_t)
        outs = _attend(q_s[rows, :].astype(BF16), k_ref[bb].astype(BF16), v_ref[bb].astype(BF16),
                       heads, head_dim)
        for h, o in enumerate(outs):
            o_s[rows, h * head_dim:(h + 1) * head_dim] = o

    @pl.when(i == pl.num_programs(0) - 1)
    def _():
        xo_ref[...] = x_ref[...] + _dot(o_s[...].astype(BF16), wo_ref[...])


def _attn_sample(x_pad, cache_k, cache_v, layer, g, wq, wo, *, batch, n_mem, heads, nb, pad_t):
    rows, d = x_pad.shape
    kv_spec = pl.BlockSpec((None, nb, n_mem, d), lambda i: (layer, i, 0, 0))
    kern = functools.partial(_attn_sample_kernel, heads=heads, head_dim=d // heads, nb=nb, pad_t=pad_t)
    return pl.pallas_call(
        kern,
        grid=(batch // nb,),
        in_specs=[_full((rows, d)), kv_spec, kv_spec, _full((1, d)), _full(wq.shape), _full(wo.shape)],
        out_specs=_full((rows, d)),
        out_shape=jax.ShapeDtypeStruct((rows, d), F32),
        scratch_shapes=[pltpu.VMEM((rows, d), F32), pltpu.VMEM((rows, d), F32)],
        compiler_params=pltpu.CompilerParams(
            dimension_semantics=("arbitrary",), vmem_limit_bytes=VMEM_LIMIT),
        name="attn_sample",
    )(x_pad, cache_k, cache_v, g, wq, wo)


def _ffn_kernel(x_ref, g_ref, wg_ref, wu_ref, wd_ref, gf_ref, xo_ref, h_s, *, col_chunks, final_norm):
    x = x_ref[...]
    z = _rms(x, g_ref[...]).astype(BF16)
    for lo, hi in col_chunks:
        h_s[:, lo:hi] = (_silu(_dot(z, wg_ref[:, lo:hi])) * _dot(z, wu_ref[:, lo:hi])).astype(BF16)
    y = x + _dot(h_s[...], wd_ref[...])
    if final_norm:
        y = _rms(y, gf_ref[...])
    xo_ref[...] = y


def _ffn(x2d, g, wg, wu, wd, gf, *, tile, final_norm):
    rows, d = x2d.shape
    d_ff = wg.shape[1]
    step = 512
    col_chunks = tuple((lo, min(lo + step, d_ff)) for lo in range(0, d_ff, step))
    row_spec = pl.BlockSpec((tile, d), lambda i: (i, 0))

    def resident(shape):
        return pl.BlockSpec(shape, lambda i: (0,) * len(shape), pipeline_mode=pl.Buffered(1))

    kern = functools.partial(_ffn_kernel, col_chunks=col_chunks, final_norm=final_norm)
    return pl.pallas_call(
        kern,
        grid=(rows // tile,),
        in_specs=[row_spec, _full((1, d)), resident(wg.shape), resident(wu.shape), resident(wd.shape),
                  _full((1, d))],
        out_specs=row_spec,
        out_shape=jax.ShapeDtypeStruct((rows, d), F32),
        scratch_shapes=[pltpu.VMEM((tile, d_ff), BF16)],
        compiler_params=pltpu.CompilerParams(
            dimension_semantics=("arbitrary",), vmem_limit_bytes=VMEM_LIMIT),
        name="ffn",
    )(x2d, g, wg, wu, wd, gf)


def _rows8(v):
    return jnp.broadcast_to(v[..., None, :], v.shape[:-1] + (SUBLANES, v.shape[-1]))


def _block_diag(w):
    h, i, j = w.shape
    eye = jnp.eye(h, dtype=w.dtype)
    return jnp.einsum("hij,hg->higj", w, eye).reshape(h * i, h * j)


def kernel(x_prompt, x_sample, state_rglru_h, state_rglru_conv, state_conf_conv, cache_mem_k, cache_mem_v, mem_prompt, norm_mix_g, w_in, rg_conv_w, rg_conv_b, rg_wa, rg_ba, rg_wx, rg_bx, rg_lambda, cf_conv_w, cf_conv_b, cf_ln_g, cf_ln_b, w_out, norm_attn_g, norm_mem_g, w_q, w_k, w_v, w_o, norm_ffn_g, w_gate, w_up, w_down, norm_final_g):
    batch, seq, d = x_prompt.shape
    dec_batch, dec_seq, _ = x_sample.shape
    depth = w_in.shape[0]
    n_mem = mem_prompt.shape[1]
    heads = cache_mem_k.shape[3]
    pad_t = SUBLANES

    w_in_b = w_in.astype(BF16)
    w_out_b = w_out.astype(BF16)
    w_q_b = w_q.astype(BF16)
    w_o_b = w_o.astype(BF16)
    w_gate_b = w_gate.astype(BF16)
    w_up_b = w_up.astype(BF16)
    w_down_b = w_down.astype(BF16)
    gf = norm_final_g[None, :]

    def layer_weights(l):
        return {
            "g_mix": norm_mix_g[l][None, :],
            "w_in": w_in_b[l],
            "w4": _rows8(rg_conv_w[l]), "b4": _rows8(rg_conv_b[l]),
            "wa": _block_diag(rg_wa[l]).astype(BF16), "ba": _rows8(rg_ba[l]),
            "wx": _block_diag(rg_wx[l]).astype(BF16), "bx": _rows8(rg_bx[l]),
            "lam": _rows8(rg_lambda[l]),
            "w31": _rows8(cf_conv_w[l]), "b31": _rows8(cf_conv_b[l]),
            "ln_g": _rows8(cf_ln_g[l]), "ln_b": _rows8(cf_ln_b[l]),
            "w_out": w_out_b[l],
        }

    mem_k, mem_v, mem_kb, mem_vb = _memkv(
        mem_prompt.reshape(batch * n_mem, d), norm_mem_g[:, None, :],
        w_k.astype(BF16), w_v.astype(BF16), tile=512)

    cache_k = cache_mem_k.reshape(depth, dec_batch, n_mem, d)
    cache_v = cache_mem_v.reshape(depth, dec_batch, n_mem, d)

    xp = x_prompt.reshape(batch * seq, d)
    xs_tm = jnp.transpose(x_sample, (1, 0, 2)).reshape(dec_seq * dec_batch, d)
    xs_bm = None
    p_h, p_rg, p_cf, s_h, s_rg, s_cf = [], [], [], [], [], []
    for l in range(depth):
        lw = layer_weights(l)
        g_attn = norm_attn_g[l][None, :]
        g_ffn = norm_ffn_g[l][None, :]
        last = l == depth - 1

        xp, hp, rgp, cfp = _mix_prompt(xp, lw, batch=batch, seq=seq, tile=512)
        p_h.append(hp[:, 0, :]); p_rg.append(rgp); p_cf.append(cfp)
        xp = _attn_prompt(xp, mem_kb, mem_vb, l, g_attn, w_q_b[l], w_o_b[l],
                          batch=batch, seq=seq, n_mem=n_mem, heads=heads, tile=512)
        xp = _ffn(xp, g_ffn, w_gate_b[l], w_up_b[l], w_down_b[l], gf, tile=512, final_norm=last)

        rb_tm = jnp.transpose(state_rglru_conv[l], (1, 0, 2))
        cb_tm = jnp.transpose(state_conf_conv[l], (1, 0, 2))
        xs_tm, hs, xr_tm, c_tm = _mix_sample(xs_tm, state_rglru_h[l], rb_tm, cb_tm, lw,
                                             steps=dec_seq, batch=dec_batch)
        s_h.append(hs)
        xr_bm = jnp.transpose(xr_tm.reshape(dec_seq, dec_batch, -1), (1, 0, 2))
        c_bm = jnp.transpose(c_tm.reshape(dec_seq, dec_batch, -1), (1, 0, 2))
        s_rg.append(jnp.concatenate([state_rglru_conv[l], xr_bm], axis=1)[:, -(RG_CONV_W - 1):])
        s_cf.append(jnp.concatenate([state_conf_conv[l], c_bm], axis=1)[:, -(CF_CONV_W - 1):])

        xs_pad = jnp.pad(jnp.transpose(xs_tm.reshape(dec_seq, dec_batch, d), (1, 0, 2)),
                         ((0, 0), (0, pad_t - dec_seq), (0, 0))).reshape(dec_batch * pad_t, d)
        xs_pad = _attn_sample(xs_pad, cache_k, cache_v, l, g_attn, w_q_b[l], w_o_b[l],
                              batch=dec_batch, n_mem=n_mem, heads=heads, nb=4, pad_t=pad_t)
        xs_bm = xs_pad.reshape(dec_batch, pad_t, d)[:, :dec_seq].reshape(dec_batch * dec_seq, d)
        xs_bm = _ffn(xs_bm, g_ffn, w_gate_b[l], w_up_b[l], w_down_b[l], gf, tile=512, final_norm=last)
        if not last:
            xs_tm = jnp.transpose(xs_bm.reshape(dec_batch, dec_seq, d), (1, 0, 2)).reshape(
                dec_seq * dec_batch, d)

    kv_shape = (depth, batch, n_mem, heads, d // heads)
    return (xp.reshape(batch, seq, d), xs_bm.reshape(dec_batch, dec_seq, d),
            jnp.stack(p_h), jnp.stack(p_rg), jnp.stack(p_cf),
            mem_k.reshape(kv_shape), mem_v.reshape(kv_shape),
            jnp.stack(s_h), jnp.stack(s_rg), jnp.stack(s_cf))
```

```python
import functools
import math

import jax
import jax.numpy as jnp
from jax import lax
from jax.experimental import pallas as pl
from jax.experimental.pallas import tpu as pltpu

F32 = jnp.float32
BF16 = jnp.bfloat16

EPS = 1e-6
LRU_C = 8.0
RG_CONV_W = 4
CF_CONV_W = 31
SUBLANES = 8
CF_HIST = 32
RG_HIST = 8
CHUNK = 16
OUT_CHUNK = 32
CONV_LANES = 256
VMEM_LIMIT = 56 * 1024 * 1024


def _dot(a, b):
    return jnp.dot(a, b, preferred_element_type=F32)


def _rms(x, g):
    return x * lax.rsqrt(jnp.mean(x * x, axis=-1, keepdims=True) + EPS) * g


def _gelu_tanh(x):
    return 0.5 * x * (1.0 + jnp.tanh(math.sqrt(2.0 / math.pi) * (x + 0.044715 * (x * x * x))))


def _sigmoid(x):
    return 0.5 * (jnp.tanh(0.5 * x) + 1.0)


def _silu(x):
    return x * _sigmoid(x)


def _softplus(z):
    return jnp.maximum(z, 0.0) + jnp.log1p(jnp.exp(-jnp.abs(z)))


def _lru_coeffs(r_pre, i_pre, xc, ba, bx, nsp):
    r = _sigmoid(r_pre + ba)
    i = _sigmoid(i_pre + bx)
    log_a = r * nsp
    a = jnp.exp(log_a)
    t = jnp.tanh(log_a)
    mult = jnp.sqrt(-2.0 * t / (1.0 - t))
    return a, mult * (i * xc)


def _ln_silu(y, g, b):
    mu = jnp.mean(y, axis=-1, keepdims=True)
    yc = y - mu
    var = jnp.mean(yc * yc, axis=-1, keepdims=True)
    return _silu(yc * lax.rsqrt(var + EPS) * g + b)


def _memkv_kernel(mem_ref, g_ref, wk_ref, wv_ref, k_ref, v_ref, kb_ref, vb_ref):
    m = _rms(mem_ref[...], g_ref[...]).astype(BF16)
    k = _dot(m, wk_ref[...])
    v = _dot(m, wv_ref[...])
    k_ref[...] = k
    v_ref[...] = v
    kb_ref[...] = k.astype(BF16)
    vb_ref[...] = v.astype(BF16)


def _memkv(mem2d, g, wk, wv, tile):
    depth, d, _ = wk.shape
    rows = mem2d.shape[0]
    row_spec = pl.BlockSpec((None, tile, d), lambda l, i: (l, i, 0))
    w_spec = pl.BlockSpec((None, d, d), lambda l, i: (l, 0, 0))
    return pl.pallas_call(
        _memkv_kernel,
        grid=(depth, rows // tile),
        in_specs=[
            pl.BlockSpec((tile, d), lambda l, i: (i, 0)),
            pl.BlockSpec((None, 1, d), lambda l, i: (l, 0, 0)),
            w_spec, w_spec,
        ],
        out_specs=[row_spec, row_spec, row_spec, row_spec],
        out_shape=[
            jax.ShapeDtypeStruct((depth, rows, d), F32),
            jax.ShapeDtypeStruct((depth, rows, d), F32),
            jax.ShapeDtypeStruct((depth, rows, d), BF16),
            jax.ShapeDtypeStruct((depth, rows, d), BF16),
        ],
        compiler_params=pltpu.CompilerParams(
            dimension_semantics=("arbitrary", "arbitrary"), vmem_limit_bytes=VMEM_LIMIT),
        name="memkv",
    )(mem2d, g, wk, wv)


def _mix_prompt_kernel(x_ref, g_ref, win_ref, w4_ref, b4_ref, wa_ref, ba_ref, wx_ref, bx_ref,
                       lam_ref, w31_ref, b31_ref, lng_ref, lnb_ref, wout_ref,
                       xo_ref, h_ref, rg_ref, cf_ref,
                       xr_hist, c_hist, hcar, gr_s, xc_s, p_s, q_s, yc_s, cat_s, *, tile, d_rnn):
    j = pl.program_id(1)
    nt = pl.num_programs(1)
    c_w = d_rnn
    n_grp = CHUNK // SUBLANES

    @pl.when(j == 0)
    def _():
        xr_hist[pl.ds(0, RG_HIST), :] = jnp.zeros((RG_HIST, c_w), F32)
        c_hist[pl.ds(0, CF_HIST), :] = jnp.zeros((CF_HIST, c_w), F32)
        c_hist[pl.ds(CF_HIST + tile, SUBLANES), :] = jnp.zeros((SUBLANES, c_w), F32)
        hcar[...] = jnp.zeros((SUBLANES, c_w), F32)

    xn = _rms(x_ref[...], g_ref[...]).astype(BF16)
    xr_hist[pl.ds(RG_HIST, tile), :] = _dot(xn, win_ref[:, 0:c_w])
    gr_s[...] = _dot(xn, win_ref[:, c_w:2 * c_w])
    c_hist[pl.ds(CF_HIST, tile), :] = (
        _dot(xn, win_ref[:, 2 * c_w:3 * c_w]) * _sigmoid(_dot(xn, win_ref[:, 3 * c_w:4 * c_w])))

    sub = lax.broadcasted_iota(jnp.int32, (n_grp, SUBLANES, c_w), 1)
    sub8 = lax.broadcasted_iota(jnp.int32, (SUBLANES, c_w), 0)

    def group_rows(gi, offset=0):
        return pl.ds(pl.multiple_of(gi * SUBLANES, SUBLANES) + offset, SUBLANES)

    def conv4_group(gi, prev_rolled):
        xg = xr_hist[group_rows(gi, RG_HIST), :]
        acc = xg * w4_ref[RG_CONV_W - 1] + b4_ref[...]
        rolled = []
        for dly in range(1, RG_CONV_W):
            rolled.append(pltpu.roll(xg, dly, axis=0))
            acc = acc + jnp.where(sub8 >= dly, rolled[-1], prev_rolled[dly - 1]) * w4_ref[RG_CONV_W - 1 - dly]
        xc_s[group_rows(gi), :] = acc
        return tuple(rolled)

    x_hist = xr_hist[pl.ds(0, RG_HIST), :]
    lax.fori_loop(0, tile // SUBLANES, conv4_group,
                  tuple(pltpu.roll(x_hist, dly, axis=0) for dly in range(1, RG_CONV_W)), unroll=2)

    xcb = xc_s[...].astype(BF16)
    p_s[...] = _dot(xcb, wa_ref[...])
    q_s[...] = _dot(xcb, wx_ref[...])

    nsp = -LRU_C * _softplus(-lam_ref[...])

    def coeff_chunk(ci, carry):
        s = pl.multiple_of(ci * CHUNK, CHUNK)
        rows = pl.ds(s, CHUNK)
        grp = (n_grp, SUBLANES, c_w)
        a, b = _lru_coeffs(p_s[rows, :].reshape(grp), q_s[rows, :].reshape(grp),
                           xc_s[rows, :].reshape(grp), ba_ref[...], bx_ref[...], nsp)
        for dly in (1, 2, 4):
            ra = pltpu.roll(a, dly, axis=1)
            rb = pltpu.roll(b, dly, axis=1)
            keep = sub >= dly
            b = a * jnp.where(keep, rb, 0.0) + b
            a = a * jnp.where(keep, ra, 1.0)
        p_s[rows, :] = a.reshape(CHUNK, c_w)
        q_s[rows, :] = b.reshape(CHUNK, c_w)
        return carry

    lax.fori_loop(0, tile // CHUNK, coeff_chunk, 0, unroll=2)

    def carry_group(gi, hb):
        rows = group_rows(gi)
        h = p_s[rows, :] * hb + q_s[rows, :]
        q_s[rows, :] = h
        return jnp.broadcast_to(h[SUBLANES - 1:SUBLANES, :], (SUBLANES, c_w))

    hb = lax.fori_loop(0, tile // SUBLANES, carry_group, hcar[...], unroll=8)
    hcar[...] = hb

    first_tap_off = CF_HIST - (CF_CONV_W - 1)
    n_hist_groups = CF_HIST // SUBLANES + 1

    def conv31_pass(lane0):
        lanes = pl.ds(lane0, CONV_LANES)
        sub_l = lax.broadcasted_iota(jnp.int32, (SUBLANES, CONV_LANES), 0)

        def conv31_group(si, carry):
            p0_prev, rolled_prev = carry
            hist = [c_hist[group_rows(si, SUBLANES * q), lanes] for q in range(n_hist_groups)]
            parts = []
            for res in range(SUBLANES):
                part = None
                for q in range(n_hist_groups):
                    k = SUBLANES * q + res - first_tap_off
                    if 0 <= k < CF_CONV_W:
                        term = hist[q] * w31_ref[k, :, lanes]
                        part = term if part is None else part + term
                parts.append(part)
            rolled = tuple(pltpu.roll(parts[res], SUBLANES - res, axis=0) for res in range(1, SUBLANES))
            y = p0_prev
            for res in range(1, SUBLANES):
                y = y + jnp.where(sub_l < SUBLANES - res, rolled_prev[res - 1], rolled[res - 1])
            yc_s[group_rows(si), lanes] = y
            return parts[0], rolled

        zero = jnp.zeros((SUBLANES, CONV_LANES), F32)
        lax.fori_loop(0, tile // SUBLANES + 1, conv31_group, (zero, (zero,) * (SUBLANES - 1)))

    for lane0 in range(0, c_w, CONV_LANES):
        conv31_pass(lane0)

    def out_chunk(ci, carry):
        s = pl.multiple_of(ci * OUT_CHUNK, OUT_CHUNK)
        rows = pl.ds(s, OUT_CHUNK)
        y_r = q_s[rows, :] * _gelu_tanh(gr_s[rows, :])
        cat_s[rows, 0:c_w] = y_r.astype(BF16)
        conv = yc_s[pl.ds(s + SUBLANES, OUT_CHUNK), :].reshape(OUT_CHUNK // SUBLANES, SUBLANES, c_w)
        y_c = _ln_silu(conv + b31_ref[...], lng_ref[...], lnb_ref[...])
        cat_s[rows, c_w:2 * c_w] = y_c.reshape(OUT_CHUNK, c_w).astype(BF16)
        return carry

    lax.fori_loop(0, tile // OUT_CHUNK, out_chunk, 0, unroll=2)

    xo_ref[...] = x_ref[...] + _dot(cat_s[...], wout_ref[...])

    @pl.when(j == nt - 1)
    def _():
        h_ref[...] = hb[0:1, :]
        rg_ref[...] = xr_hist[pl.ds(RG_HIST + tile - (RG_CONV_W - 1), RG_CONV_W - 1), :]
        cf_ref[...] = c_hist[pl.ds(CF_HIST + tile - (CF_CONV_W - 1), CF_CONV_W - 1), :]

    xr_hist[pl.ds(0, RG_HIST), :] = xr_hist[pl.ds(tile, RG_HIST), :]
    c_hist[pl.ds(0, CF_HIST), :] = c_hist[pl.ds(tile, CF_HIST), :]


def _full(shape):
    return pl.BlockSpec(shape, lambda *_: (0,) * len(shape))


def _mix_prompt(x2d, lw, *, batch, seq, tile):
    d = x2d.shape[1]
    c_w = lw["ba"].shape[1]
    nt = seq // tile
    row_spec = pl.BlockSpec((tile, d), lambda b, j: (b * nt + j, 0))
    kern = functools.partial(_mix_prompt_kernel, tile=tile, d_rnn=c_w)
    return pl.pallas_call(
        kern,
        grid=(batch, nt),
        in_specs=[
            row_spec,
            _full((1, d)),
            _full(lw["w_in"].shape),
            _full(lw["w4"].shape), _full(lw["b4"].shape),
            _full(lw["wa"].shape), _full(lw["ba"].shape),
            _full(lw["wx"].shape), _full(lw["bx"].shape),
            _full(lw["lam"].shape),
            _full(lw["w31"].shape), _full(lw["b31"].shape),
            _full(lw["ln_g"].shape), _full(lw["ln_b"].shape),
            _full(lw["w_out"].shape),
        ],
        out_specs=[
            row_spec,
            pl.BlockSpec((None, 1, c_w), lambda b, j: (b, 0, 0)),
            pl.BlockSpec((None, RG_CONV_W - 1, c_w), lambda b, j: (b, 0, 0)),
            pl.BlockSpec((None, CF_CONV_W - 1, c_w), lambda b, j: (b, 0, 0)),
        ],
        out_shape=[
            jax.ShapeDtypeStruct(x2d.shape, F32),
            jax.ShapeDtypeStruct((batch, 1, c_w), F32),
            jax.ShapeDtypeStruct((batch, RG_CONV_W - 1, c_w), F32),
            jax.ShapeDtypeStruct((batch, CF_CONV_W - 1, c_w), F32),
        ],
        scratch_shapes=[
            pltpu.VMEM((RG_HIST + tile, c_w), F32),
            pltpu.VMEM((CF_HIST + tile + SUBLANES, c_w), F32),
            pltpu.VMEM((SUBLANES, c_w), F32),
            pltpu.VMEM((tile, c_w), F32),
            pltpu.VMEM((tile, c_w), F32),
            pltpu.VMEM((tile, c_w), F32),
            pltpu.VMEM((tile, c_w), F32),
            pltpu.VMEM((tile + SUBLANES, c_w), F32),
            pltpu.VMEM((tile, 2 * c_w), BF16),
        ],
        compiler_params=pltpu.CompilerParams(
            dimension_semantics=("arbitrary", "arbitrary"), vmem_limit_bytes=VMEM_LIMIT),
        name="mix_prompt",
    )(x2d, lw["g_mix"], lw["w_in"], lw["w4"], lw["b4"], lw["wa"], lw["ba"], lw["wx"], lw["bx"],
      lw["lam"], lw["w31"], lw["b31"], lw["ln_g"], lw["ln_b"], lw["w_out"])


def _mix_sample_kernel(x_ref, h0_ref, rb_ref, cb_ref, g_ref, win_ref, w4_ref, b4_ref, wa_ref, ba_ref,
                       wx_ref, bx_ref, lam_ref, w31_ref, b31_ref, lng_ref, lnb_ref, wout_ref,
                       xo_ref, h_ref, xr_ref, c_ref,
                       gr_s, xc_s, p_s, q_s, cat_s, *, steps, batch, d_rnn):
    c_w = d_rnn
    xn = _rms(x_ref[...], g_ref[...]).astype(BF16)
    xr_ref[...] = _dot(xn, win_ref[:, 0:c_w])
    gr_s[...] = _dot(xn, win_ref[:, c_w:2 * c_w])
    c_ref[...] = (
        _dot(xn, win_ref[:, 2 * c_w:3 * c_w]) * _sigmoid(_dot(xn, win_ref[:, 3 * c_w:4 * c_w])))

    def rows_of(t, s):
        return pl.ds(pl.multiple_of(t * batch + s, SUBLANES), SUBLANES)

    def conv4_blk(bi, carry):
        s = bi * SUBLANES
        for t in range(steps):
            acc = b4_ref[...]
            for k in range(RG_CONV_W):
                jj = t + k
                if jj < RG_CONV_W - 1:
                    src = rb_ref[jj, pl.ds(pl.multiple_of(s, SUBLANES), SUBLANES), :]
                else:
                    src = xr_ref[rows_of(jj - (RG_CONV_W - 1), s), :]
                acc = acc + src * w4_ref[k]
            xc_s[rows_of(t, s), :] = acc
        return carry

    lax.fori_loop(0, batch // SUBLANES, conv4_blk, 0)

    xcb = xc_s[...].astype(BF16)
    p_s[...] = _dot(xcb, wa_ref[...])
    q_s[...] = _dot(xcb, wx_ref[...])
    nsp = -LRU_C * _softplus(-lam_ref[...])

    def scan_blk(bi, carry):
        s = bi * SUBLANES
        b_rows = pl.ds(pl.multiple_of(s, SUBLANES), SUBLANES)
        h = h0_ref[b_rows, :]
        for t in range(steps):
            rows = rows_of(t, s)
            a, b = _lru_coeffs(p_s[rows, :], q_s[rows, :], xc_s[rows, :], ba_ref[...], bx_ref[...], nsp)
            h = a * h + b
            cat_s[rows, 0:c_w] = h * _gelu_tanh(gr_s[rows, :])
        h_ref[b_rows, :] = h
        return carry

    lax.fori_loop(0, batch // SUBLANES, scan_blk, 0)

    def conv31_blk(bi, carry):
        s = bi * SUBLANES
        for t in range(steps):
            acc = b31_ref[...]
            for k in range(CF_CONV_W):
                jj = t + k
                if jj < CF_CONV_W - 1:
                    src = cb_ref[jj, pl.ds(pl.multiple_of(s, SUBLANES), SUBLANES), :]
                else:
                    src = c_ref[rows_of(jj - (CF_CONV_W - 1), s), :]
                acc = acc + src * w31_ref[k]
            cat_s[rows_of(t, s), c_w:2 * c_w] = _ln_silu(acc, lng_ref[...], lnb_ref[...])
        return carry

    lax.fori_loop(0, batch // SUBLANES, conv31_blk, 0)

    xo_ref[...] = x_ref[...] + _dot(cat_s[...].astype(BF16), wout_ref[...])


def _mix_sample(x_tm, h0, rb_tm, cb_tm, lw, *, steps, batch):
    rows, d = x_tm.shape
    c_w = lw["ba"].shape[1]
    kern = functools.partial(_mix_sample_kernel, steps=steps, batch=batch, d_rnn=c_w)
    operands = (x_tm, h0, rb_tm, cb_tm, lw["g_mix"], lw["w_in"], lw["w4"], lw["b4"], lw["wa"], lw["ba"],
                lw["wx"], lw["bx"], lw["lam"], lw["w31"], lw["b31"], lw["ln_g"], lw["ln_b"], lw["w_out"])
    return pl.pallas_call(
        kern,
        grid=(1,),
        in_specs=[_full(o.shape) for o in operands],
        out_specs=[_full((rows, d)), _full((batch, c_w)), _full((rows, c_w)), _full((rows, c_w))],
        out_shape=[
            jax.ShapeDtypeStruct((rows, d), F32),
            jax.ShapeDtypeStruct((batch, c_w), F32),
            jax.ShapeDtypeStruct((rows, c_w), F32),
            jax.ShapeDtypeStruct((rows, c_w), F32),
        ],
        scratch_shapes=[
            pltpu.VMEM((rows, c_w), F32),
            pltpu.VMEM((rows, c_w), F32),
            pltpu.VMEM((rows, c_w), F32),
            pltpu.VMEM((rows, c_w), F32),
            pltpu.VMEM((rows, 2 * c_w), F32),
        ],
        compiler_params=pltpu.CompilerParams(
            dimension_semantics=("arbitrary",), vmem_limit_bytes=VMEM_LIMIT),
        name="mix_sample",
    )(*operands)


def _attend(q, k, v, heads, head_dim):
    outs = []
    for h in range(heads):
        sl = slice(h * head_dim, (h + 1) * head_dim)
        s = lax.dot_general(q[:, sl], k[:, sl], (((1,), (1,)), ((), ())), preferred_element_type=F32)
        p = jnp.exp(s - jnp.max(s, axis=-1, keepdims=True))
        inv = 1.0 / jnp.sum(p, axis=-1, keepdims=True)
        outs.append(_dot(p.astype(BF16), v[:, sl]) * inv)
    return outs


def _attn_prompt_kernel(x_ref, k_ref, v_ref, g_ref, wq_ref, wo_ref, xo_ref, o_s, *, heads, head_dim):
    xn = _rms(x_ref[...], g_ref[...]).astype(BF16)
    q = (_dot(xn, wq_ref[...]) * (head_dim ** -0.5)).astype(BF16)
    outs = _attend(q, k_ref[...], v_ref[...], heads, head_dim)
    for h, o in enumerate(outs):
        o_s[:, h * head_dim:(h + 1) * head_dim] = o.astype(BF16)
    xo_ref[...] = x_ref[...] + _dot(o_s[...], wo_ref[...])


def _attn_prompt(x2d, kb, vb, layer, g, wq, wo, *, batch, seq, n_mem, heads, tile):
    d = x2d.shape[1]
    nt = seq // tile
    row_spec = pl.BlockSpec((tile, d), lambda b, j: (b * nt + j, 0))
    kv_spec = pl.BlockSpec((None, n_mem, d), lambda b, j: (layer, b, 0))
    kern = functools.partial(_attn_prompt_kernel, heads=heads, head_dim=d // heads)
    return pl.pallas_call(
        kern,
        grid=(batch, nt),
        in_specs=[row_spec, kv_spec, kv_spec, _full((1, d)), _full(wq.shape), _full(wo.shape)],
        out_specs=row_spec,
        out_shape=jax.ShapeDtypeStruct(x2d.shape, F32),
        scratch_shapes=[pltpu.VMEM((tile, d), BF16)],
        compiler_params=pltpu.CompilerParams(
            dimension_semantics=("arbitrary", "arbitrary"), vmem_limit_bytes=VMEM_LIMIT),
        name="attn_prompt",
    )(x2d, kb, vb, g, wq, wo)


def _attn_sample_kernel(x_ref, k_ref, v_ref, g_ref, wq_ref, wo_ref, xo_ref, q_s, o_s,
                        *, heads, head_dim, nb, pad_t):
    i = pl.program_id(0)

    @pl.when(i == 0)
    def _():
        xn = _rms(x_ref[...], g_ref[...]).astype(BF16)
        q_s[...] = _dot(xn, wq_ref[...]) * (head_dim ** -0.5)

    n_kv = k_ref.shape[1] * heads
    col_head = lax.broadcasted_iota(jnp.int32, (heads * pad_t, n_kv), 1) % heads
    row_head = lax.broadcasted_iota(jnp.int32, (heads * pad_t, n_kv), 0) // pad_t
    own_head = col_head == row_head
    for bb in range(nb):
        rows = pl.ds(pl.multiple_of((i * nb + bb) * pad_t, pad_t), pad_t)
        q = q_s[rows, :]
        qs = jnp.concatenate([q[:, h * head_dim:(h + 1) * head_dim] for h in range(heads)], axis=0)
        k2 = k_ref[bb].reshape(n_kv, head_dim).astype(BF16)
        v2 = v_ref[bb].reshape(n_kv, head_dim).astype(BF16)
        s = lax.dot_general(qs.astype(BF16), k2, (((1,), (1,)), ((), ())), preferred_element_type=F32)
        s = jnp.where(own_head, s, -1e30)
        p = jnp.exp(s - jnp.max(s, axis=-1, keepdims=True))
        inv = 1.0 / jnp.sum(p, axis=-1, keepdims=True)
        o = _dot(p.astype(BF16), v2) * inv
        for h in range(heads):
            o_s[rows, h * head_dim:(h + 1) * head_dim] = o[h * pad_t:(h + 1) * pad_t, :]

    @pl.when(i == pl.num_programs(0) - 1)
    def _():
        xo_ref[...] = x_ref[...] + _dot(o_s[...].astype(BF16), wo_ref[...])


def _attn_sample(x_pad, cache_k, cache_v, layer, g, wq, wo, *, batch, n_mem, heads, nb, pad_t):
    rows, d = x_pad.shape
    kv_spec = pl.BlockSpec((None, nb, n_mem, heads, d // heads), lambda i: (layer, i, 0, 0, 0))
    kern = functools.partial(_attn_sample_kernel, heads=heads, head_dim=d // heads, nb=nb, pad_t=pad_t)
    return pl.pallas_call(
        kern,
        grid=(batch // nb,),
        in_specs=[_full((rows, d)), kv_spec, kv_spec, _full((1, d)), _full(wq.shape), _full(wo.shape)],
        out_specs=_full((rows, d)),
        out_shape=jax.ShapeDtypeStruct((rows, d), F32),
        scratch_shapes=[pltpu.VMEM((rows, d), F32), pltpu.VMEM((rows, d), F32)],
        compiler_params=pltpu.CompilerParams(
            dimension_semantics=("arbitrary",), vmem_limit_bytes=VMEM_LIMIT),
        name="attn_sample",
    )(x_pad, cache_k, cache_v, g, wq, wo)


def _ffn_kernel(x_ref, g_ref, wg_ref, wu_ref, wd_ref, gf_ref, xo_ref, h_s, *, col_chunks, final_norm):
    x = x_ref[...]
    z = _rms(x, g_ref[...]).astype(BF16)
    for lo, hi in col_chunks:
        h_s[:, lo:hi] = (_silu(_dot(z, wg_ref[:, lo:hi])) * _dot(z, wu_ref[:, lo:hi])).astype(BF16)
    y = x + _dot(h_s[...], wd_ref[...])
    if final_norm:
        y = _rms(y, gf_ref[...])
    xo_ref[...] = y


def _ffn(x2d, g, wg, wu, wd, gf, *, tile, final_norm):
    rows, d = x2d.shape
    d_ff = wg.shape[1]
    step = 512
    col_chunks = tuple((lo, min(lo + step, d_ff)) for lo in range(0, d_ff, step))
    row_spec = pl.BlockSpec((tile, d), lambda i: (i, 0))

    def resident(shape):
        return pl.BlockSpec(shape, lambda i: (0,) * len(shape), pipeline_mode=pl.Buffered(1))

    kern = functools.partial(_ffn_kernel, col_chunks=col_chunks, final_norm=final_norm)
    return pl.pallas_call(
        kern,
        grid=(rows // tile,),
        in_specs=[row_spec, _full((1, d)), resident(wg.shape), resident(wu.shape), resident(wd.shape),
                  _full((1, d))],
        out_specs=row_spec,
        out_shape=jax.ShapeDtypeStruct((rows, d), F32),
        scratch_shapes=[pltpu.VMEM((tile, d_ff), BF16)],
        compiler_params=pltpu.CompilerParams(
            dimension_semantics=("arbitrary",), vmem_limit_bytes=VMEM_LIMIT),
        name="ffn",
    )(x2d, g, wg, wu, wd, gf)


def _rows8(v):
    return jnp.broadcast_to(v[..., None, :], v.shape[:-1] + (SUBLANES, v.shape[-1]))


def _block_diag(w):
    h, i, j = w.shape
    eye = jnp.eye(h, dtype=w.dtype)
    return jnp.einsum("hij,hg->higj", w, eye).reshape(h * i, h * j)


def kernel(x_prompt, x_sample, state_rglru_h, state_rglru_conv, state_conf_conv, cache_mem_k, cache_mem_v, mem_prompt, norm_mix_g, w_in, rg_conv_w, rg_conv_b, rg_wa, rg_ba, rg_wx, rg_bx, rg_lambda, cf_conv_w, cf_conv_b, cf_ln_g, cf_ln_b, w_out, norm_attn_g, norm_mem_g, w_q, w_k, w_v, w_o, norm_ffn_g, w_gate, w_up, w_down, norm_final_g):
    batch, seq, d = x_prompt.shape
    dec_batch, dec_seq, _ = x_sample.shape
    depth = w_in.shape[0]
    n_mem = mem_prompt.shape[1]
    heads = cache_mem_k.shape[3]
    pad_t = SUBLANES

    w_in_b = w_in.astype(BF16)
    w_out_b = w_out.astype(BF16)
    w_q_b = w_q.astype(BF16)
    w_o_b = w_o.astype(BF16)
    w_gate_b = w_gate.astype(BF16)
    w_up_b = w_up.astype(BF16)
    w_down_b = w_down.astype(BF16)
    gf = norm_final_g[None, :]

    def layer_weights(l):
        return {
            "g_mix": norm_mix_g[l][None, :],
            "w_in": w_in_b[l],
            "w4": _rows8(rg_conv_w[l]), "b4": _rows8(rg_conv_b[l]),
            "wa": _block_diag(rg_wa[l]).astype(BF16), "ba": _rows8(rg_ba[l]),
            "wx": _block_diag(rg_wx[l]).astype(BF16), "bx": _rows8(rg_bx[l]),
            "lam": _rows8(rg_lambda[l]),
            "w31": _rows8(cf_conv_w[l]), "b31": _rows8(cf_conv_b[l]),
            "ln_g": _rows8(cf_ln_g[l]), "ln_b": _rows8(cf_ln_b[l]),
            "w_out": w_out_b[l],
        }

    mem_k, mem_v, mem_kb, mem_vb = _memkv(
        mem_prompt.reshape(batch * n_mem, d), norm_mem_g[:, None, :],
        w_k.astype(BF16), w_v.astype(BF16), tile=512)

    xp = x_prompt.reshape(batch * seq, d)
    xs_tm = jnp.transpose(x_sample, (1, 0, 2)).reshape(dec_seq * dec_batch, d)
    xs_bm = None
    p_h, p_rg, p_cf, s_h, s_rg, s_cf = [], [], [], [], [], []
    for l in range(depth):
        lw = layer_weights(l)
        g_attn = norm_attn_g[l][None, :]
        g_ffn = norm_ffn_g[l][None, :]
        last = l == depth - 1

        xp, hp, rgp, cfp = _mix_prompt(xp, lw, batch=batch, seq=seq, tile=512)
        p_h.append(hp[:, 0, :]); p_rg.append(rgp); p_cf.append(cfp)
        xp = _attn_prompt(xp, mem_kb, mem_vb, l, g_attn, w_q_b[l], w_o_b[l],
                          batch=batch, seq=seq, n_mem=n_mem, heads=heads, tile=512)
        xp = _ffn(xp, g_ffn, w_gate_b[l], w_up_b[l], w_down_b[l], gf, tile=512, final_norm=last)

        rb_tm = jnp.transpose(state_rglru_conv[l], (1, 0, 2))
        cb_tm = jnp.transpose(state_conf_conv[l], (1, 0, 2))
        xs_tm, hs, xr_tm, c_tm = _mix_sample(xs_tm, state_rglru_h[l], rb_tm, cb_tm, lw,
                                             steps=dec_seq, batch=dec_batch)
        s_h.append(hs)
        xr_bm = jnp.transpose(xr_tm.reshape(dec_seq, dec_batch, -1), (1, 0, 2))
        c_bm = jnp.transpose(c_tm.reshape(dec_seq, dec_batch, -1), (1, 0, 2))
        s_rg.append(jnp.concatenate([state_rglru_conv[l], xr_bm], axis=1)[:, -(RG_CONV_W - 1):])
        s_cf.append(jnp.concatenate([state_conf_conv[l], c_bm], axis=1)[:, -(CF_CONV_W - 1):])

        xs_pad = jnp.pad(jnp.transpose(xs_tm.reshape(dec_seq, dec_batch, d), (1, 0, 2)),
                         ((0, 0), (0, pad_t - dec_seq), (0, 0))).reshape(dec_batch * pad_t, d)
        xs_pad = _attn_sample(xs_pad, cache_mem_k, cache_mem_v, l, g_attn, w_q_b[l], w_o_b[l],
                              batch=dec_batch, n_mem=n_mem, heads=heads, nb=4, pad_t=pad_t)
        xs_bm = xs_pad.reshape(dec_batch, pad_t, d)[:, :dec_seq].reshape(dec_batch * dec_seq, d)
        xs_bm = _ffn(xs_bm, g_ffn, w_gate_b[l], w_up_b[l], w_down_b[l], gf, tile=512, final_norm=last)
        if not last:
            xs_tm = jnp.transpose(xs_bm.reshape(dec_batch, dec_seq, d), (1, 0, 2)).reshape(
                dec_seq * dec_batch, d)

    kv_shape = (depth, batch, n_mem, heads, d // heads)
    return (xp.reshape(batch, seq, d), xs_bm.reshape(dec_batch, dec_seq, d),
            jnp.stack(p_h), jnp.stack(p_rg), jnp.stack(p_cf),
            mem_k.reshape(kv_shape), mem_v.reshape(kv_shape),
            jnp.stack(s_h), jnp.stack(s_rg), jnp.stack(s_cf))
```

```python
import functools
import math

import jax
import jax.numpy as jnp
from jax import lax
from jax.experimental import pallas as pl
from jax.experimental.pallas import tpu as pltpu

F32 = jnp.float32
BF16 = jnp.bfloat16

EPS = 1e-6
LRU_C = 8.0
RG_CONV_W = 4
CF_CONV_W = 31
SUBLANES = 8
CF_HIST = 32
RG_HIST = SUBLANES
CHUNK = 16
OUT_CHUNK = 32
CONV_LANES = 256
CONV_GROUPS_PER_SEGMENT = 6
VEC_COST_SCALE = 1.3
FFN_COLS = 512
PROMPT_TILE = 256
VMEM_LIMIT = 56 * 1024 * 1024


def _dot(a, b):
    return jnp.dot(a, b, preferred_element_type=F32)


def _rms(x, g):
    return x * lax.rsqrt(jnp.mean(x * x, axis=-1, keepdims=True) + EPS) * g


def _gelu_tanh(x):
    return 0.5 * x * (1.0 + jnp.tanh(math.sqrt(2.0 / math.pi) * (x + 0.044715 * (x * x * x))))


def _sigmoid(x):
    return 0.5 * (jnp.tanh(0.5 * x) + 1.0)


def _silu(x):
    return x * _sigmoid(x)


def _softplus(z):
    return jnp.maximum(z, 0.0) + jnp.log1p(jnp.exp(-jnp.abs(z)))


def _lru_coeffs(r_pre, i_pre, xc, ba, bx, nsp):
    r = _sigmoid(r_pre + ba)
    i = _sigmoid(i_pre + bx)
    log_a = r * nsp
    a = jnp.exp(log_a)
    t = jnp.tanh(log_a)
    mult = jnp.sqrt(-2.0 * t / (1.0 - t))
    return a, mult * (i * xc)


def _ln_silu(y, g, b):
    mu = jnp.mean(y, axis=-1, keepdims=True)
    yc = y - mu
    var = jnp.mean(yc * yc, axis=-1, keepdims=True)
    return _silu(yc * lax.rsqrt(var + EPS) * g + b)


def _attend(q, k, v, heads, head_dim):
    outs = []
    for h in range(heads):
        sl = slice(h * head_dim, (h + 1) * head_dim)
        s = lax.dot_general(q[:, sl], k[:, sl], (((1,), (1,)), ((), ())), preferred_element_type=F32)
        p = jnp.exp(s - jnp.max(s, axis=-1, keepdims=True))
        inv = 1.0 / jnp.sum(p, axis=-1, keepdims=True)
        outs.append(_dot(p.astype(BF16), v[:, sl]) * inv)
    return outs


def _ffn_cols(d_ff):
    return tuple((lo, min(lo + FFN_COLS, d_ff)) for lo in range(0, d_ff, FFN_COLS))


def _swiglu(x, g, wg_ref, wu_ref, wd_ref, h_s):
    z = _rms(x, g).astype(BF16)
    for lo, hi in _ffn_cols(wg_ref.shape[1]):
        h_s[:, lo:hi] = (_silu(_dot(z, wg_ref[:, lo:hi])) * _dot(z, wu_ref[:, lo:hi])).astype(BF16)
    return x + _dot(h_s[...], wd_ref[...])


def _full(shape):
    return pl.BlockSpec(shape, lambda *_: (0,) * len(shape))


def _layer_spec(arr, layer):
    zeros = (0,) * (arr.ndim - 1)
    return pl.BlockSpec((None,) + arr.shape[1:], lambda *_: (layer,) + zeros, pipeline_mode=pl.Buffered(1))


def _memkv_kernel(mem_ref, g_ref, wk_ref, wv_ref, k_ref, v_ref, kb_ref, vb_ref):
    m = _rms(mem_ref[...], g_ref[...]).astype(BF16)
    k = _dot(m, wk_ref[...])
    v = _dot(m, wv_ref[...])
    k_ref[...] = k
    v_ref[...] = v
    kb_ref[...] = k.astype(BF16)
    vb_ref[...] = v.astype(BF16)


def _memkv(mem2d, g, wk, wv, tile):
    depth, d, _ = wk.shape
    rows = mem2d.shape[0]
    row_spec = pl.BlockSpec((None, tile, d), lambda l, i: (l, i, 0))
    w_spec = pl.BlockSpec((None, d, d), lambda l, i: (l, 0, 0))
    return pl.pallas_call(
        _memkv_kernel,
        grid=(depth, rows // tile),
        in_specs=[
            pl.BlockSpec((tile, d), lambda l, i: (i, 0)),
            pl.BlockSpec((None, 1, d), lambda l, i: (l, 0, 0)),
            w_spec, w_spec,
        ],
        out_specs=[row_spec, row_spec, row_spec, row_spec],
        out_shape=[
            jax.ShapeDtypeStruct((depth, rows, d), F32),
            jax.ShapeDtypeStruct((depth, rows, d), F32),
            jax.ShapeDtypeStruct((depth, rows, d), BF16),
            jax.ShapeDtypeStruct((depth, rows, d), BF16),
        ],
        compiler_params=pltpu.CompilerParams(
            dimension_semantics=("arbitrary", "arbitrary"), vmem_limit_bytes=VMEM_LIMIT),
        name="memkv",
    )(mem2d, g, wk, wv)


MIX_PARAMS = ("g_mix", "w_in", "w4", "b4", "wa", "ba", "wx", "bx", "lam", "w31", "b31", "ln_g", "ln_b")
BLOCK_PARAMS = ("w_out", "g_attn", "w_q", "w_o", "g_ffn", "w_gate", "w_up", "w_down")


def _group_rows(gi, offset=0):
    return pl.ds(gi * SUBLANES + offset, SUBLANES)


def _mix_stage(x_ref, prm, xr_hist, c_hist, hcar, gr_s, xc_s, p_s, q_s, yc_s, cat_s, *, tile, c_w):
    win_ref = prm["w_in"]
    n_groups = tile // SUBLANES
    sub8 = lax.broadcasted_iota(jnp.int32, (SUBLANES, c_w), 0)

    xn = _rms(x_ref[...], prm["g_mix"][...]).astype(BF16)
    c_hist[pl.ds(CF_HIST, tile), :] = (
        _dot(xn, win_ref[:, 2 * c_w:3 * c_w]) * _sigmoid(_dot(xn, win_ref[:, 3 * c_w:4 * c_w])))
    yield 1100
    xr_hist[pl.ds(RG_HIST, tile), :] = _dot(xn, win_ref[:, 0:c_w])
    gr_s[...] = _dot(xn, win_ref[:, c_w:2 * c_w])
    yield 1100

    first_tap_off = CF_HIST - (CF_CONV_W - 1)
    n_hist_groups = CF_HIST // SUBLANES + 1
    w31_ref = prm["w31"]
    for lane0 in range(0, c_w, CONV_LANES):
        lanes = pl.ds(lane0, CONV_LANES)
        sub_l = lax.broadcasted_iota(jnp.int32, (SUBLANES, CONV_LANES), 0)
        zero = jnp.zeros((SUBLANES, CONV_LANES), F32)
        p0_prev, rolled_prev = zero, (zero,) * (SUBLANES - 1)
        for si in range(n_groups + 1):
            hist = [c_hist[_group_rows(si, SUBLANES * q), lanes] for q in range(n_hist_groups)]
            parts = []
            for res in range(SUBLANES):
                part = None
                for q in range(n_hist_groups):
                    k = SUBLANES * q + res - first_tap_off
                    if 0 <= k < CF_CONV_W:
                        term = hist[q] * w31_ref[k, :, lanes]
                        part = term if part is None else part + term
                parts.append(part)
            rolled = tuple(pltpu.roll(parts[res], SUBLANES - res, axis=0) for res in range(1, SUBLANES))
            y = p0_prev
            for res in range(1, SUBLANES):
                y = y + jnp.where(sub_l < SUBLANES - res, rolled_prev[res - 1], rolled[res - 1])
            yc_s[_group_rows(si), lanes] = y
            p0_prev, rolled_prev = parts[0], rolled
            if si % CONV_GROUPS_PER_SEGMENT == CONV_GROUPS_PER_SEGMENT - 1:
                yield VEC_COST_SCALE * 52 * CONV_GROUPS_PER_SEGMENT

    w4_ref = prm["w4"]
    x_hist = xr_hist[pl.ds(0, RG_HIST), :]
    prev_rolled = [pltpu.roll(x_hist, dly, axis=0) for dly in range(1, RG_CONV_W)]
    for gi in range(n_groups):
        xg = xr_hist[_group_rows(gi, RG_HIST), :]
        acc = xg * w4_ref[RG_CONV_W - 1] + prm["b4"][...]
        rolled = []
        for dly in range(1, RG_CONV_W):
            rolled.append(pltpu.roll(xg, dly, axis=0))
            acc = acc + jnp.where(sub8 >= dly, rolled[-1], prev_rolled[dly - 1]) * w4_ref[RG_CONV_W - 1 - dly]
        xc_s[_group_rows(gi), :] = acc
        prev_rolled = rolled
    yield VEC_COST_SCALE * 20 * n_groups

    xcb = xc_s[...].astype(BF16)
    p_s[...] = _dot(xcb, prm["wa"][...])
    q_s[...] = _dot(xcb, prm["wx"][...])
    yield 520

    nsp = -LRU_C * _softplus(-prm["lam"][...])
    n_grp = CHUNK // SUBLANES
    grp = (n_grp, SUBLANES, c_w)
    sub = lax.broadcasted_iota(jnp.int32, grp, 1)
    for ci in range(tile // CHUNK):
        rows = pl.ds(ci * CHUNK, CHUNK)
        a, b = _lru_coeffs(p_s[rows, :].reshape(grp), q_s[rows, :].reshape(grp),
                           xc_s[rows, :].reshape(grp), prm["ba"][...], prm["bx"][...], nsp)
        for dly in (1, 2, 4):
            ra = pltpu.roll(a, dly, axis=1)
            rb = pltpu.roll(b, dly, axis=1)
            keep = sub >= dly
            b = a * jnp.where(keep, rb, 0.0) + b
            a = a * jnp.where(keep, ra, 1.0)
        p_s[rows, :] = a.reshape(CHUNK, c_w)
        q_s[rows, :] = b.reshape(CHUNK, c_w)
        if ci % 2 == 1:
            yield VEC_COST_SCALE * 240

    hb = hcar[...]
    for gi in range(n_groups):
        rows = _group_rows(gi)
        h = p_s[rows, :] * hb + q_s[rows, :]
        q_s[rows, :] = h
        hb = jnp.broadcast_to(h[SUBLANES - 1:SUBLANES, :], (SUBLANES, c_w))
    hcar[...] = hb
    yield VEC_COST_SCALE * 21 * n_groups

    for ci in range(tile // OUT_CHUNK):
        rows = pl.ds(ci * OUT_CHUNK, OUT_CHUNK)
        y_r = q_s[rows, :] * _gelu_tanh(gr_s[rows, :])
        cat_s[rows, 0:c_w] = y_r.astype(BF16)
        conv = yc_s[pl.ds(ci * OUT_CHUNK + SUBLANES, OUT_CHUNK), :].reshape(
            OUT_CHUNK // SUBLANES, SUBLANES, c_w)
        y_c = _ln_silu(conv + prm["b31"][...], prm["ln_g"][...], prm["ln_b"][...])
        cat_s[rows, c_w:2 * c_w] = y_c.reshape(OUT_CHUNK, c_w).astype(BF16)
        yield VEC_COST_SCALE * 220


def _block_stage(x_ref, cat_s, k_ref, v_ref, prm, gf_ref, xo_ref, xb_s, zb_s, qb_s, o_s, h_s,
                 *, heads, final_norm):
    d = x_ref.shape[1]
    head_dim = d // heads
    xb_s[...] = x_ref[...] + _dot(cat_s[...], prm["w_out"][...])
    yield 600
    zb_s[...] = _rms(xb_s[...], prm["g_attn"][...]).astype(BF16)
    qb_s[...] = (_dot(zb_s[...], prm["w_q"][...]) * (head_dim ** -0.5)).astype(BF16)
    yield 700
    for h in range(heads):
        sl = slice(h * head_dim, (h + 1) * head_dim)
        (o,) = _attend(qb_s[:, sl], k_ref[:, sl], v_ref[:, sl], 1, head_dim)
        o_s[:, sl] = o.astype(BF16)
        yield 300
    xb_s[...] = xb_s[...] + _dot(o_s[...], prm["w_o"][...])
    yield 600
    zb_s[...] = _rms(xb_s[...], prm["g_ffn"][...]).astype(BF16)
    yield 150
    wg_ref, wu_ref, wd_ref = prm["w_gate"], prm["w_up"], prm["w_down"]
    for lo, hi in _ffn_cols(wg_ref.shape[1]):
        h_s[:, lo:hi] = (_silu(_dot(zb_s[...], wg_ref[:, lo:hi])) * _dot(zb_s[...], wu_ref[:, lo:hi])
                         ).astype(BF16)
        yield 2 * (hi - lo) + 80
    for lo, hi in _ffn_cols(d):
        xo_ref[:, lo:hi] = xb_s[:, lo:hi] + _dot(h_s[...], wd_ref[:, lo:hi])
        yield 1450
    if final_norm:
        xo_ref[...] = _rms(xo_ref[...], gf_ref[...])
        yield 150


def _interleave(*stages):
    live = [[0, i, st] for i, st in enumerate(stages)]
    while live:
        entry = min(live)
        try:
            entry[0] += next(entry[2])
        except StopIteration:
            live.remove(entry)


def _layer_prompt_kernel(*refs, tile, tiles_per_seq, c_w, heads, final_norm):
    n_mix, n_blk = len(MIX_PARAMS), len(BLOCK_PARAMS)
    xa_ref, xb_ref, k_ref, v_ref = refs[:4]
    mix_prm = dict(zip(MIX_PARAMS, refs[4:4 + n_mix]))
    blk_prm = dict(zip(BLOCK_PARAMS, refs[4 + n_mix:4 + n_mix + n_blk]))
    gf_ref = refs[4 + n_mix + n_blk]
    xo_ref, h_ref, rg_ref, cf_ref = refs[5 + n_mix + n_blk:9 + n_mix + n_blk]
    (xr_hist, c_hist, hcar, gr_s, xc_s, p_s, q_s, yc_s, cat_s,
     xb_s, zb_s, qb_s, o_s, h_s) = refs[9 + n_mix + n_blk:]

    g = pl.program_id(0)
    pos = g % tiles_per_seq

    @pl.when(g == 0)
    def _():
        cat_s[...] = jnp.zeros(cat_s.shape, BF16)
        c_hist[pl.ds(CF_HIST + tile, SUBLANES), :] = jnp.zeros((SUBLANES, c_w), F32)

    @pl.when(pos == 0)
    def _():
        xr_hist[pl.ds(0, RG_HIST), :] = jnp.zeros((RG_HIST, c_w), F32)
        c_hist[pl.ds(0, CF_HIST), :] = jnp.zeros((CF_HIST, c_w), F32)
        hcar[...] = jnp.zeros((SUBLANES, c_w), F32)

    _interleave(
        _block_stage(xb_ref, cat_s, k_ref, v_ref, blk_prm, gf_ref, xo_ref, xb_s, zb_s, qb_s, o_s, h_s,
                     heads=heads, final_norm=final_norm),
        _mix_stage(xa_ref, mix_prm, xr_hist, c_hist, hcar, gr_s, xc_s, p_s, q_s, yc_s, cat_s,
                   tile=tile, c_w=c_w))

    @pl.when(pos == tiles_per_seq - 1)
    def _():
        h_ref[...] = hcar[0:1, :]
        rg_ref[...] = xr_hist[pl.ds(RG_HIST + tile - (RG_CONV_W - 1), RG_CONV_W - 1), :]
        cf_ref[...] = c_hist[pl.ds(CF_HIST + tile - (CF_CONV_W - 1), CF_CONV_W - 1), :]

    xr_hist[pl.ds(0, RG_HIST), :] = xr_hist[pl.ds(tile, RG_HIST), :]
    c_hist[pl.ds(0, CF_HIST), :] = c_hist[pl.ds(tile, CF_HIST), :]


def _layer_prompt(x2d, kb, vb, lw, gf, layer, *, batch, seq, n_mem, heads, final_norm):
    rows, d = x2d.shape
    tile = PROMPT_TILE
    c_w = lw["ba"].shape[-1]
    d_ff = lw["w_gate"].shape[-1]
    tps = seq // tile
    n_tiles = batch * tps

    def mix_tile(g):
        return jnp.minimum(g, n_tiles - 1)

    def blk_tile(g):
        return jnp.maximum(g - 1, 0)

    kv_spec = pl.BlockSpec((None, n_mem, d), lambda g: (layer, blk_tile(g) // tps, 0))
    params = [lw[name] for name in MIX_PARAMS + BLOCK_PARAMS]
    kern = functools.partial(_layer_prompt_kernel, tile=tile, tiles_per_seq=tps, c_w=c_w,
                             heads=heads, final_norm=final_norm)
    return pl.pallas_call(
        kern,
        grid=(n_tiles + 1,),
        in_specs=[
            pl.BlockSpec((tile, d), lambda g: (mix_tile(g), 0)),
            pl.BlockSpec((tile, d), lambda g: (blk_tile(g), 0)),
            kv_spec, kv_spec,
            *[_layer_spec(p, layer) for p in params],
            _full(gf.shape),
        ],
        out_specs=[
            pl.BlockSpec((tile, d), lambda g: (blk_tile(g), 0)),
            pl.BlockSpec((None, 1, c_w), lambda g: (mix_tile(g) // tps, 0, 0)),
            pl.BlockSpec((None, RG_CONV_W - 1, c_w), lambda g: (mix_tile(g) // tps, 0, 0)),
            pl.BlockSpec((None, CF_CONV_W - 1, c_w), lambda g: (mix_tile(g) // tps, 0, 0)),
        ],
        out_shape=[
            jax.ShapeDtypeStruct((rows, d), F32),
            jax.ShapeDtypeStruct((batch, 1, c_w), F32),
            jax.ShapeDtypeStruct((batch, RG_CONV_W - 1, c_w), F32),
            jax.ShapeDtypeStruct((batch, CF_CONV_W - 1, c_w), F32),
        ],
        scratch_shapes=[
            pltpu.VMEM((RG_HIST + tile, c_w), F32),
            pltpu.VMEM((CF_HIST + tile + SUBLANES, c_w), F32),
            pltpu.VMEM((SUBLANES, c_w), F32),
            pltpu.VMEM((tile, c_w), F32),
            pltpu.VMEM((tile, c_w), F32),
            pltpu.VMEM((tile, c_w), F32),
            pltpu.VMEM((tile, c_w), F32),
            pltpu.VMEM((tile + SUBLANES, c_w), F32),
            pltpu.VMEM((tile, 2 * c_w), BF16),
            pltpu.VMEM((tile, d), F32),
            pltpu.VMEM((tile, d), BF16),
            pltpu.VMEM((tile, d), BF16),
            pltpu.VMEM((tile, d), BF16),
            pltpu.VMEM((tile, d_ff), BF16),
        ],
        compiler_params=pltpu.CompilerParams(
            dimension_semantics=("arbitrary",), vmem_limit_bytes=VMEM_LIMIT),
        name="layer_prompt",
    )(x2d, x2d, kb, vb, *params, gf)


def _mix_sample_kernel(x_ref, h0_ref, rb_ref, cb_ref, g_ref, win_ref, w4_ref, b4_ref, wa_ref, ba_ref,
                       wx_ref, bx_ref, lam_ref, w31_ref, b31_ref, lng_ref, lnb_ref, wout_ref,
                       xo_ref, h_ref, xr_ref, c_ref,
                       gr_s, xc_s, p_s, q_s, cat_s, *, steps, batch, d_rnn):
    c_w = d_rnn
    xn = _rms(x_ref[...], g_ref[...]).astype(BF16)
    xr_ref[...] = _dot(xn, win_ref[:, 0:c_w])
    gr_s[...] = _dot(xn, win_ref[:, c_w:2 * c_w])
    c_ref[...] = (
        _dot(xn, win_ref[:, 2 * c_w:3 * c_w]) * _sigmoid(_dot(xn, win_ref[:, 3 * c_w:4 * c_w])))

    def rows_of(t, s):
        return pl.ds(pl.multiple_of(t * batch + s, SUBLANES), SUBLANES)

    def conv4_blk(bi, carry):
        s = bi * SUBLANES
        for t in range(steps):
            acc = b4_ref[...]
            for k in range(RG_CONV_W):
                jj = t + k
                if jj < RG_CONV_W - 1:
                    src = rb_ref[jj, pl.ds(pl.multiple_of(s, SUBLANES), SUBLANES), :]
                else:
                    src = xr_ref[rows_of(jj - (RG_CONV_W - 1), s), :]
                acc = acc + src * w4_ref[k]
            xc_s[rows_of(t, s), :] = acc
        return carry

    lax.fori_loop(0, batch // SUBLANES, conv4_blk, 0)

    xcb = xc_s[...].astype(BF16)
    p_s[...] = _dot(xcb, wa_ref[...])
    q_s[...] = _dot(xcb, wx_ref[...])
    nsp = -LRU_C * _softplus(-lam_ref[...])

    def scan_blk(bi, carry):
        s = bi * SUBLANES
        b_rows = pl.ds(pl.multiple_of(s, SUBLANES), SUBLANES)
        h = h0_ref[b_rows, :]
        for t in range(steps):
            rows = rows_of(t, s)
            a, b = _lru_coeffs(p_s[rows, :], q_s[rows, :], xc_s[rows, :], ba_ref[...], bx_ref[...], nsp)
            h = a * h + b
            cat_s[rows, 0:c_w] = h * _gelu_tanh(gr_s[rows, :])
        h_ref[b_rows, :] = h
        return carry

    lax.fori_loop(0, batch // SUBLANES, scan_blk, 0)

    def conv31_blk(bi, carry):
        s = bi * SUBLANES
        for t in range(steps):
            acc = b31_ref[...]
            for k in range(CF_CONV_W):
                jj = t + k
                if jj < CF_CONV_W - 1:
                    src = cb_ref[jj, pl.ds(pl.multiple_of(s, SUBLANES), SUBLANES), :]
                else:
                    src = c_ref[rows_of(jj - (CF_CONV_W - 1), s), :]
                acc = acc + src * w31_ref[k]
            cat_s[rows_of(t, s), c_w:2 * c_w] = _ln_silu(acc, lng_ref[...], lnb_ref[...])
        return carry

    lax.fori_loop(0, batch // SUBLANES, conv31_blk, 0)

    xo_ref[...] = x_ref[...] + _dot(cat_s[...].astype(BF16), wout_ref[...])


def _mix_sample(x_tm, h0, rb_tm, cb_tm, lw, layer, *, steps, batch):
    rows, d = x_tm.shape
    c_w = lw["ba"].shape[-1]
    kern = functools.partial(_mix_sample_kernel, steps=steps, batch=batch, d_rnn=c_w)
    params = [lw[name] for name in MIX_PARAMS + ("w_out",)]
    return pl.pallas_call(
        kern,
        grid=(1,),
        in_specs=[_full(x_tm.shape), _layer_spec(h0, layer), _full(rb_tm.shape), _full(cb_tm.shape),
                  *[_layer_spec(p, layer) for p in params]],
        out_specs=[_full((rows, d)), _full((batch, c_w)), _full((rows, c_w)), _full((rows, c_w))],
        out_shape=[
            jax.ShapeDtypeStruct((rows, d), F32),
            jax.ShapeDtypeStruct((batch, c_w), F32),
            jax.ShapeDtypeStruct((rows, c_w), F32),
            jax.ShapeDtypeStruct((rows, c_w), F32),
        ],
        scratch_shapes=[
            pltpu.VMEM((rows, c_w), F32),
            pltpu.VMEM((rows, c_w), F32),
            pltpu.VMEM((rows, c_w), F32),
            pltpu.VMEM((rows, c_w), F32),
            pltpu.VMEM((rows, 2 * c_w), F32),
        ],
        compiler_params=pltpu.CompilerParams(
            dimension_semantics=("arbitrary",), vmem_limit_bytes=VMEM_LIMIT),
        name="mix_sample",
    )(x_tm, h0, rb_tm, cb_tm, *params)


def _attn_sample_kernel(x_ref, k_ref, v_ref, g_ref, wq_ref, wo_ref, xo_ref, q_s, o_s,
                        *, heads, head_dim, nb, pad_t):
    i = pl.program_id(0)

    @pl.when(i == 0)
    def _():
        xn = _rms(x_ref[...], g_ref[...]).astype(BF16)
        q_s[...] = _dot(xn, wq_ref[...]) * (head_dim ** -0.5)

    n_kv = k_ref.shape[1] * heads
    col_head = lax.broadcasted_iota(jnp.int32, (heads * pad_t, n_kv), 1) % heads
    row_head = lax.broadcasted_iota(jnp.int32, (heads * pad_t, n_kv), 0) // pad_t
    own_head = col_head == row_head
    for bb in range(nb):
        rows = pl.ds(pl.multiple_of((i * nb + bb) * pad_t, pad_t), pad_t)
        q = q_s[rows, :]
        qs = jnp.concatenate([q[:, h * head_dim:(h + 1) * head_dim] for h in range(heads)], axis=0)
        k2 = k_ref[bb].reshape(n_kv, head_dim).astype(BF16)
        v2 = v_ref[bb].reshape(n_kv, head_dim).astype(BF16)
        s = lax.dot_general(qs.astype(BF16), k2, (((1,), (1,)), ((), ())), preferred_element_type=F32)
        s = jnp.where(own_head, s, -1e30)
        p = jnp.exp(s - jnp.max(s, axis=-1, keepdims=True))
        inv = 1.0 / jnp.sum(p, axis=-1, keepdims=True)
        o = _dot(p.astype(BF16), v2) * inv
        for h in range(heads):
            o_s[rows, h * head_dim:(h + 1) * head_dim] = o[h * pad_t:(h + 1) * pad_t, :]

    @pl.when(i == pl.num_programs(0) - 1)
    def _():
        xo_ref[...] = x_ref[...] + _dot(o_s[...].astype(BF16), wo_ref[...])


def _attn_sample(x_pad, cache_k, cache_v, lw, layer, *, batch, n_mem, heads, nb, pad_t):
    rows, d = x_pad.shape
    kv_spec = pl.BlockSpec((None, nb, n_mem, heads, d // heads), lambda i: (layer, i, 0, 0, 0))
    kern = functools.partial(_attn_sample_kernel, heads=heads, head_dim=d // heads, nb=nb, pad_t=pad_t)
    return pl.pallas_call(
        kern,
        grid=(batch // nb,),
        in_specs=[_full((rows, d)), kv_spec, kv_spec, _layer_spec(lw["g_attn"], layer),
                  _layer_spec(lw["w_q"], layer), _layer_spec(lw["w_o"], layer)],
        out_specs=_full((rows, d)),
        out_shape=jax.ShapeDtypeStruct((rows, d), F32),
        scratch_shapes=[pltpu.VMEM((rows, d), F32), pltpu.VMEM((rows, d), F32)],
        compiler_params=pltpu.CompilerParams(
            dimension_semantics=("arbitrary",), vmem_limit_bytes=VMEM_LIMIT),
        name="attn_sample",
    )(x_pad, cache_k, cache_v, lw["g_attn"], lw["w_q"], lw["w_o"])


def _ffn_kernel(x_ref, g_ref, wg_ref, wu_ref, wd_ref, gf_ref, xo_ref, h_s, *, final_norm):
    y = _swiglu(x_ref[...], g_ref[...], wg_ref, wu_ref, wd_ref, h_s)
    if final_norm:
        y = _rms(y, gf_ref[...])
    xo_ref[...] = y


def _ffn(x2d, lw, gf, layer, *, tile, final_norm):
    rows, d = x2d.shape
    d_ff = lw["w_gate"].shape[-1]
    row_spec = pl.BlockSpec((tile, d), lambda i: (i, 0))
    kern = functools.partial(_ffn_kernel, final_norm=final_norm)
    return pl.pallas_call(
        kern,
        grid=(rows // tile,),
        in_specs=[row_spec, _layer_spec(lw["g_ffn"], layer), _layer_spec(lw["w_gate"], layer),
                  _layer_spec(lw["w_up"], layer), _layer_spec(lw["w_down"], layer), _full(gf.shape)],
        out_specs=row_spec,
        out_shape=jax.ShapeDtypeStruct((rows, d), F32),
        scratch_shapes=[pltpu.VMEM((tile, d_ff), BF16)],
        compiler_params=pltpu.CompilerParams(
            dimension_semantics=("arbitrary",), vmem_limit_bytes=VMEM_LIMIT),
        name="ffn",
    )(x2d, lw["g_ffn"], lw["w_gate"], lw["w_up"], lw["w_down"], gf)


def _rows8(v):
    return jnp.broadcast_to(v[..., None, :], v.shape[:-1] + (SUBLANES, v.shape[-1]))


def _block_diag(w):
    l, h, i, j = w.shape
    eye = jnp.eye(h, dtype=w.dtype)
    return jnp.einsum("lhij,hg->lhigj", w, eye).reshape(l, h * i, h * j)


def kernel(x_prompt, x_sample, state_rglru_h, state_rglru_conv, state_conf_conv, cache_mem_k, cache_mem_v, mem_prompt, norm_mix_g, w_in, rg_conv_w, rg_conv_b, rg_wa, rg_ba, rg_wx, rg_bx, rg_lambda, cf_conv_w, cf_conv_b, cf_ln_g, cf_ln_b, w_out, norm_attn_g, norm_mem_g, w_q, w_k, w_v, w_o, norm_ffn_g, w_gate, w_up, w_down, norm_final_g):
    batch, seq, d = x_prompt.shape
    dec_batch, dec_seq, _ = x_sample.shape
    depth = w_in.shape[0]
    n_mem = mem_prompt.shape[1]
    heads = cache_mem_k.shape[3]
    pad_t = SUBLANES

    lw = {
        "g_mix": norm_mix_g[:, None, :],
        "w_in": w_in.astype(BF16),
        "w4": _rows8(rg_conv_w), "b4": _rows8(rg_conv_b),
        "wa": _block_diag(rg_wa).astype(BF16), "ba": _rows8(rg_ba),
        "wx": _block_diag(rg_wx).astype(BF16), "bx": _rows8(rg_bx),
        "lam": _rows8(rg_lambda),
        "w31": _rows8(cf_conv_w), "b31": _rows8(cf_conv_b),
        "ln_g": _rows8(cf_ln_g), "ln_b": _rows8(cf_ln_b),
        "w_out": w_out.astype(BF16),
        "g_attn": norm_attn_g[:, None, :],
        "w_q": w_q.astype(BF16), "w_o": w_o.astype(BF16),
        "g_ffn": norm_ffn_g[:, None, :],
        "w_gate": w_gate.astype(BF16), "w_up": w_up.astype(BF16), "w_down": w_down.astype(BF16),
    }
    gf = norm_final_g[None, :]

    mem_k, mem_v, mem_kb, mem_vb = _memkv(
        mem_prompt.reshape(batch * n_mem, d), norm_mem_g[:, None, :],
        w_k.astype(BF16), w_v.astype(BF16), tile=512)

    xp = x_prompt.reshape(batch * seq, d)
    xs_tm = jnp.transpose(x_sample, (1, 0, 2)).reshape(dec_seq * dec_batch, d)
    xs_bm = None
    p_h, p_rg, p_cf, s_h, s_rg, s_cf = [], [], [], [], [], []
    for l in range(depth):
        last = l == depth - 1

        xp, hp, rgp, cfp = _layer_prompt(xp, mem_kb, mem_vb, lw, gf, l, batch=batch, seq=seq,
                                         n_mem=n_mem, heads=heads, final_norm=last)
        p_h.append(hp[:, 0, :]); p_rg.append(rgp); p_cf.append(cfp)

        rb_tm = jnp.transpose(state_rglru_conv[l], (1, 0, 2))
        cb_tm = jnp.transpose(state_conf_conv[l], (1, 0, 2))
        xs_tm, hs, xr_tm, c_tm = _mix_sample(xs_tm, state_rglru_h, rb_tm, cb_tm, lw, l,
                                             steps=dec_seq, batch=dec_batch)
        s_h.append(hs)
        xr_bm = jnp.transpose(xr_tm.reshape(dec_seq, dec_batch, -1), (1, 0, 2))
        c_bm = jnp.transpose(c_tm.reshape(dec_seq, dec_batch, -1), (1, 0, 2))
        s_rg.append(jnp.concatenate([state_rglru_conv[l], xr_bm], axis=1)[:, -(RG_CONV_W - 1):])
        s_cf.append(jnp.concatenate([state_conf_conv[l], c_bm], axis=1)[:, -(CF_CONV_W - 1):])

        xs_pad = jnp.pad(jnp.transpose(xs_tm.reshape(dec_seq, dec_batch, d), (1, 0, 2)),
                         ((0, 0), (0, pad_t - dec_seq), (0, 0))).reshape(dec_batch * pad_t, d)
        xs_pad = _attn_sample(xs_pad, cache_mem_k, cache_mem_v, lw, l,
                              batch=dec_batch, n_mem=n_mem, heads=heads, nb=4, pad_t=pad_t)
        xs_bm = xs_pad.reshape(dec_batch, pad_t, d)[:, :dec_seq].reshape(dec_batch * dec_seq, d)
        xs_bm = _ffn(xs_bm, lw, gf, l, tile=512, final_norm=last)
        if not last:
            xs_tm = jnp.transpose(xs_bm.reshape(dec_batch, dec_seq, d), (1, 0, 2)).reshape(
                dec_seq * dec_batch, d)

    kv_shape = (depth, batch, n_mem, heads, d // heads)
    return (xp.reshape(batch, seq, d), xs_bm.reshape(dec_batch, dec_seq, d),
            jnp.stack(p_h), jnp.stack(p_rg), jnp.stack(p_cf),
            mem_k.reshape(kv_shape), mem_v.reshape(kv_shape),
            jnp.stack(s_h), jnp.stack(s_rg), jnp.stack(s_cf))
```

```python
import functools
import math

import jax
import jax.numpy as jnp
from jax import lax
from jax.experimental import pallas as pl
from jax.experimental.pallas import tpu as pltpu

F32 = jnp.float32
BF16 = jnp.bfloat16

EPS = 1e-6
LRU_C = 8.0
RG_CONV_W = 4
CF_CONV_W = 31
SUBLANES = 8
CF_HIST = 32
RG_HIST = SUBLANES
CHUNK = 16
OUT_CHUNK = 32
CONV_LANES = 256
CONV_GROUPS_PER_SEGMENT = 6
FFN_COLS = 512
PROMPT_TILE = 256
VMEM_LIMIT = 56 * 1024 * 1024


def _dot(a, b):
    return jnp.dot(a, b, preferred_element_type=F32)


def _rms(x, g):
    return x * lax.rsqrt(jnp.mean(x * x, axis=-1, keepdims=True) + EPS) * g


def _gelu_tanh(x):
    return 0.5 * x * (1.0 + jnp.tanh(math.sqrt(2.0 / math.pi) * (x + 0.044715 * (x * x * x))))


def _sigmoid(x):
    return 0.5 * (jnp.tanh(0.5 * x) + 1.0)


def _silu(x):
    return x * _sigmoid(x)


def _softplus(z):
    return jnp.maximum(z, 0.0) + jnp.log1p(jnp.exp(-jnp.abs(z)))


def _lru_coeffs(r_pre, i_pre, xc, ba, bx, nsp):
    r = _sigmoid(r_pre + ba)
    i = _sigmoid(i_pre + bx)
    log_a = r * nsp
    a = jnp.exp(log_a)
    t = jnp.tanh(log_a)
    mult = jnp.sqrt(-2.0 * t / (1.0 - t))
    return a, mult * (i * xc)


def _ln_silu(y, g, b):
    mu = jnp.mean(y, axis=-1, keepdims=True)
    yc = y - mu
    var = jnp.mean(yc * yc, axis=-1, keepdims=True)
    return _silu(yc * lax.rsqrt(var + EPS) * g + b)


def _attend(q, k, v, heads, head_dim):
    outs = []
    for h in range(heads):
        sl = slice(h * head_dim, (h + 1) * head_dim)
        s = lax.dot_general(q[:, sl], k[:, sl], (((1,), (1,)), ((), ())), preferred_element_type=F32)
        p = jnp.exp(s - jnp.max(s, axis=-1, keepdims=True))
        inv = 1.0 / jnp.sum(p, axis=-1, keepdims=True)
        outs.append(_dot(p.astype(BF16), v[:, sl]) * inv)
    return outs


def _ffn_cols(d_ff):
    return tuple((lo, min(lo + FFN_COLS, d_ff)) for lo in range(0, d_ff, FFN_COLS))


def _swiglu(x, g, wg_ref, wu_ref, wd_ref, h_s):
    z = _rms(x, g).astype(BF16)
    for lo, hi in _ffn_cols(wg_ref.shape[1]):
        h_s[:, lo:hi] = (_silu(_dot(z, wg_ref[:, lo:hi])) * _dot(z, wu_ref[:, lo:hi])).astype(BF16)
    return x + _dot(h_s[...], wd_ref[...])


def _mxu_cost(rows, k, n):
    return (rows // 2) * pl.cdiv(k, 256) * pl.cdiv(n, 256) // 2


def _full(shape):
    return pl.BlockSpec(shape, lambda *_: (0,) * len(shape))


def _layer_spec(arr, layer):
    zeros = (0,) * (arr.ndim - 1)
    return pl.BlockSpec((None,) + arr.shape[1:], lambda *_: (layer,) + zeros, pipeline_mode=pl.Buffered(1))


def _memkv_kernel(mem_ref, g_ref, wk_ref, wv_ref, k_ref, v_ref, kb_ref, vb_ref):
    m = _rms(mem_ref[...], g_ref[...]).astype(BF16)
    k = _dot(m, wk_ref[...])
    v = _dot(m, wv_ref[...])
    k_ref[...] = k.reshape(k_ref.shape)
    v_ref[...] = v.reshape(v_ref.shape)
    kb_ref[...] = k.astype(BF16)
    vb_ref[...] = v.astype(BF16)


def _memkv(mem2d, g, wk, wv, tile, n_mem, heads):
    depth, d, _ = wk.shape
    rows = mem2d.shape[0]
    row_spec = pl.BlockSpec((None, tile, d), lambda l, i: (l, i, 0))
    out5_spec = pl.BlockSpec((None, tile // n_mem, n_mem, heads, d // heads), lambda l, i: (l, i, 0, 0, 0))
    out5_shape = jax.ShapeDtypeStruct((depth, rows // n_mem, n_mem, heads, d // heads), F32)
    w_spec = pl.BlockSpec((None, d, d), lambda l, i: (l, 0, 0))
    return pl.pallas_call(
        _memkv_kernel,
        grid=(depth, rows // tile),
        in_specs=[
            pl.BlockSpec((tile, d), lambda l, i: (i, 0)),
            pl.BlockSpec((None, 1, d), lambda l, i: (l, 0, 0)),
            w_spec, w_spec,
        ],
        out_specs=[out5_spec, out5_spec, row_spec, row_spec],
        out_shape=[
            out5_shape,
            out5_shape,
            jax.ShapeDtypeStruct((depth, rows, d), BF16),
            jax.ShapeDtypeStruct((depth, rows, d), BF16),
        ],
        compiler_params=pltpu.CompilerParams(
            dimension_semantics=("arbitrary", "arbitrary"), vmem_limit_bytes=VMEM_LIMIT),
        name="memkv",
    )(mem2d, g, wk, wv)


MIX_PARAMS = ("g_mix", "w_in", "w4", "b4", "wa", "ba", "wx", "bx", "lam", "w31", "b31", "ln_g", "ln_b")
BLOCK_PARAMS = ("w_out", "g_attn", "w_q", "w_o", "g_ffn", "w_gate", "w_up", "w_down")


def _group_rows(gi, offset=0):
    return pl.ds(gi * SUBLANES + offset, SUBLANES)


def _mix_stage(x_ref, prm, xr_hist, c_hist, hcar, gr_s, xc_s, p_s, q_s, yc_s, xn_s, cat_s, *, tile, c_w):
    win_ref = prm["w_in"]
    n_groups = tile // SUBLANES
    sub8 = lax.broadcasted_iota(jnp.int32, (SUBLANES, c_w), 0)

    yield 2 * _mxu_cost(tile, win_ref.shape[0], c_w), 2 * tile
    xn_s[...] = _rms(x_ref[...], prm["g_mix"][...]).astype(BF16)
    c_hist[pl.ds(CF_HIST, tile), :] = (
        _dot(xn_s[...], win_ref[:, 2 * c_w:3 * c_w]) * _sigmoid(_dot(xn_s[...], win_ref[:, 3 * c_w:4 * c_w])))
    yield 2 * _mxu_cost(tile, win_ref.shape[0], c_w), tile // 4
    xr_hist[pl.ds(RG_HIST, tile), :] = _dot(xn_s[...], win_ref[:, 0:c_w])
    gr_s[...] = _dot(xn_s[...], win_ref[:, c_w:2 * c_w])

    first_tap_off = CF_HIST - (CF_CONV_W - 1)
    n_hist_groups = CF_HIST // SUBLANES + 1
    w31_ref = prm["w31"]
    for lane0 in range(0, c_w, CONV_LANES):
        lanes = pl.ds(lane0, CONV_LANES)
        sub_l = lax.broadcasted_iota(jnp.int32, (SUBLANES, CONV_LANES), 0)
        zero = jnp.zeros((SUBLANES, CONV_LANES), F32)
        p0_prev, rolled_prev = zero, (zero,) * (SUBLANES - 1)
        for si in range(n_groups + 1):
            if si % CONV_GROUPS_PER_SEGMENT == 0:
                yield 0, 26 * (CONV_LANES // 128) * min(CONV_GROUPS_PER_SEGMENT, n_groups + 1 - si)
            hist = [c_hist[_group_rows(si, SUBLANES * q), lanes] for q in range(n_hist_groups)]
            parts = []
            for res in range(SUBLANES):
                part = None
                for q in range(n_hist_groups):
                    k = SUBLANES * q + res - first_tap_off
                    if 0 <= k < CF_CONV_W:
                        term = hist[q] * w31_ref[k, :, lanes]
                        part = term if part is None else part + term
                parts.append(part)
            rolled = tuple(pltpu.roll(parts[res], SUBLANES - res, axis=0) for res in range(1, SUBLANES))
            y = p0_prev
            for res in range(1, SUBLANES):
                y = y + jnp.where(sub_l < SUBLANES - res, rolled_prev[res - 1], rolled[res - 1])
            yc_s[_group_rows(si), lanes] = y
            p0_prev, rolled_prev = parts[0], rolled

    yield 0, 20 * n_groups
    w4_ref = prm["w4"]
    x_hist = xr_hist[pl.ds(0, RG_HIST), :]
    prev_rolled = [pltpu.roll(x_hist, dly, axis=0) for dly in range(1, RG_CONV_W)]
    for gi in range(n_groups):
        xg = xr_hist[_group_rows(gi, RG_HIST), :]
        acc = xg * w4_ref[RG_CONV_W - 1] + prm["b4"][...]
        rolled = []
        for dly in range(1, RG_CONV_W):
            rolled.append(pltpu.roll(xg, dly, axis=0))
            acc = acc + jnp.where(sub8 >= dly, rolled[-1], prev_rolled[dly - 1]) * w4_ref[RG_CONV_W - 1 - dly]
        xc_s[_group_rows(gi), :] = acc
        prev_rolled = rolled

    yield 2 * _mxu_cost(tile, c_w, c_w), tile // 4
    xcb_s = xn_s.at[:, 0:c_w]
    xcb_s[...] = xc_s[...].astype(BF16)
    p_s[...] = _dot(xcb_s[...], prm["wa"][...])
    q_s[...] = _dot(xcb_s[...], prm["wx"][...])

    nsp = -LRU_C * _softplus(-prm["lam"][...])
    n_grp = CHUNK // SUBLANES
    grp = (n_grp, SUBLANES, c_w)
    sub = lax.broadcasted_iota(jnp.int32, grp, 1)
    for ci in range(tile // CHUNK):
        if ci % 2 == 0:
            yield 0, 240
        rows = pl.ds(ci * CHUNK, CHUNK)
        a, b = _lru_coeffs(p_s[rows, :].reshape(grp), q_s[rows, :].reshape(grp),
                           xc_s[rows, :].reshape(grp), prm["ba"][...], prm["bx"][...], nsp)
        for dly in (1, 2, 4):
            ra = pltpu.roll(a, dly, axis=1)
            rb = pltpu.roll(b, dly, axis=1)
            keep = sub >= dly
            b = a * jnp.where(keep, rb, 0.0) + b
            a = a * jnp.where(keep, ra, 1.0)
        p_s[rows, :] = a.reshape(CHUNK, c_w)
        q_s[rows, :] = b.reshape(CHUNK, c_w)

    yield 0, 21 * n_groups
    hb = hcar[...]
    for gi in range(n_groups):
        rows = _group_rows(gi)
        h = p_s[rows, :] * hb + q_s[rows, :]
        q_s[rows, :] = h
        hb = jnp.broadcast_to(h[SUBLANES - 1:SUBLANES, :], (SUBLANES, c_w))
    hcar[...] = hb

    for ci in range(tile // OUT_CHUNK):
        yield 0, 220
        rows = pl.ds(ci * OUT_CHUNK, OUT_CHUNK)
        y_r = q_s[rows, :] * _gelu_tanh(gr_s[rows, :])
        cat_s[rows, 0:c_w] = y_r.astype(BF16)
        conv = yc_s[pl.ds(ci * OUT_CHUNK + SUBLANES, OUT_CHUNK), :].reshape(
            OUT_CHUNK // SUBLANES, SUBLANES, c_w)
        y_c = _ln_silu(conv + prm["b31"][...], prm["ln_g"][...], prm["ln_b"][...])
        cat_s[rows, c_w:2 * c_w] = y_c.reshape(OUT_CHUNK, c_w).astype(BF16)


def _block_stage(x_ref, cat_s, k_ref, v_ref, prm, gf_ref, xo_ref, xb_s, zb_s, qb_s, o_s, h_s,
                 *, heads, final_norm):
    d = x_ref.shape[1]
    head_dim = d // heads
    rows = x_ref.shape[0]
    yield _mxu_cost(rows, d, d), rows // 2
    xb_s[...] = x_ref[...] + _dot(cat_s[...], prm["w_out"][...])
    yield _mxu_cost(rows, d, d), 2 * rows
    zb_s[...] = _rms(xb_s[...], prm["g_attn"][...]).astype(BF16)
    qb_s[...] = (_dot(zb_s[...], prm["w_q"][...]) * (head_dim ** -0.5)).astype(BF16)
    for h in range(heads):
        yield rows // 2, rows
        sl = slice(h * head_dim, (h + 1) * head_dim)
        (o,) = _attend(qb_s[:, sl], k_ref[:, sl], v_ref[:, sl], 1, head_dim)
        o_s[:, sl] = o.astype(BF16)
    yield _mxu_cost(rows, d, d), rows // 2
    xb_s[...] = xb_s[...] + _dot(o_s[...], prm["w_o"][...])
    yield 0, rows
    zb_s[...] = _rms(xb_s[...], prm["g_ffn"][...]).astype(BF16)
    wg_ref, wu_ref, wd_ref = prm["w_gate"], prm["w_up"], prm["w_down"]
    for lo, hi in _ffn_cols(wg_ref.shape[1]):
        yield 2 * _mxu_cost(rows, d, hi - lo), (hi - lo) // 2
        h_s[:, lo:hi] = (_silu(_dot(zb_s[...], wg_ref[:, lo:hi])) * _dot(zb_s[...], wu_ref[:, lo:hi])
                         ).astype(BF16)
    for lo, hi in _ffn_cols(d):
        yield _mxu_cost(rows, wd_ref.shape[0], hi - lo), rows // 2
        xo_ref[:, lo:hi] = xb_s[:, lo:hi] + _dot(h_s[...], wd_ref[:, lo:hi])
    if final_norm:
        yield 0, rows
        xo_ref[...] = _rms(xo_ref[...], gf_ref[...])


def _interleave(first, second):
    pending = {first: next(first), second: next(second)}
    clock = {first: 0, second: 1}
    while pending:
        stage = min(pending, key=clock.get)
        clock[stage] += max(pending.pop(stage))
        try:
            pending[stage] = next(stage)
        except StopIteration:
            pass


def _layer_prompt_kernel(*refs, tile, tiles_per_seq, c_w, heads, final_norm):
    n_mix, n_blk = len(MIX_PARAMS), len(BLOCK_PARAMS)
    xa_ref, xb_ref, k_ref, v_ref = refs[:4]
    mix_prm = dict(zip(MIX_PARAMS, refs[4:4 + n_mix]))
    blk_prm = dict(zip(BLOCK_PARAMS, refs[4 + n_mix:4 + n_mix + n_blk]))
    gf_ref = refs[4 + n_mix + n_blk]
    xo_ref, h_ref, rg_ref, cf_ref = refs[5 + n_mix + n_blk:9 + n_mix + n_blk]
    (xr_hist, c_hist, hcar, gr_s, xc_s, p_s, q_s, yc_s, xn_s, cat_s,
     xb_s, zb_s, qb_s, o_s, h_s) = refs[9 + n_mix + n_blk:]

    g = pl.program_id(0)
    pos = g % tiles_per_seq

    @pl.when(g == 0)
    def _():
        cat_s[...] = jnp.zeros(cat_s.shape, BF16)
        c_hist[pl.ds(CF_HIST + tile, SUBLANES), :] = jnp.zeros((SUBLANES, c_w), F32)

    @pl.when(pos == 0)
    def _():
        xr_hist[pl.ds(0, RG_HIST), :] = jnp.zeros((RG_HIST, c_w), F32)
        c_hist[pl.ds(0, CF_HIST), :] = jnp.zeros((CF_HIST, c_w), F32)
        hcar[...] = jnp.zeros((SUBLANES, c_w), F32)

    _interleave(
        _block_stage(xb_ref, cat_s, k_ref, v_ref, blk_prm, gf_ref, xo_ref, xb_s, zb_s, qb_s, o_s, h_s,
                     heads=heads, final_norm=final_norm),
        _mix_stage(xa_ref, mix_prm, xr_hist, c_hist, hcar, gr_s, xc_s, p_s, q_s, yc_s, xn_s, cat_s,
                   tile=tile, c_w=c_w))

    @pl.when(pos == tiles_per_seq - 1)
    def _():
        h_ref[...] = hcar[0:1, :]
        rg_ref[...] = xr_hist[pl.ds(RG_HIST + tile - (RG_CONV_W - 1), RG_CONV_W - 1), :]
        cf_ref[...] = c_hist[pl.ds(CF_HIST + tile - (CF_CONV_W - 1), CF_CONV_W - 1), :]

    xr_hist[pl.ds(0, RG_HIST), :] = xr_hist[pl.ds(tile, RG_HIST), :]
    c_hist[pl.ds(0, CF_HIST), :] = c_hist[pl.ds(tile, CF_HIST), :]


def _layer_prompt(x2d, kb, vb, lw, gf, layer, *, batch, seq, n_mem, heads, final_norm):
    rows, d = x2d.shape
    tile = PROMPT_TILE
    c_w = lw["ba"].shape[-1]
    d_ff = lw["w_gate"].shape[-1]
    tps = seq // tile
    n_tiles = batch * tps

    def mix_tile(g):
        return jnp.minimum(g, n_tiles - 1)

    def blk_tile(g):
        return jnp.maximum(g - 1, 0)

    kv_spec = pl.BlockSpec((None, n_mem, d), lambda g: (layer, blk_tile(g) // tps, 0))
    params = [lw[name] for name in MIX_PARAMS + BLOCK_PARAMS]
    kern = functools.partial(_layer_prompt_kernel, tile=tile, tiles_per_seq=tps, c_w=c_w,
                             heads=heads, final_norm=final_norm)
    return pl.pallas_call(
        kern,
        grid=(n_tiles + 1,),
        in_specs=[
            pl.BlockSpec((tile, d), lambda g: (mix_tile(g), 0)),
            pl.BlockSpec((tile, d), lambda g: (blk_tile(g), 0)),
            kv_spec, kv_spec,
            *[_layer_spec(p, layer) for p in params],
            _full(gf.shape),
        ],
        out_specs=[
            pl.BlockSpec((tile, d), lambda g: (blk_tile(g), 0)),
            pl.BlockSpec((None, 1, c_w), lambda g: (mix_tile(g) // tps, 0, 0)),
            pl.BlockSpec((None, RG_CONV_W - 1, c_w), lambda g: (mix_tile(g) // tps, 0, 0)),
            pl.BlockSpec((None, CF_CONV_W - 1, c_w), lambda g: (mix_tile(g) // tps, 0, 0)),
        ],
        out_shape=[
            jax.ShapeDtypeStruct((rows, d), F32),
            jax.ShapeDtypeStruct((batch, 1, c_w), F32),
            jax.ShapeDtypeStruct((batch, RG_CONV_W - 1, c_w), F32),
            jax.ShapeDtypeStruct((batch, CF_CONV_W - 1, c_w), F32),
        ],
        scratch_shapes=[
            pltpu.VMEM((RG_HIST + tile, c_w), F32),
            pltpu.VMEM((CF_HIST + tile + SUBLANES, c_w), F32),
            pltpu.VMEM((SUBLANES, c_w), F32),
            pltpu.VMEM((tile, c_w), F32),
            pltpu.VMEM((tile, c_w), F32),
            pltpu.VMEM((tile, c_w), F32),
            pltpu.VMEM((tile, c_w), F32),
            pltpu.VMEM((tile + SUBLANES, c_w), F32),
            pltpu.VMEM((tile, d), BF16),
            pltpu.VMEM((tile, 2 * c_w), BF16),
            pltpu.VMEM((tile, d), F32),
            pltpu.VMEM((tile, d), BF16),
            pltpu.VMEM((tile, d), BF16),
            pltpu.VMEM((tile, d), BF16),
            pltpu.VMEM((tile, d_ff), BF16),
        ],
        compiler_params=pltpu.CompilerParams(
            dimension_semantics=("arbitrary",), vmem_limit_bytes=VMEM_LIMIT),
        name="layer_prompt",
    )(x2d, x2d, kb, vb, *params, gf)


def _mix_sample_kernel(x_ref, h0_ref, rb_ref, cb_ref, g_ref, win_ref, w4_ref, b4_ref, wa_ref, ba_ref,
                       wx_ref, bx_ref, lam_ref, w31_ref, b31_ref, lng_ref, lnb_ref, wout_ref,
                       xo_ref, h_ref, xr_ref, c_ref,
                       gr_s, xc_s, p_s, q_s, cat_s, *, steps, batch, d_rnn):
    c_w = d_rnn
    xn = _rms(x_ref[...], g_ref[...]).astype(BF16)
    xr_ref[...] = _dot(xn, win_ref[:, 0:c_w])
    gr_s[...] = _dot(xn, win_ref[:, c_w:2 * c_w])
    c_ref[...] = (
        _dot(xn, win_ref[:, 2 * c_w:3 * c_w]) * _sigmoid(_dot(xn, win_ref[:, 3 * c_w:4 * c_w])))

    def rows_of(t, s):
        return pl.ds(pl.multiple_of(t * batch + s, SUBLANES), SUBLANES)

    def conv4_blk(bi, carry):
        s = bi * SUBLANES
        for t in range(steps):
            acc = b4_ref[...]
            for k in range(RG_CONV_W):
                jj = t + k
                if jj < RG_CONV_W - 1:
                    src = rb_ref[jj, pl.ds(pl.multiple_of(s, SUBLANES), SUBLANES), :]
                else:
                    src = xr_ref[rows_of(jj - (RG_CONV_W - 1), s), :]
                acc = acc + src * w4_ref[k]
            xc_s[rows_of(t, s), :] = acc
        return carry

    lax.fori_loop(0, batch // SUBLANES, conv4_blk, 0)

    xcb = xc_s[...].astype(BF16)
    p_s[...] = _dot(xcb, wa_ref[...])
    q_s[...] = _dot(xcb, wx_ref[...])
    nsp = -LRU_C * _softplus(-lam_ref[...])

    def scan_blk(bi, carry):
        s = bi * SUBLANES
        b_rows = pl.ds(pl.multiple_of(s, SUBLANES), SUBLANES)
        h = h0_ref[b_rows, :]
        for t in range(steps):
            rows = rows_of(t, s)
            a, b = _lru_coeffs(p_s[rows, :], q_s[rows, :], xc_s[rows, :], ba_ref[...], bx_ref[...], nsp)
            h = a * h + b
            cat_s[rows, 0:c_w] = h * _gelu_tanh(gr_s[rows, :])
        h_ref[b_rows, :] = h
        return carry

    lax.fori_loop(0, batch // SUBLANES, scan_blk, 0)

    def conv31_blk(bi, carry):
        s = bi * SUBLANES
        for t in range(steps):
            acc = b31_ref[...]
            for k in range(CF_CONV_W):
                jj = t + k
                if jj < CF_CONV_W - 1:
                    src = cb_ref[jj, pl.ds(pl.multiple_of(s, SUBLANES), SUBLANES), :]
                else:
                    src = c_ref[rows_of(jj - (CF_CONV_W - 1), s), :]
                acc = acc + src * w31_ref[k]
            cat_s[rows_of(t, s), c_w:2 * c_w] = _ln_silu(acc, lng_ref[...], lnb_ref[...])
        return carry

    lax.fori_loop(0, batch // SUBLANES, conv31_blk, 0)

    xo_ref[...] = x_ref[...] + _dot(cat_s[...].astype(BF16), wout_ref[...])


def _mix_sample(x_tm, h0, rb_tm, cb_tm, lw, layer, *, steps, batch):
    rows, d = x_tm.shape
    c_w = lw["ba"].shape[-1]
    kern = functools.partial(_mix_sample_kernel, steps=steps, batch=batch, d_rnn=c_w)
    params = [lw[name] for name in MIX_PARAMS + ("w_out",)]
    return pl.pallas_call(
        kern,
        grid=(1,),
        in_specs=[_full(x_tm.shape), _layer_spec(h0, layer), _layer_spec(rb_tm, layer),
                  _layer_spec(cb_tm, layer),
                  *[_layer_spec(p, layer) for p in params]],
        out_specs=[_full((rows, d)), _full((batch, c_w)), _full((rows, c_w)), _full((rows, c_w))],
        out_shape=[
            jax.ShapeDtypeStruct((rows, d), F32),
            jax.ShapeDtypeStruct((batch, c_w), F32),
            jax.ShapeDtypeStruct((rows, c_w), F32),
            jax.ShapeDtypeStruct((rows, c_w), F32),
        ],
        scratch_shapes=[
            pltpu.VMEM((rows, c_w), F32),
            pltpu.VMEM((rows, c_w), F32),
            pltpu.VMEM((rows, c_w), F32),
            pltpu.VMEM((rows, c_w), F32),
            pltpu.VMEM((rows, 2 * c_w), F32),
        ],
        compiler_params=pltpu.CompilerParams(
            dimension_semantics=("arbitrary",), vmem_limit_bytes=VMEM_LIMIT),
        name="mix_sample",
    )(x_tm, h0, rb_tm, cb_tm, *params)


def _attn_sample_kernel(x_ref, k_ref, v_ref, g_ref, wq_ref, wo_ref, xo_ref, q_s, o_s,
                        *, heads, head_dim, nb, pad_t):
    i = pl.program_id(0)

    @pl.when(i == 0)
    def _():
        xn = _rms(x_ref[...], g_ref[...]).astype(BF16)
        q_s[...] = _dot(xn, wq_ref[...]) * (head_dim ** -0.5)

    n_kv = k_ref.shape[1] * heads
    col_head = lax.broadcasted_iota(jnp.int32, (heads * pad_t, n_kv), 1) % heads
    row_head = lax.broadcasted_iota(jnp.int32, (heads * pad_t, n_kv), 0) // pad_t
    own_head = col_head == row_head
    for bb in range(nb):
        rows = pl.ds(pl.multiple_of((i * nb + bb) * pad_t, pad_t), pad_t)
        q = q_s[rows, :]
        qs = jnp.concatenate([q[:, h * head_dim:(h + 1) * head_dim] for h in range(heads)], axis=0)
        k2 = k_ref[bb].reshape(n_kv, head_dim).astype(BF16)
        v2 = v_ref[bb].reshape(n_kv, head_dim).astype(BF16)
        s = lax.dot_general(qs.astype(BF16), k2, (((1,), (1,)), ((), ())), preferred_element_type=F32)
        s = jnp.where(own_head, s, -1e30)
        p = jnp.exp(s - jnp.max(s, axis=-1, keepdims=True))
        inv = 1.0 / jnp.sum(p, axis=-1, keepdims=True)
        o = _dot(p.astype(BF16), v2) * inv
        for h in range(heads):
            o_s[rows, h * head_dim:(h + 1) * head_dim] = o[h * pad_t:(h + 1) * pad_t, :]

    @pl.when(i == pl.num_programs(0) - 1)
    def _():
        xo_ref[...] = x_ref[...] + _dot(o_s[...].astype(BF16), wo_ref[...])


def _attn_sample(x_pad, cache_k, cache_v, lw, layer, *, batch, n_mem, heads, nb, pad_t):
    rows, d = x_pad.shape
    kv_spec = pl.BlockSpec((None, nb, n_mem, heads, d // heads), lambda i: (layer, i, 0, 0, 0))
    kern = functools.partial(_attn_sample_kernel, heads=heads, head_dim=d // heads, nb=nb, pad_t=pad_t)
    return pl.pallas_call(
        kern,
        grid=(batch // nb,),
        in_specs=[_full((rows, d)), kv_spec, kv_spec, _layer_spec(lw["g_attn"], layer),
                  _layer_spec(lw["w_q"], layer), _layer_spec(lw["w_o"], layer)],
        out_specs=_full((rows, d)),
        out_shape=jax.ShapeDtypeStruct((rows, d), F32),
        scratch_shapes=[pltpu.VMEM((rows, d), F32), pltpu.VMEM((rows, d), F32)],
        compiler_params=pltpu.CompilerParams(
            dimension_semantics=("arbitrary",), vmem_limit_bytes=VMEM_LIMIT),
        name="attn_sample",
    )(x_pad, cache_k, cache_v, lw["g_attn"], lw["w_q"], lw["w_o"])


def _ffn_kernel(x_ref, g_ref, wg_ref, wu_ref, wd_ref, gf_ref, xo_ref, h_s, *, final_norm):
    y = _swiglu(x_ref[...], g_ref[...], wg_ref, wu_ref, wd_ref, h_s)
    if final_norm:
        y = _rms(y, gf_ref[...])
    xo_ref[...] = y


def _ffn(x2d, lw, gf, layer, *, tile, final_norm):
    rows, d = x2d.shape
    d_ff = lw["w_gate"].shape[-1]
    row_spec = pl.BlockSpec((tile, d), lambda i: (i, 0))
    kern = functools.partial(_ffn_kernel, final_norm=final_norm)
    return pl.pallas_call(
        kern,
        grid=(rows // tile,),
        in_specs=[row_spec, _layer_spec(lw["g_ffn"], layer), _layer_spec(lw["w_gate"], layer),
                  _layer_spec(lw["w_up"], layer), _layer_spec(lw["w_down"], layer), _full(gf.shape)],
        out_specs=row_spec,
        out_shape=jax.ShapeDtypeStruct((rows, d), F32),
        scratch_shapes=[pltpu.VMEM((tile, d_ff), BF16)],
        compiler_params=pltpu.CompilerParams(
            dimension_semantics=("arbitrary",), vmem_limit_bytes=VMEM_LIMIT),
        name="ffn",
    )(x2d, lw["g_ffn"], lw["w_gate"], lw["w_up"], lw["w_down"], gf)


def _rows8(v):
    return jnp.broadcast_to(v[..., None, :], v.shape[:-1] + (SUBLANES, v.shape[-1]))


def _block_diag(w):
    l, h, i, j = w.shape
    eye = jnp.eye(h, dtype=w.dtype)
    return jnp.einsum("lhij,hg->lhigj", w, eye).reshape(l, h * i, h * j)


def kernel(x_prompt, x_sample, state_rglru_h, state_rglru_conv, state_conf_conv, cache_mem_k, cache_mem_v, mem_prompt, norm_mix_g, w_in, rg_conv_w, rg_conv_b, rg_wa, rg_ba, rg_wx, rg_bx, rg_lambda, cf_conv_w, cf_conv_b, cf_ln_g, cf_ln_b, w_out, norm_attn_g, norm_mem_g, w_q, w_k, w_v, w_o, norm_ffn_g, w_gate, w_up, w_down, norm_final_g):
    batch, seq, d = x_prompt.shape
    dec_batch, dec_seq, _ = x_sample.shape
    depth = w_in.shape[0]
    n_mem = mem_prompt.shape[1]
    heads = cache_mem_k.shape[3]
    pad_t = SUBLANES

    lw = {
        "g_mix": norm_mix_g[:, None, :],
        "w_in": w_in.astype(BF16),
        "w4": _rows8(rg_conv_w), "b4": _rows8(rg_conv_b),
        "wa": _block_diag(rg_wa).astype(BF16), "ba": _rows8(rg_ba),
        "wx": _block_diag(rg_wx).astype(BF16), "bx": _rows8(rg_bx),
        "lam": _rows8(rg_lambda),
        "w31": _rows8(cf_conv_w), "b31": _rows8(cf_conv_b),
        "ln_g": _rows8(cf_ln_g), "ln_b": _rows8(cf_ln_b),
        "w_out": w_out.astype(BF16),
        "g_attn": norm_attn_g[:, None, :],
        "w_q": w_q.astype(BF16), "w_o": w_o.astype(BF16),
        "g_ffn": norm_ffn_g[:, None, :],
        "w_gate": w_gate.astype(BF16), "w_up": w_up.astype(BF16), "w_down": w_down.astype(BF16),
    }
    gf = norm_final_g[None, :]

    mem_k, mem_v, mem_kb, mem_vb = _memkv(
        mem_prompt.reshape(batch * n_mem, d), norm_mem_g[:, None, :],
        w_k.astype(BF16), w_v.astype(BF16), tile=512, n_mem=n_mem, heads=heads)

    rb_tm = jnp.transpose(state_rglru_conv, (0, 2, 1, 3))
    cb_tm = jnp.transpose(state_conf_conv, (0, 2, 1, 3))

    xp = x_prompt.reshape(batch * seq, d)
    xs_tm = jnp.transpose(x_sample, (1, 0, 2)).reshape(dec_seq * dec_batch, d)
    xs_bm = None
    p_h, p_rg, p_cf, s_h, s_xr, s_c = [], [], [], [], [], []
    for l in range(depth):
        last = l == depth - 1

        xp, hp, rgp, cfp = _layer_prompt(xp, mem_kb, mem_vb, lw, gf, l, batch=batch, seq=seq,
                                         n_mem=n_mem, heads=heads, final_norm=last)
        p_h.append(hp[:, 0, :]); p_rg.append(rgp); p_cf.append(cfp)

        xs_tm, hs, xr_tm, c_tm = _mix_sample(xs_tm, state_rglru_h, rb_tm, cb_tm, lw, l,
                                             steps=dec_seq, batch=dec_batch)
        s_h.append(hs); s_xr.append(xr_tm); s_c.append(c_tm)

        xs_pad = jnp.pad(jnp.transpose(xs_tm.reshape(dec_seq, dec_batch, d), (1, 0, 2)),
                         ((0, 0), (0, pad_t - dec_seq), (0, 0))).reshape(dec_batch * pad_t, d)
        xs_pad = _attn_sample(xs_pad, cache_mem_k, cache_mem_v, lw, l,
                              batch=dec_batch, n_mem=n_mem, heads=heads, nb=4, pad_t=pad_t)
        xs_bm = xs_pad.reshape(dec_batch, pad_t, d)[:, :dec_seq].reshape(dec_batch * dec_seq, d)
        xs_bm = _ffn(xs_bm, lw, gf, l, tile=512, final_norm=last)
        if not last:
            xs_tm = jnp.transpose(xs_bm.reshape(dec_batch, dec_seq, d), (1, 0, 2)).reshape(
                dec_seq * dec_batch, d)

    def new_conv_state(old, fresh_tm, width):
        fresh = jnp.transpose(jnp.stack(fresh_tm).reshape(depth, dec_seq, dec_batch, -1), (0, 2, 1, 3))
        return jnp.concatenate([old, fresh], axis=2)[:, :, -(width - 1):]

    return (xp.reshape(batch, seq, d), xs_bm.reshape(dec_batch, dec_seq, d),
            jnp.stack(p_h), jnp.stack(p_rg), jnp.stack(p_cf),
            mem_k, mem_v,
            jnp.stack(s_h),
            new_conv_state(state_rglru_conv, s_xr, RG_CONV_W),
            new_conv_state(state_conf_conv, s_c, CF_CONV_W))
```

```python
import functools
import math

import jax
import jax.numpy as jnp
from jax import lax
from jax.experimental import pallas as pl
from jax.experimental.pallas import tpu as pltpu

F32 = jnp.float32
BF16 = jnp.bfloat16

EPS = 1e-6
LRU_C = 8.0
RG_CONV_W = 4
CF_CONV_W = 31
SUBLANES = 8
CF_HIST = 32
RG_HIST = SUBLANES
CHUNK = 16
OUT_CHUNK = 32
CONV_LANES = 256
CONV_GROUPS_PER_SEGMENT = 6
FFN_COLS = 512
PROMPT_TILE = 256
VMEM_LIMIT = 56 * 1024 * 1024


def _dot(a, b):
    return jnp.dot(a, b, preferred_element_type=F32)


def _rms(x, g):
    return x * lax.rsqrt(jnp.mean(x * x, axis=-1, keepdims=True) + EPS) * g


def _gelu_tanh(x):
    return 0.5 * x * (1.0 + jnp.tanh(math.sqrt(2.0 / math.pi) * (x + 0.044715 * (x * x * x))))


def _sigmoid(x):
    return 0.5 * (jnp.tanh(0.5 * x) + 1.0)


def _silu(x):
    return x * _sigmoid(x)


def _softplus(z):
    return jnp.maximum(z, 0.0) + jnp.log1p(jnp.exp(-jnp.abs(z)))


def _lru_coeffs(r_pre, i_pre, xc, ba, bx, nsp):
    r = _sigmoid(r_pre + ba)
    i = _sigmoid(i_pre + bx)
    log_a = r * nsp
    a = jnp.exp(log_a)
    t = jnp.tanh(log_a)
    mult = jnp.sqrt(-2.0 * t / (1.0 - t))
    return a, mult * (i * xc)


def _ln_silu(y, g, b):
    mu = jnp.mean(y, axis=-1, keepdims=True)
    yc = y - mu
    var = jnp.mean(yc * yc, axis=-1, keepdims=True)
    return _silu(yc * lax.rsqrt(var + EPS) * g + b)


def _attend(q, k, v, heads, head_dim):
    outs = []
    for h in range(heads):
        sl = slice(h * head_dim, (h + 1) * head_dim)
        s = lax.dot_general(q[:, sl], k[:, sl], (((1,), (1,)), ((), ())), preferred_element_type=F32)
        p = jnp.exp(s - jnp.max(s, axis=-1, keepdims=True))
        inv = 1.0 / jnp.sum(p, axis=-1, keepdims=True)
        outs.append(_dot(p.astype(BF16), v[:, sl]) * inv)
    return outs


def _ffn_cols(d_ff):
    return tuple((lo, min(lo + FFN_COLS, d_ff)) for lo in range(0, d_ff, FFN_COLS))


def _swiglu(x, g, wg_ref, wu_ref, wd_ref, h_s):
    z = _rms(x, g).astype(BF16)
    for lo, hi in _ffn_cols(wg_ref.shape[1]):
        h_s[:, lo:hi] = (_silu(_dot(z, wg_ref[:, lo:hi])) * _dot(z, wu_ref[:, lo:hi])).astype(BF16)
    return x + _dot(h_s[...], wd_ref[...])


def _mxu_cost(rows, k, n):
    return (rows // 2) * pl.cdiv(k, 256) * pl.cdiv(n, 256) // 2


def _full(shape):
    return pl.BlockSpec(shape, lambda *_: (0,) * len(shape))


def _layer_spec(arr, layer):
    zeros = (0,) * (arr.ndim - 1)
    return pl.BlockSpec((None,) + arr.shape[1:], lambda *_: (layer,) + zeros, pipeline_mode=pl.Buffered(1))


def _memkv_kernel(mem_ref, g_ref, wk_ref, wv_ref, k_ref, v_ref, kb_ref, vb_ref):
    m = _rms(mem_ref[...], g_ref[...]).astype(BF16)
    k = _dot(m, wk_ref[...])
    v = _dot(m, wv_ref[...])
    k_ref[...] = k.reshape(k_ref.shape)
    v_ref[...] = v.reshape(v_ref.shape)
    kb_ref[...] = k.astype(BF16)
    vb_ref[...] = v.astype(BF16)


def _memkv(mem2d, g, wk, wv, tile, n_mem, heads):
    depth, d, _ = wk.shape
    rows = mem2d.shape[0]
    row_spec = pl.BlockSpec((None, tile, d), lambda l, i: (l, i, 0))
    out5_spec = pl.BlockSpec((None, tile // n_mem, n_mem, heads, d // heads), lambda l, i: (l, i, 0, 0, 0))
    out5_shape = jax.ShapeDtypeStruct((depth, rows // n_mem, n_mem, heads, d // heads), F32)
    w_spec = pl.BlockSpec((None, d, d), lambda l, i: (l, 0, 0))
    return pl.pallas_call(
        _memkv_kernel,
        grid=(depth, rows // tile),
        in_specs=[
            pl.BlockSpec((tile, d), lambda l, i: (i, 0)),
            pl.BlockSpec((None, 1, d), lambda l, i: (l, 0, 0)),
            w_spec, w_spec,
        ],
        out_specs=[out5_spec, out5_spec, row_spec, row_spec],
        out_shape=[
            out5_shape,
            out5_shape,
            jax.ShapeDtypeStruct((depth, rows, d), BF16),
            jax.ShapeDtypeStruct((depth, rows, d), BF16),
        ],
        compiler_params=pltpu.CompilerParams(
            dimension_semantics=("arbitrary", "arbitrary"), vmem_limit_bytes=VMEM_LIMIT),
        name="memkv",
    )(mem2d, g, wk, wv)


MIX_PARAMS = ("g_mix", "w_in", "w4", "b4", "wa", "ba", "wx", "bx", "lam", "w31", "b31", "ln_g", "ln_b")
BLOCK_PARAMS = ("w_out", "g_attn", "w_q", "w_o", "g_ffn", "w_gate", "w_up", "w_down")


def _group_rows(gi, offset=0):
    return pl.ds(gi * SUBLANES + offset, SUBLANES)


def _mix_stage(x_ref, prm, xr_hist, c_hist, hcar, gr_s, xc_s, p_s, q_s, yc_s, xn_s, cat_s, *, tile, c_w):
    win_ref = prm["w_in"]
    n_groups = tile // SUBLANES
    sub8 = lax.broadcasted_iota(jnp.int32, (SUBLANES, c_w), 0)

    yield 2 * _mxu_cost(tile, win_ref.shape[0], c_w), 2 * tile
    xn_s[...] = _rms(x_ref[...], prm["g_mix"][...]).astype(BF16)
    c_hist[pl.ds(CF_HIST, tile), :] = (
        _dot(xn_s[...], win_ref[:, 2 * c_w:3 * c_w]) * _sigmoid(_dot(xn_s[...], win_ref[:, 3 * c_w:4 * c_w])))
    yield 2 * _mxu_cost(tile, win_ref.shape[0], c_w), tile // 4
    xr_hist[pl.ds(RG_HIST, tile), :] = _dot(xn_s[...], win_ref[:, 0:c_w])
    gr_s[...] = _dot(xn_s[...], win_ref[:, c_w:2 * c_w])

    first_tap_off = CF_HIST - (CF_CONV_W - 1)
    n_hist_groups = CF_HIST // SUBLANES + 1
    w31_ref = prm["w31"]
    for lane0 in range(0, c_w, CONV_LANES):
        lanes = pl.ds(lane0, CONV_LANES)
        sub_l = lax.broadcasted_iota(jnp.int32, (SUBLANES, CONV_LANES), 0)
        zero = jnp.zeros((SUBLANES, CONV_LANES), F32)
        p0_prev, rolled_prev = zero, (zero,) * (SUBLANES - 1)
        for si in range(n_groups + 1):
            if si % CONV_GROUPS_PER_SEGMENT == 0:
                yield 0, 26 * (CONV_LANES // 128) * min(CONV_GROUPS_PER_SEGMENT, n_groups + 1 - si)
            hist = [c_hist[_group_rows(si, SUBLANES * q), lanes] for q in range(n_hist_groups)]
            parts = []
            for res in range(SUBLANES):
                part = None
                for q in range(n_hist_groups):
                    k = SUBLANES * q + res - first_tap_off
                    if 0 <= k < CF_CONV_W:
                        term = hist[q] * w31_ref[k, :, lanes]
                        part = term if part is None else part + term
                parts.append(part)
            rolled = tuple(pltpu.roll(parts[res], SUBLANES - res, axis=0) for res in range(1, SUBLANES))
            y = p0_prev
            for res in range(1, SUBLANES):
                y = y + jnp.where(sub_l < SUBLANES - res, rolled_prev[res - 1], rolled[res - 1])
            yc_s[_group_rows(si), lanes] = y
            p0_prev, rolled_prev = parts[0], rolled

    yield 0, 20 * n_groups
    w4_ref = prm["w4"]
    x_hist = xr_hist[pl.ds(0, RG_HIST), :]
    prev_rolled = [pltpu.roll(x_hist, dly, axis=0) for dly in range(1, RG_CONV_W)]
    for gi in range(n_groups):
        xg = xr_hist[_group_rows(gi, RG_HIST), :]
        acc = xg * w4_ref[RG_CONV_W - 1] + prm["b4"][...]
        rolled = []
        for dly in range(1, RG_CONV_W):
            rolled.append(pltpu.roll(xg, dly, axis=0))
            acc = acc + jnp.where(sub8 >= dly, rolled[-1], prev_rolled[dly - 1]) * w4_ref[RG_CONV_W - 1 - dly]
        xc_s[_group_rows(gi), :] = acc
        prev_rolled = rolled

    yield 2 * _mxu_cost(tile, c_w, c_w), tile // 4
    xcb_s = xn_s.at[:, 0:c_w]
    xcb_s[...] = xc_s[...].astype(BF16)
    p_s[...] = _dot(xcb_s[...], prm["wa"][...])
    q_s[...] = _dot(xcb_s[...], prm["wx"][...])

    nsp = -LRU_C * _softplus(-prm["lam"][...])
    n_grp = CHUNK // SUBLANES
    grp = (n_grp, SUBLANES, c_w)
    sub = lax.broadcasted_iota(jnp.int32, grp, 1)
    for ci in range(tile // CHUNK):
        if ci % 2 == 0:
            yield 0, 240
        rows = pl.ds(ci * CHUNK, CHUNK)
        a, b = _lru_coeffs(p_s[rows, :].reshape(grp), q_s[rows, :].reshape(grp),
                           xc_s[rows, :].reshape(grp), prm["ba"][...], prm["bx"][...], nsp)
        for dly in (1, 2, 4):
            ra = pltpu.roll(a, dly, axis=1)
            rb = pltpu.roll(b, dly, axis=1)
            keep = sub >= dly
            b = a * jnp.where(keep, rb, 0.0) + b
            a = a * jnp.where(keep, ra, 1.0)
        p_s[rows, :] = a.reshape(CHUNK, c_w)
        q_s[rows, :] = b.reshape(CHUNK, c_w)

    yield 0, 21 * n_groups
    hb = hcar[...]
    for gi in range(n_groups):
        rows = _group_rows(gi)
        h = p_s[rows, :] * hb + q_s[rows, :]
        q_s[rows, :] = h
        hb = jnp.broadcast_to(h[SUBLANES - 1:SUBLANES, :], (SUBLANES, c_w))
    hcar[...] = hb

    for ci in range(tile // OUT_CHUNK):
        yield 0, 220
        rows = pl.ds(ci * OUT_CHUNK, OUT_CHUNK)
        y_r = q_s[rows, :] * _gelu_tanh(gr_s[rows, :])
        cat_s[rows, 0:c_w] = y_r.astype(BF16)
        conv = yc_s[pl.ds(ci * OUT_CHUNK + SUBLANES, OUT_CHUNK), :].reshape(
            OUT_CHUNK // SUBLANES, SUBLANES, c_w)
        y_c = _ln_silu(conv + prm["b31"][...], prm["ln_g"][...], prm["ln_b"][...])
        cat_s[rows, c_w:2 * c_w] = y_c.reshape(OUT_CHUNK, c_w).astype(BF16)


def _block_stage(x_ref, cat_s, k_ref, v_ref, prm, gf_ref, xo_ref, xb_s, zb_s, qb_s, o_s, h_s,
                 *, heads, final_norm):
    d = x_ref.shape[1]
    head_dim = d // heads
    rows = x_ref.shape[0]
    yield _mxu_cost(rows, d, d), rows // 2
    xb_s[...] = x_ref[...] + _dot(cat_s[...], prm["w_out"][...])
    yield _mxu_cost(rows, d, d), 2 * rows
    zb_s[...] = _rms(xb_s[...], prm["g_attn"][...]).astype(BF16)
    qb_s[...] = (_dot(zb_s[...], prm["w_q"][...]) * (head_dim ** -0.5)).astype(BF16)
    for h in range(heads):
        yield rows // 2, rows
        sl = slice(h * head_dim, (h + 1) * head_dim)
        (o,) = _attend(qb_s[:, sl], k_ref[:, sl], v_ref[:, sl], 1, head_dim)
        o_s[:, sl] = o.astype(BF16)
    yield _mxu_cost(rows, d, d), rows // 2
    xb_s[...] = xb_s[...] + _dot(o_s[...], prm["w_o"][...])
    yield 0, rows
    zb_s[...] = _rms(xb_s[...], prm["g_ffn"][...]).astype(BF16)
    wg_ref, wu_ref, wd_ref = prm["w_gate"], prm["w_up"], prm["w_down"]
    for lo, hi in _ffn_cols(wg_ref.shape[1]):
        yield 2 * _mxu_cost(rows, d, hi - lo), (hi - lo) // 2
        h_s[:, lo:hi] = (_silu(_dot(zb_s[...], wg_ref[:, lo:hi])) * _dot(zb_s[...], wu_ref[:, lo:hi])
                         ).astype(BF16)
    for lo, hi in _ffn_cols(d):
        yield _mxu_cost(rows, wd_ref.shape[0], hi - lo), rows // 2
        xo_ref[:, lo:hi] = xb_s[:, lo:hi] + _dot(h_s[...], wd_ref[:, lo:hi])
    if final_norm:
        yield 0, rows
        xo_ref[...] = _rms(xo_ref[...], gf_ref[...])


def _interleave(*stages):
    pending = {stage: next(stage) for stage in stages}
    clock = {stage: i for i, stage in enumerate(stages)}
    while pending:
        stage = min(pending, key=clock.get)
        clock[stage] += max(pending.pop(stage))
        try:
            pending[stage] = next(stage)
        except StopIteration:
            pass


def _layer_prompt_kernel(*refs, tile, tiles_per_seq, c_w, heads, final_norm, pad_t):
    n_mix, n_blk = len(MIX_PARAMS), len(BLOCK_PARAMS)
    xa_ref, xb_ref, k_ref, v_ref, sq_ref, sk_ref, sv_ref = refs[:7]
    refs = refs[7:]
    mix_prm = dict(zip(MIX_PARAMS, refs[:n_mix]))
    blk_prm = dict(zip(BLOCK_PARAMS, refs[n_mix:n_mix + n_blk]))
    gf_ref = refs[n_mix + n_blk]
    xo_ref, h_ref, rg_ref, cf_ref, so_ref = refs[1 + n_mix + n_blk:6 + n_mix + n_blk]
    (xr_hist, c_hist, hcar, gr_s, xc_s, p_s, q_s, yc_s, xn_s, cat_s,
     xb_s, zb_s, qb_s, o_s, h_s) = refs[6 + n_mix + n_blk:]

    g = pl.program_id(0)
    pos = g % tiles_per_seq

    @pl.when(g == 0)
    def _():
        cat_s[...] = jnp.zeros(cat_s.shape, BF16)
        c_hist[pl.ds(CF_HIST + tile, SUBLANES), :] = jnp.zeros((SUBLANES, c_w), F32)

    @pl.when(pos == 0)
    def _():
        xr_hist[pl.ds(0, RG_HIST), :] = jnp.zeros((RG_HIST, c_w), F32)
        c_hist[pl.ds(0, CF_HIST), :] = jnp.zeros((CF_HIST, c_w), F32)
        hcar[...] = jnp.zeros((SUBLANES, c_w), F32)

    _interleave(
        _block_stage(xb_ref, cat_s, k_ref, v_ref, blk_prm, gf_ref, xo_ref, xb_s, zb_s, qb_s, o_s, h_s,
                     heads=heads, final_norm=final_norm),
        _mix_stage(xa_ref, mix_prm, xr_hist, c_hist, hcar, gr_s, xc_s, p_s, q_s, yc_s, xn_s, cat_s,
                   tile=tile, c_w=c_w),
        _sample_attn_stage(sq_ref, sk_ref, sv_ref, so_ref, heads=heads, pad_t=pad_t))

    @pl.when(pos == tiles_per_seq - 1)
    def _():
        h_ref[...] = hcar[0:1, :]
        rg_ref[...] = xr_hist[pl.ds(RG_HIST + tile - (RG_CONV_W - 1), RG_CONV_W - 1), :]
        cf_ref[...] = c_hist[pl.ds(CF_HIST + tile - (CF_CONV_W - 1), CF_CONV_W - 1), :]

    xr_hist[pl.ds(0, RG_HIST), :] = xr_hist[pl.ds(tile, RG_HIST), :]
    c_hist[pl.ds(0, CF_HIST), :] = c_hist[pl.ds(tile, CF_HIST), :]


def _layer_prompt(x2d, kb, vb, sq_pad, cache_k, cache_v, lw, gf, layer,
                  *, batch, seq, n_mem, heads, final_norm, pad_t):
    rows, d = x2d.shape
    tile = PROMPT_TILE
    c_w = lw["ba"].shape[-1]
    d_ff = lw["w_gate"].shape[-1]
    tps = seq // tile
    n_tiles = batch * tps
    dec_batch = cache_k.shape[1]
    nb = dec_batch // n_tiles
    assert nb * n_tiles == dec_batch and sq_pad.shape[0] == dec_batch * pad_t

    def mix_tile(g):
        return jnp.minimum(g, n_tiles - 1)

    def blk_tile(g):
        return jnp.maximum(g - 1, 0)

    kv_spec = pl.BlockSpec((None, n_mem, d), lambda g: (layer, blk_tile(g) // tps, 0))
    sq_spec = pl.BlockSpec((nb * pad_t, d), lambda g: (mix_tile(g), 0))
    cache_spec = pl.BlockSpec((None, nb, n_mem, heads, d // heads), lambda g: (layer, mix_tile(g), 0, 0, 0))
    params = [lw[name] for name in MIX_PARAMS + BLOCK_PARAMS]
    kern = functools.partial(_layer_prompt_kernel, tile=tile, tiles_per_seq=tps, c_w=c_w,
                             heads=heads, final_norm=final_norm, pad_t=pad_t)
    return pl.pallas_call(
        kern,
        grid=(n_tiles + 1,),
        in_specs=[
            pl.BlockSpec((tile, d), lambda g: (mix_tile(g), 0)),
            pl.BlockSpec((tile, d), lambda g: (blk_tile(g), 0)),
            kv_spec, kv_spec,
            sq_spec, cache_spec, cache_spec,
            *[_layer_spec(p, layer) for p in params],
            _full(gf.shape),
        ],
        out_specs=[
            pl.BlockSpec((tile, d), lambda g: (blk_tile(g), 0)),
            pl.BlockSpec((None, 1, c_w), lambda g: (mix_tile(g) // tps, 0, 0)),
            pl.BlockSpec((None, RG_CONV_W - 1, c_w), lambda g: (mix_tile(g) // tps, 0, 0)),
            pl.BlockSpec((None, CF_CONV_W - 1, c_w), lambda g: (mix_tile(g) // tps, 0, 0)),
            sq_spec,
        ],
        out_shape=[
            jax.ShapeDtypeStruct((rows, d), F32),
            jax.ShapeDtypeStruct((batch, 1, c_w), F32),
            jax.ShapeDtypeStruct((batch, RG_CONV_W - 1, c_w), F32),
            jax.ShapeDtypeStruct((batch, CF_CONV_W - 1, c_w), F32),
            jax.ShapeDtypeStruct(sq_pad.shape, F32),
        ],
        scratch_shapes=[
            pltpu.VMEM((RG_HIST + tile, c_w), F32),
            pltpu.VMEM((CF_HIST + tile + SUBLANES, c_w), F32),
            pltpu.VMEM((SUBLANES, c_w), F32),
            pltpu.VMEM((tile, c_w), F32),
            pltpu.VMEM((tile, c_w), F32),
            pltpu.VMEM((tile, c_w), F32),
            pltpu.VMEM((tile, c_w), F32),
            pltpu.VMEM((tile + SUBLANES, c_w), F32),
            pltpu.VMEM((tile, d), BF16),
            pltpu.VMEM((tile, 2 * c_w), BF16),
            pltpu.VMEM((tile, d), F32),
            pltpu.VMEM((tile, d), BF16),
            pltpu.VMEM((tile, d), BF16),
            pltpu.VMEM((tile, d), BF16),
            pltpu.VMEM((tile, d_ff), BF16),
        ],
        compiler_params=pltpu.CompilerParams(
            dimension_semantics=("arbitrary",), vmem_limit_bytes=VMEM_LIMIT),
        name="layer_prompt",
    )(x2d, x2d, kb, vb, sq_pad, cache_k, cache_v, *params, gf)


def _mix_sample_kernel(x_ref, h0_ref, rb_ref, cb_ref, g_ref, win_ref, w4_ref, b4_ref, wa_ref, ba_ref,
                       wx_ref, bx_ref, lam_ref, w31_ref, b31_ref, lng_ref, lnb_ref, wout_ref,
                       ga_ref, wq_ref,
                       xo_ref, h_ref, xr_ref, c_ref, qo_ref,
                       gr_s, xc_s, p_s, q_s, cat_s, *, steps, batch, d_rnn, heads):
    c_w = d_rnn
    xn = _rms(x_ref[...], g_ref[...]).astype(BF16)
    xr_ref[...] = _dot(xn, win_ref[:, 0:c_w])
    gr_s[...] = _dot(xn, win_ref[:, c_w:2 * c_w])
    c_ref[...] = (
        _dot(xn, win_ref[:, 2 * c_w:3 * c_w]) * _sigmoid(_dot(xn, win_ref[:, 3 * c_w:4 * c_w])))

    def rows_of(t, s):
        return pl.ds(pl.multiple_of(t * batch + s, SUBLANES), SUBLANES)

    def conv4_blk(bi, carry):
        s = bi * SUBLANES
        for t in range(steps):
            acc = b4_ref[...]
            for k in range(RG_CONV_W):
                jj = t + k
                if jj < RG_CONV_W - 1:
                    src = rb_ref[jj, pl.ds(pl.multiple_of(s, SUBLANES), SUBLANES), :]
                else:
                    src = xr_ref[rows_of(jj - (RG_CONV_W - 1), s), :]
                acc = acc + src * w4_ref[k]
            xc_s[rows_of(t, s), :] = acc
        return carry

    lax.fori_loop(0, batch // SUBLANES, conv4_blk, 0)

    xcb = xc_s[...].astype(BF16)
    p_s[...] = _dot(xcb, wa_ref[...])
    q_s[...] = _dot(xcb, wx_ref[...])
    nsp = -LRU_C * _softplus(-lam_ref[...])

    def scan_blk(bi, carry):
        s = bi * SUBLANES
        b_rows = pl.ds(pl.multiple_of(s, SUBLANES), SUBLANES)
        h = h0_ref[b_rows, :]
        for t in range(steps):
            rows = rows_of(t, s)
            a, b = _lru_coeffs(p_s[rows, :], q_s[rows, :], xc_s[rows, :], ba_ref[...], bx_ref[...], nsp)
            h = a * h + b
            cat_s[rows, 0:c_w] = h * _gelu_tanh(gr_s[rows, :])
        h_ref[b_rows, :] = h
        return carry

    lax.fori_loop(0, batch // SUBLANES, scan_blk, 0)

    def conv31_blk(bi, carry):
        s = bi * SUBLANES
        for t in range(steps):
            acc = b31_ref[...]
            for k in range(CF_CONV_W):
                jj = t + k
                if jj < CF_CONV_W - 1:
                    src = cb_ref[jj, pl.ds(pl.multiple_of(s, SUBLANES), SUBLANES), :]
                else:
                    src = c_ref[rows_of(jj - (CF_CONV_W - 1), s), :]
                acc = acc + src * w31_ref[k]
            cat_s[rows_of(t, s), c_w:2 * c_w] = _ln_silu(acc, lng_ref[...], lnb_ref[...])
        return carry

    lax.fori_loop(0, batch // SUBLANES, conv31_blk, 0)

    xo = x_ref[...] + _dot(cat_s[...].astype(BF16), wout_ref[...])
    xo_ref[...] = xo
    head_dim = xo.shape[1] // heads
    qo_ref[...] = _dot(_rms(xo, ga_ref[...]).astype(BF16), wq_ref[...]) * (head_dim ** -0.5)


def _mix_sample(x_tm, h0, rb_tm, cb_tm, lw, layer, *, steps, batch, heads):
    rows, d = x_tm.shape
    c_w = lw["ba"].shape[-1]
    kern = functools.partial(_mix_sample_kernel, steps=steps, batch=batch, d_rnn=c_w, heads=heads)
    params = [lw[name] for name in MIX_PARAMS + ("w_out", "g_attn", "w_q")]
    return pl.pallas_call(
        kern,
        grid=(1,),
        in_specs=[_full(x_tm.shape), _layer_spec(h0, layer), _layer_spec(rb_tm, layer),
                  _layer_spec(cb_tm, layer),
                  *[_layer_spec(p, layer) for p in params]],
        out_specs=[_full((rows, d)), _full((batch, c_w)), _full((rows, c_w)), _full((rows, c_w)),
                   _full((rows, d))],
        out_shape=[
            jax.ShapeDtypeStruct((rows, d), F32),
            jax.ShapeDtypeStruct((batch, c_w), F32),
            jax.ShapeDtypeStruct((rows, c_w), F32),
            jax.ShapeDtypeStruct((rows, c_w), F32),
            jax.ShapeDtypeStruct((rows, d), F32),
        ],
        scratch_shapes=[
            pltpu.VMEM((rows, c_w), F32),
            pltpu.VMEM((rows, c_w), F32),
            pltpu.VMEM((rows, c_w), F32),
            pltpu.VMEM((rows, c_w), F32),
            pltpu.VMEM((rows, 2 * c_w), F32),
        ],
        compiler_params=pltpu.CompilerParams(
            dimension_semantics=("arbitrary",), vmem_limit_bytes=VMEM_LIMIT),
        name="mix_sample",
    )(x_tm, h0, rb_tm, cb_tm, *params)


def _sample_attn_stage(q_ref, k_ref, v_ref, o_ref, *, heads, pad_t):
    nb, n_mem, _, head_dim = k_ref.shape
    n_kv = n_mem * heads
    col_head = lax.broadcasted_iota(jnp.int32, (heads * pad_t, n_kv), 1) % heads
    row_head = lax.broadcasted_iota(jnp.int32, (heads * pad_t, n_kv), 0) // pad_t
    own_head = col_head == row_head
    for bb in range(nb):
        yield 2 * _mxu_cost(heads * pad_t, head_dim, n_kv) + n_kv // 2, n_kv // 2
        rows = pl.ds(bb * pad_t, pad_t)
        q = q_ref[rows, :]
        qs = jnp.concatenate([q[:, h * head_dim:(h + 1) * head_dim] for h in range(heads)], axis=0)
        k2 = k_ref[bb].reshape(n_kv, head_dim).astype(BF16)
        v2 = v_ref[bb].reshape(n_kv, head_dim).astype(BF16)
        s = lax.dot_general(qs.astype(BF16), k2, (((1,), (1,)), ((), ())), preferred_element_type=F32)
        s = jnp.where(own_head, s, -1e30)
        p = jnp.exp(s - jnp.max(s, axis=-1, keepdims=True))
        inv = 1.0 / jnp.sum(p, axis=-1, keepdims=True)
        o = _dot(p.astype(BF16), v2) * inv
        for h in range(heads):
            o_ref[rows, h * head_dim:(h + 1) * head_dim] = o[h * pad_t:(h + 1) * pad_t, :]


def _ffn_kernel(x_ref, o_ref, wo_ref, g_ref, wg_ref, wu_ref, wd_ref, gf_ref, xo_ref, h_s, *, final_norm):
    x = x_ref[...] + _dot(o_ref[...].astype(BF16), wo_ref[...])
    y = _swiglu(x, g_ref[...], wg_ref, wu_ref, wd_ref, h_s)
    if final_norm:
        y = _rms(y, gf_ref[...])
    xo_ref[...] = y


def _ffn(x2d, o2d, lw, gf, layer, *, tile, final_norm):
    rows, d = x2d.shape
    d_ff = lw["w_gate"].shape[-1]
    row_spec = pl.BlockSpec((tile, d), lambda i: (i, 0))
    kern = functools.partial(_ffn_kernel, final_norm=final_norm)
    return pl.pallas_call(
        kern,
        grid=(rows // tile,),
        in_specs=[row_spec, row_spec, _layer_spec(lw["w_o"], layer), _layer_spec(lw["g_ffn"], layer),
                  _layer_spec(lw["w_gate"], layer), _layer_spec(lw["w_up"], layer),
                  _layer_spec(lw["w_down"], layer), _full(gf.shape)],
        out_specs=row_spec,
        out_shape=jax.ShapeDtypeStruct((rows, d), F32),
        scratch_shapes=[pltpu.VMEM((tile, d_ff), BF16)],
        compiler_params=pltpu.CompilerParams(
            dimension_semantics=("arbitrary",), vmem_limit_bytes=VMEM_LIMIT),
        name="ffn",
    )(x2d, o2d, lw["w_o"], lw["g_ffn"], lw["w_gate"], lw["w_up"], lw["w_down"], gf)


def _rows8(v):
    return jnp.broadcast_to(v[..., None, :], v.shape[:-1] + (SUBLANES, v.shape[-1]))


def _block_diag(w):
    l, h, i, j = w.shape
    eye = jnp.eye(h, dtype=w.dtype)
    return jnp.einsum("lhij,hg->lhigj", w, eye).reshape(l, h * i, h * j)


def kernel(x_prompt, x_sample, state_rglru_h, state_rglru_conv, state_conf_conv, cache_mem_k, cache_mem_v, mem_prompt, norm_mix_g, w_in, rg_conv_w, rg_conv_b, rg_wa, rg_ba, rg_wx, rg_bx, rg_lambda, cf_conv_w, cf_conv_b, cf_ln_g, cf_ln_b, w_out, norm_attn_g, norm_mem_g, w_q, w_k, w_v, w_o, norm_ffn_g, w_gate, w_up, w_down, norm_final_g):
    batch, seq, d = x_prompt.shape
    dec_batch, dec_seq, _ = x_sample.shape
    depth = w_in.shape[0]
    n_mem = mem_prompt.shape[1]
    heads = cache_mem_k.shape[3]
    pad_t = SUBLANES

    lw = {
        "g_mix": norm_mix_g[:, None, :],
        "w_in": w_in.astype(BF16),
        "w4": _rows8(rg_conv_w), "b4": _rows8(rg_conv_b),
        "wa": _block_diag(rg_wa).astype(BF16), "ba": _rows8(rg_ba),
        "wx": _block_diag(rg_wx).astype(BF16), "bx": _rows8(rg_bx),
        "lam": _rows8(rg_lambda),
        "w31": _rows8(cf_conv_w), "b31": _rows8(cf_conv_b),
        "ln_g": _rows8(cf_ln_g), "ln_b": _rows8(cf_ln_b),
        "w_out": w_out.astype(BF16),
        "g_attn": norm_attn_g[:, None, :],
        "w_q": w_q.astype(BF16), "w_o": w_o.astype(BF16),
        "g_ffn": norm_ffn_g[:, None, :],
        "w_gate": w_gate.astype(BF16), "w_up": w_up.astype(BF16), "w_down": w_down.astype(BF16),
    }
    gf = norm_final_g[None, :]

    mem_k, mem_v, mem_kb, mem_vb = _memkv(
        mem_prompt.reshape(batch * n_mem, d), norm_mem_g[:, None, :],
        w_k.astype(BF16), w_v.astype(BF16), tile=512, n_mem=n_mem, heads=heads)

    rb_tm = jnp.transpose(state_rglru_conv, (0, 2, 1, 3))
    cb_tm = jnp.transpose(state_conf_conv, (0, 2, 1, 3))

    xp = x_prompt.reshape(batch * seq, d)
    xs_tm = jnp.transpose(x_sample, (1, 0, 2)).reshape(dec_seq * dec_batch, d)
    xs_bm = None
    p_h, p_rg, p_cf, s_h, s_xr, s_c = [], [], [], [], [], []
    for l in range(depth):
        last = l == depth - 1

        xs_tm, hs, xr_tm, c_tm, sq_tm = _mix_sample(xs_tm, state_rglru_h, rb_tm, cb_tm, lw, l,
                                                    steps=dec_seq, batch=dec_batch, heads=heads)
        s_h.append(hs); s_xr.append(xr_tm); s_c.append(c_tm)
        sq_pad = jnp.pad(jnp.transpose(sq_tm.reshape(dec_seq, dec_batch, d), (1, 0, 2)),
                         ((0, 0), (0, pad_t - dec_seq), (0, 0))).reshape(dec_batch * pad_t, d)

        xp, hp, rgp, cfp, so_pad = _layer_prompt(
            xp, mem_kb, mem_vb, sq_pad, cache_mem_k, cache_mem_v, lw, gf, l, batch=batch, seq=seq,
            n_mem=n_mem, heads=heads, final_norm=last, pad_t=pad_t)
        p_h.append(hp[:, 0, :]); p_rg.append(rgp); p_cf.append(cfp)

        so_bm = so_pad.reshape(dec_batch, pad_t, d)[:, :dec_seq].reshape(dec_batch * dec_seq, d)
        xs_bm = jnp.transpose(xs_tm.reshape(dec_seq, dec_batch, d), (1, 0, 2)).reshape(
            dec_batch * dec_seq, d)
        xs_bm = _ffn(xs_bm, so_bm, lw, gf, l, tile=512, final_norm=last)
        if not last:
            xs_tm = jnp.transpose(xs_bm.reshape(dec_batch, dec_seq, d), (1, 0, 2)).reshape(
                dec_seq * dec_batch, d)

    def new_conv_state(old, fresh_tm, width):
        fresh = jnp.transpose(jnp.stack(fresh_tm).reshape(depth, dec_seq, dec_batch, -1), (0, 2, 1, 3))
        return jnp.concatenate([old, fresh], axis=2)[:, :, -(width - 1):]

    return (xp.reshape(batch, seq, d), xs_bm.reshape(dec_batch, dec_seq, d),
            jnp.stack(p_h), jnp.stack(p_rg), jnp.stack(p_cf),
            mem_k, mem_v,
            jnp.stack(s_h),
            new_conv_state(state_rglru_conv, s_xr, RG_CONV_W),
            new_conv_state(state_conf_conv, s_c, CF_CONV_W))
```

```python
import functools
import math

import jax
import jax.numpy as jnp
from jax import lax
from jax.experimental import pallas as pl
from jax.experimental.pallas import tpu as pltpu

F32 = jnp.float32
BF16 = jnp.bfloat16

EPS = 1e-6
LRU_C = 8.0
RG_CONV_W = 4
CF_CONV_W = 31
SUBLANES = 8
MXU_DIM = 256
CF_HIST = 32
RG_HIST = SUBLANES
CHUNK = 16
OUT_CHUNK = 32
CONV_LANES = 256
CONV_GROUPS_PER_SEGMENT = 6
FFN_COLS = 512
PROMPT_TILE = 256
VMEM_LIMIT = 56 * 1024 * 1024


def _dot(a, b):
    return jnp.dot(a, b, preferred_element_type=F32)


def _rms(x, g):
    return x * lax.rsqrt(jnp.mean(x * x, axis=-1, keepdims=True) + EPS) * g


def _gelu_tanh(x):
    return 0.5 * x * (1.0 + jnp.tanh(math.sqrt(2.0 / math.pi) * (x + 0.044715 * (x * x * x))))


def _sigmoid(x):
    return 0.5 * (jnp.tanh(0.5 * x) + 1.0)


def _silu(x):
    return x * _sigmoid(x)


def _softplus(z):
    return jnp.maximum(z, 0.0) + jnp.log1p(jnp.exp(-jnp.abs(z)))


def _lru_coeffs(r_pre, i_pre, xc, ba, bx, nsp):
    r = _sigmoid(r_pre + ba)
    i = _sigmoid(i_pre + bx)
    log_a = r * nsp
    a = jnp.exp(log_a)
    t = jnp.tanh(log_a)
    mult = jnp.sqrt(-2.0 * t / (1.0 - t))
    return a, mult * (i * xc)


def _ln_silu(y, g, b):
    mu = jnp.mean(y, axis=-1, keepdims=True)
    yc = y - mu
    var = jnp.mean(yc * yc, axis=-1, keepdims=True)
    return _silu(yc * lax.rsqrt(var + EPS) * g + b)


def _attend(q, k, v, heads, head_dim):
    outs = []
    for h in range(heads):
        sl = slice(h * head_dim, (h + 1) * head_dim)
        s = lax.dot_general(q[:, sl], k[:, sl], (((1,), (1,)), ((), ())), preferred_element_type=F32)
        p = jnp.exp(s - jnp.max(s, axis=-1, keepdims=True))
        inv = 1.0 / jnp.sum(p, axis=-1, keepdims=True)
        outs.append(_dot(p.astype(BF16), v[:, sl]) * inv)
    return outs


def _ffn_cols(d_ff):
    return tuple((lo, min(lo + FFN_COLS, d_ff)) for lo in range(0, d_ff, FFN_COLS))


def _swiglu(x, g, wg_ref, wu_ref, wd_ref, h_s):
    z = _rms(x, g).astype(BF16)
    for lo, hi in _ffn_cols(wg_ref.shape[1]):
        h_s[:, lo:hi] = (_silu(_dot(z, wg_ref[:, lo:hi])) * _dot(z, wu_ref[:, lo:hi])).astype(BF16)
    return x + _dot(h_s[...], wd_ref[...])


def _mxu_cost(rows, k, n):
    return (rows // 2) * pl.cdiv(k, MXU_DIM) * pl.cdiv(n, MXU_DIM) // 2


def _full(shape):
    return pl.BlockSpec(shape, lambda *_: (0,) * len(shape))


def _layer_spec(arr, layer):
    zeros = (0,) * (arr.ndim - 1)
    return pl.BlockSpec((None,) + arr.shape[1:], lambda *_: (layer,) + zeros, pipeline_mode=pl.Buffered(1))


def _memkv_kernel(mem_ref, g_ref, wk_ref, wv_ref, k_ref, v_ref, kb_ref, vb_ref):
    m = _rms(mem_ref[...], g_ref[...]).astype(BF16)
    k = _dot(m, wk_ref[...])
    v = _dot(m, wv_ref[...])
    k_ref[...] = k.reshape(k_ref.shape)
    v_ref[...] = v.reshape(v_ref.shape)
    kb_ref[...] = k.astype(BF16)
    vb_ref[...] = v.astype(BF16)


def _memkv(mem2d, g, wk, wv, tile, n_mem, heads):
    depth, d, _ = wk.shape
    rows = mem2d.shape[0]
    row_spec = pl.BlockSpec((None, tile, d), lambda l, i: (l, i, 0))
    out5_spec = pl.BlockSpec((None, tile // n_mem, n_mem, heads, d // heads), lambda l, i: (l, i, 0, 0, 0))
    out5_shape = jax.ShapeDtypeStruct((depth, rows // n_mem, n_mem, heads, d // heads), F32)
    w_spec = pl.BlockSpec((None, d, d), lambda l, i: (l, 0, 0))
    return pl.pallas_call(
        _memkv_kernel,
        grid=(depth, rows // tile),
        in_specs=[
            pl.BlockSpec((tile, d), lambda l, i: (i, 0)),
            pl.BlockSpec((None, 1, d), lambda l, i: (l, 0, 0)),
            w_spec, w_spec,
        ],
        out_specs=[out5_spec, out5_spec, row_spec, row_spec],
        out_shape=[
            out5_shape,
            out5_shape,
            jax.ShapeDtypeStruct((depth, rows, d), BF16),
            jax.ShapeDtypeStruct((depth, rows, d), BF16),
        ],
        compiler_params=pltpu.CompilerParams(
            dimension_semantics=("arbitrary", "arbitrary"), vmem_limit_bytes=VMEM_LIMIT),
        name="memkv",
    )(mem2d, g, wk, wv)


MIX_PARAMS = ("g_mix", "w_in", "wa", "wx", "vec")
VEC_ROWS = (("w4", RG_CONV_W), ("b4", None), ("ba", None), ("bx", None), ("lam", None),
            ("w31", CF_CONV_W), ("b31", None), ("ln_g", None), ("ln_b", None))


def _unpack_vec(vec_ref):
    views, row = {}, 0
    for name, count in VEC_ROWS:
        views[name] = vec_ref.at[row] if count is None else vec_ref.at[row:row + count]
        row += count or 1
    return views


def _gate_products(lhs, wa_ref, wx_ref, p_s, q_s):
    n_tiles, width = wa_ref.shape[0], wa_ref.shape[1]
    for j in range(n_tiles):
        cols = slice(j * width, (j + 1) * width)
        p_s[:, cols] = _dot(lhs[:, cols], wa_ref[j])
        q_s[:, cols] = _dot(lhs[:, cols], wx_ref[j])
BLOCK_PARAMS = ("w_out", "g_attn", "w_q", "w_o", "g_ffn", "w_gate", "w_up", "w_down")


def _group_rows(gi, offset=0):
    return pl.ds(gi * SUBLANES + offset, SUBLANES)


def _mix_stage(x_ref, prm, xr_hist, c_hist, hcar, gr_s, xc_s, p_s, q_s, yc_s, xn_s, cat_s, *, tile, c_w):
    win_ref = prm["w_in"]
    n_groups = tile // SUBLANES
    sub8 = lax.broadcasted_iota(jnp.int32, (SUBLANES, c_w), 0)

    yield 2 * _mxu_cost(tile, win_ref.shape[0], c_w), 2 * tile
    xn_s[...] = _rms(x_ref[...], prm["g_mix"][...]).astype(BF16)
    c_hist[pl.ds(CF_HIST, tile), :] = (
        _dot(xn_s[...], win_ref[:, 2 * c_w:3 * c_w]) * _sigmoid(_dot(xn_s[...], win_ref[:, 3 * c_w:4 * c_w])))
    yield 2 * _mxu_cost(tile, win_ref.shape[0], c_w), tile // 4
    xr_hist[pl.ds(RG_HIST, tile), :] = _dot(xn_s[...], win_ref[:, 0:c_w])
    gr_s[...] = _dot(xn_s[...], win_ref[:, c_w:2 * c_w])

    first_tap_off = CF_HIST - (CF_CONV_W - 1)
    n_hist_groups = CF_HIST // SUBLANES + 1
    w31_ref = prm["w31"]
    for lane0 in range(0, c_w, CONV_LANES):
        lanes = pl.ds(lane0, CONV_LANES)
        sub_l = lax.broadcasted_iota(jnp.int32, (SUBLANES, CONV_LANES), 0)
        zero = jnp.zeros((SUBLANES, CONV_LANES), F32)
        p0_prev, rolled_prev = zero, (zero,) * (SUBLANES - 1)
        for si in range(n_groups + 1):
            if si % CONV_GROUPS_PER_SEGMENT == 0:
                yield 0, 26 * (CONV_LANES // 128) * min(CONV_GROUPS_PER_SEGMENT, n_groups + 1 - si)
            hist = [c_hist[_group_rows(si, SUBLANES * q), lanes] for q in range(n_hist_groups)]
            parts = []
            for res in range(SUBLANES):
                part = None
                for q in range(n_hist_groups):
                    k = SUBLANES * q + res - first_tap_off
                    if 0 <= k < CF_CONV_W:
                        term = hist[q] * w31_ref[k, :, lanes]
                        part = term if part is None else part + term
                parts.append(part)
            rolled = tuple(pltpu.roll(parts[res], SUBLANES - res, axis=0) for res in range(1, SUBLANES))
            y = p0_prev
            for res in range(1, SUBLANES):
                y = y + jnp.where(sub_l < SUBLANES - res, rolled_prev[res - 1], rolled[res - 1])
            yc_s[_group_rows(si), lanes] = y
            p0_prev, rolled_prev = parts[0], rolled

    yield 0, 20 * n_groups
    w4_ref = prm["w4"]
    x_hist = xr_hist[pl.ds(0, RG_HIST), :]
    prev_rolled = [pltpu.roll(x_hist, dly, axis=0) for dly in range(1, RG_CONV_W)]
    for gi in range(n_groups):
        xg = xr_hist[_group_rows(gi, RG_HIST), :]
        acc = xg * w4_ref[RG_CONV_W - 1] + prm["b4"][...]
        rolled = []
        for dly in range(1, RG_CONV_W):
            rolled.append(pltpu.roll(xg, dly, axis=0))
            acc = acc + jnp.where(sub8 >= dly, rolled[-1], prev_rolled[dly - 1]) * w4_ref[RG_CONV_W - 1 - dly]
        xc_s[_group_rows(gi), :] = acc
        prev_rolled = rolled

    n_gate_tiles, gate_width = prm["wa"].shape[0], prm["wa"].shape[1]
    yield 2 * n_gate_tiles * _mxu_cost(tile, gate_width, gate_width), tile // 4
    xcb_s = xn_s.at[:, 0:c_w]
    xcb_s[...] = xc_s[...].astype(BF16)
    _gate_products(xcb_s, prm["wa"], prm["wx"], p_s, q_s)

    nsp = -LRU_C * _softplus(-prm["lam"][...])
    n_grp = CHUNK // SUBLANES
    grp = (n_grp, SUBLANES, c_w)
    sub = lax.broadcasted_iota(jnp.int32, grp, 1)
    for ci in range(tile // CHUNK):
        if ci % 2 == 0:
            yield 0, 240
        rows = pl.ds(ci * CHUNK, CHUNK)
        a, b = _lru_coeffs(p_s[rows, :].reshape(grp), q_s[rows, :].reshape(grp),
                           xc_s[rows, :].reshape(grp), prm["ba"][...], prm["bx"][...], nsp)
        for dly in (1, 2, 4):
            ra = pltpu.roll(a, dly, axis=1)
            rb = pltpu.roll(b, dly, axis=1)
            keep = sub >= dly
            b = a * jnp.where(keep, rb, 0.0) + b
            a = a * jnp.where(keep, ra, 1.0)
        p_s[rows, :] = a.reshape(CHUNK, c_w)
        q_s[rows, :] = b.reshape(CHUNK, c_w)

    yield 0, 21 * n_groups
    hb = hcar[...]
    for gi in range(n_groups):
        rows = _group_rows(gi)
        h = p_s[rows, :] * hb + q_s[rows, :]
        q_s[rows, :] = h
        hb = jnp.broadcast_to(h[SUBLANES - 1:SUBLANES, :], (SUBLANES, c_w))
    hcar[...] = hb

    for ci in range(tile // OUT_CHUNK):
        yield 0, 220
        rows = pl.ds(ci * OUT_CHUNK, OUT_CHUNK)
        y_r = q_s[rows, :] * _gelu_tanh(gr_s[rows, :])
        cat_s[rows, 0:c_w] = y_r.astype(BF16)
        conv = yc_s[pl.ds(ci * OUT_CHUNK + SUBLANES, OUT_CHUNK), :].reshape(
            OUT_CHUNK // SUBLANES, SUBLANES, c_w)
        y_c = _ln_silu(conv + prm["b31"][...], prm["ln_g"][...], prm["ln_b"][...])
        cat_s[rows, c_w:2 * c_w] = y_c.reshape(OUT_CHUNK, c_w).astype(BF16)


def _block_stage(x_ref, cat_s, k_ref, v_ref, prm, gf_ref, xo_ref, xb_s, zb_s, qb_s, o_s, h_s,
                 *, heads, final_norm):
    d = x_ref.shape[1]
    head_dim = d // heads
    rows = x_ref.shape[0]
    yield _mxu_cost(rows, d, d), rows // 2
    xb_s[...] = x_ref[...] + _dot(cat_s[...], prm["w_out"][...])
    yield _mxu_cost(rows, d, d), 2 * rows
    zb_s[...] = _rms(xb_s[...], prm["g_attn"][...]).astype(BF16)
    qb_s[...] = (_dot(zb_s[...], prm["w_q"][...]) * (head_dim ** -0.5)).astype(BF16)
    for h in range(heads):
        yield rows // 2, rows
        sl = slice(h * head_dim, (h + 1) * head_dim)
        (o,) = _attend(qb_s[:, sl], k_ref[:, sl], v_ref[:, sl], 1, head_dim)
        o_s[:, sl] = o.astype(BF16)
    yield _mxu_cost(rows, d, d), rows // 2
    xb_s[...] = xb_s[...] + _dot(o_s[...], prm["w_o"][...])
    yield 0, rows
    zb_s[...] = _rms(xb_s[...], prm["g_ffn"][...]).astype(BF16)
    wg_ref, wu_ref, wd_ref = prm["w_gate"], prm["w_up"], prm["w_down"]
    for lo, hi in _ffn_cols(wg_ref.shape[1]):
        yield 2 * _mxu_cost(rows, d, hi - lo), (hi - lo) // 2
        h_s[:, lo:hi] = (_silu(_dot(zb_s[...], wg_ref[:, lo:hi])) * _dot(zb_s[...], wu_ref[:, lo:hi])
                         ).astype(BF16)
    for lo, hi in _ffn_cols(d):
        yield _mxu_cost(rows, wd_ref.shape[0], hi - lo), rows // 2
        xo_ref[:, lo:hi] = xb_s[:, lo:hi] + _dot(h_s[...], wd_ref[:, lo:hi])
    if final_norm:
        yield 0, rows
        xo_ref[...] = _rms(xo_ref[...], gf_ref[...])


def _interleave(*stages):
    pending = {stage: next(stage) for stage in stages}
    clock = {stage: i for i, stage in enumerate(stages)}
    while pending:
        stage = min(pending, key=clock.get)
        clock[stage] += max(pending.pop(stage))
        try:
            pending[stage] = next(stage)
        except StopIteration:
            pass


def _layer_prompt_kernel(*refs, tile, tiles_per_seq, c_w, heads, final_norm, pad_t):
    n_mix, n_blk = len(MIX_PARAMS), len(BLOCK_PARAMS)
    xa_ref, xb_ref, k_ref, v_ref, sq_ref, sk_ref, sv_ref = refs[:7]
    refs = refs[7:]
    mix_prm = dict(zip(MIX_PARAMS, refs[:n_mix]))
    mix_prm.update(_unpack_vec(mix_prm["vec"]))
    blk_prm = dict(zip(BLOCK_PARAMS, refs[n_mix:n_mix + n_blk]))
    gf_ref = refs[n_mix + n_blk]
    xo_ref, h_ref, rg_ref, cf_ref, so_ref = refs[1 + n_mix + n_blk:6 + n_mix + n_blk]
    (xr_hist, c_hist, hcar, gr_s, xc_s, p_s, q_s, yc_s, xn_s, cat_s,
     xb_s, zb_s, qb_s, o_s, h_s) = refs[6 + n_mix + n_blk:]

    g = pl.program_id(0)
    pos = g % tiles_per_seq

    @pl.when(g == 0)
    def _():
        cat_s[...] = jnp.zeros(cat_s.shape, BF16)
        c_hist[pl.ds(CF_HIST + tile, SUBLANES), :] = jnp.zeros((SUBLANES, c_w), F32)

    @pl.when(pos == 0)
    def _():
        xr_hist[pl.ds(0, RG_HIST), :] = jnp.zeros((RG_HIST, c_w), F32)
        c_hist[pl.ds(0, CF_HIST), :] = jnp.zeros((CF_HIST, c_w), F32)
        hcar[...] = jnp.zeros((SUBLANES, c_w), F32)

    _interleave(
        _block_stage(xb_ref, cat_s, k_ref, v_ref, blk_prm, gf_ref, xo_ref, xb_s, zb_s, qb_s, o_s, h_s,
                     heads=heads, final_norm=final_norm),
        _mix_stage(xa_ref, mix_prm, xr_hist, c_hist, hcar, gr_s, xc_s, p_s, q_s, yc_s, xn_s, cat_s,
                   tile=tile, c_w=c_w),
        _sample_attn_stage(sq_ref, sk_ref, sv_ref, so_ref, heads=heads, pad_t=pad_t))

    @pl.when(pos == tiles_per_seq - 1)
    def _():
        h_ref[...] = hcar[0:1, :]
        rg_ref[...] = xr_hist[pl.ds(RG_HIST + tile - (RG_CONV_W - 1), RG_CONV_W - 1), :]
        cf_ref[...] = c_hist[pl.ds(CF_HIST + tile - (CF_CONV_W - 1), CF_CONV_W - 1), :]

    xr_hist[pl.ds(0, RG_HIST), :] = xr_hist[pl.ds(tile, RG_HIST), :]
    c_hist[pl.ds(0, CF_HIST), :] = c_hist[pl.ds(tile, CF_HIST), :]


def _layer_prompt(x2d, kb, vb, sq_pad, cache_k, cache_v, lw, gf, layer,
                  *, batch, seq, n_mem, heads, final_norm, pad_t):
    rows, d = x2d.shape
    tile = PROMPT_TILE
    c_w = lw["vec"].shape[-1]
    d_ff = lw["w_gate"].shape[-1]
    tps = seq // tile
    n_tiles = batch * tps
    dec_batch = cache_k.shape[1]
    nb = dec_batch // n_tiles
    assert nb * n_tiles == dec_batch and sq_pad.shape[0] == dec_batch * pad_t

    def mix_tile(g):
        return jnp.minimum(g, n_tiles - 1)

    def blk_tile(g):
        return jnp.maximum(g - 1, 0)

    kv_spec = pl.BlockSpec((None, n_mem, d), lambda g: (layer, blk_tile(g) // tps, 0))
    sq_spec = pl.BlockSpec((nb * pad_t, d), lambda g: (mix_tile(g), 0))
    cache_spec = pl.BlockSpec((None, nb, n_mem, heads, d // heads), lambda g: (layer, mix_tile(g), 0, 0, 0))
    params = [lw[name] for name in MIX_PARAMS + BLOCK_PARAMS]
    kern = functools.partial(_layer_prompt_kernel, tile=tile, tiles_per_seq=tps, c_w=c_w,
                             heads=heads, final_norm=final_norm, pad_t=pad_t)
    return pl.pallas_call(
        kern,
        grid=(n_tiles + 1,),
        in_specs=[
            pl.BlockSpec((tile, d), lambda g: (mix_tile(g), 0)),
            pl.BlockSpec((tile, d), lambda g: (blk_tile(g), 0)),
            kv_spec, kv_spec,
            sq_spec, cache_spec, cache_spec,
            *[_layer_spec(p, layer) for p in params],
            _full(gf.shape),
        ],
        out_specs=[
            pl.BlockSpec((tile, d), lambda g: (blk_tile(g), 0)),
            pl.BlockSpec((None, 1, c_w), lambda g: (mix_tile(g) // tps, 0, 0)),
            pl.BlockSpec((None, RG_CONV_W - 1, c_w), lambda g: (mix_tile(g) // tps, 0, 0)),
            pl.BlockSpec((None, CF_CONV_W - 1, c_w), lambda g: (mix_tile(g) // tps, 0, 0)),
            sq_spec,
        ],
        out_shape=[
            jax.ShapeDtypeStruct((rows, d), F32),
            jax.ShapeDtypeStruct((batch, 1, c_w), F32),
            jax.ShapeDtypeStruct((batch, RG_CONV_W - 1, c_w), F32),
            jax.ShapeDtypeStruct((batch, CF_CONV_W - 1, c_w), F32),
            jax.ShapeDtypeStruct(sq_pad.shape, F32),
        ],
        scratch_shapes=[
            pltpu.VMEM((RG_HIST + tile, c_w), F32),
            pltpu.VMEM((CF_HIST + tile + SUBLANES, c_w), F32),
            pltpu.VMEM((SUBLANES, c_w), F32),
            pltpu.VMEM((tile, c_w), F32),
            pltpu.VMEM((tile, c_w), F32),
            pltpu.VMEM((tile, c_w), F32),
            pltpu.VMEM((tile, c_w), F32),
            pltpu.VMEM((tile + SUBLANES, c_w), F32),
            pltpu.VMEM((tile, d), BF16),
            pltpu.VMEM((tile, 2 * c_w), BF16),
            pltpu.VMEM((tile, d), F32),
            pltpu.VMEM((tile, d), BF16),
            pltpu.VMEM((tile, d), BF16),
            pltpu.VMEM((tile, d), BF16),
            pltpu.VMEM((tile, d_ff), BF16),
        ],
        compiler_params=pltpu.CompilerParams(
            dimension_semantics=("arbitrary",), vmem_limit_bytes=VMEM_LIMIT),
        name="layer_prompt",
    )(x2d, x2d, kb, vb, sq_pad, cache_k, cache_v, *params, gf)


def _mix_sample_kernel(x_ref, h0_ref, rb_ref, cb_ref, g_ref, win_ref, wa_ref, wx_ref, vec_ref, wout_ref,
                       ga_ref, wq_ref,
                       xo_ref, h_ref, xr_ref, c_ref, qo_ref,
                       gr_s, xc_s, p_s, q_s, cat_s, *, steps, batch, d_rnn, heads):
    c_w = d_rnn
    vec = _unpack_vec(vec_ref)
    w4_ref, b4_ref, ba_ref, bx_ref, lam_ref = vec["w4"], vec["b4"], vec["ba"], vec["bx"], vec["lam"]
    w31_ref, b31_ref, lng_ref, lnb_ref = vec["w31"], vec["b31"], vec["ln_g"], vec["ln_b"]
    xn = _rms(x_ref[...], g_ref[...]).astype(BF16)
    xr_ref[...] = _dot(xn, win_ref[:, 0:c_w])
    gr_s[...] = _dot(xn, win_ref[:, c_w:2 * c_w])
    c_ref[...] = (
        _dot(xn, win_ref[:, 2 * c_w:3 * c_w]) * _sigmoid(_dot(xn, win_ref[:, 3 * c_w:4 * c_w])))

    def rows_of(t, s):
        return pl.ds(pl.multiple_of(t * batch + s, SUBLANES), SUBLANES)

    def conv4_blk(bi, carry):
        s = bi * SUBLANES
        for t in range(steps):
            acc = b4_ref[...]
            for k in range(RG_CONV_W):
                jj = t + k
                if jj < RG_CONV_W - 1:
                    src = rb_ref[jj, pl.ds(pl.multiple_of(s, SUBLANES), SUBLANES), :]
                else:
                    src = xr_ref[rows_of(jj - (RG_CONV_W - 1), s), :]
                acc = acc + src * w4_ref[k]
            xc_s[rows_of(t, s), :] = acc
        return carry

    lax.fori_loop(0, batch // SUBLANES, conv4_blk, 0)

    _gate_products(xc_s[...].astype(BF16), wa_ref, wx_ref, p_s, q_s)
    nsp = -LRU_C * _softplus(-lam_ref[...])

    def scan_blk(bi, carry):
        s = bi * SUBLANES
        b_rows = pl.ds(pl.multiple_of(s, SUBLANES), SUBLANES)
        h = h0_ref[b_rows, :]
        for t in range(steps):
            rows = rows_of(t, s)
            a, b = _lru_coeffs(p_s[rows, :], q_s[rows, :], xc_s[rows, :], ba_ref[...], bx_ref[...], nsp)
            h = a * h + b
            cat_s[rows, 0:c_w] = h * _gelu_tanh(gr_s[rows, :])
        h_ref[b_rows, :] = h
        return carry

    lax.fori_loop(0, batch // SUBLANES, scan_blk, 0)

    def conv31_blk(bi, carry):
        s = bi * SUBLANES
        for t in range(steps):
            acc = b31_ref[...]
            for k in range(CF_CONV_W):
                jj = t + k
                if jj < CF_CONV_W - 1:
                    src = cb_ref[jj, pl.ds(pl.multiple_of(s, SUBLANES), SUBLANES), :]
                else:
                    src = c_ref[rows_of(jj - (CF_CONV_W - 1), s), :]
                acc = acc + src * w31_ref[k]
            cat_s[rows_of(t, s), c_w:2 * c_w] = _ln_silu(acc, lng_ref[...], lnb_ref[...])
        return carry

    lax.fori_loop(0, batch // SUBLANES, conv31_blk, 0)

    xo = x_ref[...] + _dot(cat_s[...].astype(BF16), wout_ref[...])
    xo_ref[...] = xo
    head_dim = xo.shape[1] // heads
    qo_ref[...] = _dot(_rms(xo, ga_ref[...]).astype(BF16), wq_ref[...]) * (head_dim ** -0.5)


def _mix_sample(x_tm, h0, rb_tm, cb_tm, lw, layer, *, steps, batch, heads):
    rows, d = x_tm.shape
    c_w = lw["vec"].shape[-1]
    kern = functools.partial(_mix_sample_kernel, steps=steps, batch=batch, d_rnn=c_w, heads=heads)
    params = [lw[name] for name in MIX_PARAMS + ("w_out", "g_attn", "w_q")]
    return pl.pallas_call(
        kern,
        grid=(1,),
        in_specs=[_full(x_tm.shape), _layer_spec(h0, layer), _layer_spec(rb_tm, layer),
                  _layer_spec(cb_tm, layer),
                  *[_layer_spec(p, layer) for p in params]],
        out_specs=[_full((rows, d)), _full((batch, c_w)), _full((rows, c_w)), _full((rows, c_w)),
                   _full((rows, d))],
        out_shape=[
            jax.ShapeDtypeStruct((rows, d), F32),
            jax.ShapeDtypeStruct((batch, c_w), F32),
            jax.ShapeDtypeStruct((rows, c_w), F32),
            jax.ShapeDtypeStruct((rows, c_w), F32),
            jax.ShapeDtypeStruct((rows, d), F32),
        ],
        scratch_shapes=[
            pltpu.VMEM((rows, c_w), F32),
            pltpu.VMEM((rows, c_w), F32),
            pltpu.VMEM((rows, c_w), F32),
            pltpu.VMEM((rows, c_w), F32),
            pltpu.VMEM((rows, 2 * c_w), F32),
        ],
        compiler_params=pltpu.CompilerParams(
            dimension_semantics=("arbitrary",), vmem_limit_bytes=VMEM_LIMIT),
        name="mix_sample",
    )(x_tm, h0, rb_tm, cb_tm, *params)


def _sample_attn_stage(q_ref, k_ref, v_ref, o_ref, *, heads, pad_t):
    nb, n_mem, _, head_dim = k_ref.shape
    n_kv = n_mem * heads
    col_head = lax.broadcasted_iota(jnp.int32, (heads * pad_t, n_kv), 1) % heads
    row_head = lax.broadcasted_iota(jnp.int32, (heads * pad_t, n_kv), 0) // pad_t
    own_head = col_head == row_head
    for bb in range(nb):
        yield 2 * _mxu_cost(heads * pad_t, head_dim, n_kv) + n_kv // 2, n_kv // 2
        rows = pl.ds(bb * pad_t, pad_t)
        q = q_ref[rows, :]
        qs = jnp.concatenate([q[:, h * head_dim:(h + 1) * head_dim] for h in range(heads)], axis=0)
        k2 = k_ref[bb].reshape(n_kv, head_dim).astype(BF16)
        v2 = v_ref[bb].reshape(n_kv, head_dim).astype(BF16)
        s = lax.dot_general(qs.astype(BF16), k2, (((1,), (1,)), ((), ())), preferred_element_type=F32)
        s = jnp.where(own_head, s, -1e30)
        p = jnp.exp(s - jnp.max(s, axis=-1, keepdims=True))
        inv = 1.0 / jnp.sum(p, axis=-1, keepdims=True)
        o = _dot(p.astype(BF16), v2) * inv
        for h in range(heads):
            o_ref[rows, h * head_dim:(h + 1) * head_dim] = o[h * pad_t:(h + 1) * pad_t, :]


def _ffn_kernel(x_ref, o_ref, wo_ref, g_ref, wg_ref, wu_ref, wd_ref, gf_ref, xo_ref, h_s, *, final_norm):
    x = x_ref[...] + _dot(o_ref[...].astype(BF16), wo_ref[...])
    y = _swiglu(x, g_ref[...], wg_ref, wu_ref, wd_ref, h_s)
    if final_norm:
        y = _rms(y, gf_ref[...])
    xo_ref[...] = y


def _ffn(x2d, o2d, lw, gf, layer, *, tile, final_norm):
    rows, d = x2d.shape
    d_ff = lw["w_gate"].shape[-1]
    row_spec = pl.BlockSpec((tile, d), lambda i: (i, 0))
    kern = functools.partial(_ffn_kernel, final_norm=final_norm)
    return pl.pallas_call(
        kern,
        grid=(rows // tile,),
        in_specs=[row_spec, row_spec, _layer_spec(lw["w_o"], layer), _layer_spec(lw["g_ffn"], layer),
                  _layer_spec(lw["w_gate"], layer), _layer_spec(lw["w_up"], layer),
                  _layer_spec(lw["w_down"], layer), _full(gf.shape)],
        out_specs=row_spec,
        out_shape=jax.ShapeDtypeStruct((rows, d), F32),
        scratch_shapes=[pltpu.VMEM((tile, d_ff), BF16)],
        compiler_params=pltpu.CompilerParams(
            dimension_semantics=("arbitrary",), vmem_limit_bytes=VMEM_LIMIT),
        name="ffn",
    )(x2d, o2d, lw["w_o"], lw["g_ffn"], lw["w_gate"], lw["w_up"], lw["w_down"], gf)


def _rows8(v):
    return jnp.broadcast_to(v[..., None, :], v.shape[:-1] + (SUBLANES, v.shape[-1]))


def _block_diag_tiles(w, width):
    l, h, i, _ = w.shape
    per_tile = width // i
    eye = jnp.eye(per_tile, dtype=w.dtype)
    tiles = jnp.einsum("lthij,hg->lthigj", w.reshape(l, h // per_tile, per_tile, i, i), eye)
    return tiles.reshape(l, h // per_tile, width, width)


def kernel(x_prompt, x_sample, state_rglru_h, state_rglru_conv, state_conf_conv, cache_mem_k, cache_mem_v, mem_prompt, norm_mix_g, w_in, rg_conv_w, rg_conv_b, rg_wa, rg_ba, rg_wx, rg_bx, rg_lambda, cf_conv_w, cf_conv_b, cf_ln_g, cf_ln_b, w_out, norm_attn_g, norm_mem_g, w_q, w_k, w_v, w_o, norm_ffn_g, w_gate, w_up, w_down, norm_final_g):
    batch, seq, d = x_prompt.shape
    dec_batch, dec_seq, _ = x_sample.shape
    depth = w_in.shape[0]
    n_mem = mem_prompt.shape[1]
    heads = cache_mem_k.shape[3]
    pad_t = SUBLANES

    lw = {
        "g_mix": norm_mix_g[:, None, :],
        "w_in": w_in.astype(BF16),
        "wa": _block_diag_tiles(rg_wa, MXU_DIM).astype(BF16),
        "wx": _block_diag_tiles(rg_wx, MXU_DIM).astype(BF16),
        "vec": _rows8(jnp.concatenate(
            [v if v.ndim == 3 else v[:, None, :] for v in
             (rg_conv_w, rg_conv_b, rg_ba, rg_bx, rg_lambda, cf_conv_w, cf_conv_b, cf_ln_g, cf_ln_b)],
            axis=1)),
        "w_out": w_out.astype(BF16),
        "g_attn": norm_attn_g[:, None, :],
        "w_q": w_q.astype(BF16), "w_o": w_o.astype(BF16),
        "g_ffn": norm_ffn_g[:, None, :],
        "w_gate": w_gate.astype(BF16), "w_up": w_up.astype(BF16), "w_down": w_down.astype(BF16),
    }
    gf = norm_final_g[None, :]

    mem_k, mem_v, mem_kb, mem_vb = _memkv(
        mem_prompt.reshape(batch * n_mem, d), norm_mem_g[:, None, :],
        w_k.astype(BF16), w_v.astype(BF16), tile=512, n_mem=n_mem, heads=heads)

    rb_tm = jnp.transpose(state_rglru_conv, (0, 2, 1, 3))
    cb_tm = jnp.transpose(state_conf_conv, (0, 2, 1, 3))

    xp = x_prompt.reshape(batch * seq, d)
    xs_tm = jnp.transpose(x_sample, (1, 0, 2)).reshape(dec_seq * dec_batch, d)
    xs_bm = None
    p_h, p_rg, p_cf, s_h, s_xr, s_c = [], [], [], [], [], []
    for l in range(depth):
        last = l == depth - 1

        xs_tm, hs, xr_tm, c_tm, sq_tm = _mix_sample(xs_tm, state_rglru_h, rb_tm, cb_tm, lw, l,
                                                    steps=dec_seq, batch=dec_batch, heads=heads)
        s_h.append(hs); s_xr.append(xr_tm); s_c.append(c_tm)
        sq_pad = jnp.pad(jnp.transpose(sq_tm.reshape(dec_seq, dec_batch, d), (1, 0, 2)),
                         ((0, 0), (0, pad_t - dec_seq), (0, 0))).reshape(dec_batch * pad_t, d)

        xp, hp, rgp, cfp, so_pad = _layer_prompt(
            xp, mem_kb, mem_vb, sq_pad, cache_mem_k, cache_mem_v, lw, gf, l, batch=batch, seq=seq,
            n_mem=n_mem, heads=heads, final_norm=last, pad_t=pad_t)
        p_h.append(hp[:, 0, :]); p_rg.append(rgp); p_cf.append(cfp)

        so_bm = so_pad.reshape(dec_batch, pad_t, d)[:, :dec_seq].reshape(dec_batch * dec_seq, d)
        xs_bm = jnp.transpose(xs_tm.reshape(dec_seq, dec_batch, d), (1, 0, 2)).reshape(
            dec_batch * dec_seq, d)
        xs_bm = _ffn(xs_bm, so_bm, lw, gf, l, tile=512, final_norm=last)
        if not last:
            xs_tm = jnp.transpose(xs_bm.reshape(dec_batch, dec_seq, d), (1, 0, 2)).reshape(
                dec_seq * dec_batch, d)

    def new_conv_state(old, fresh_tm, width):
        fresh = jnp.transpose(jnp.stack(fresh_tm).reshape(depth, dec_seq, dec_batch, -1), (0, 2, 1, 3))
        return jnp.concatenate([old, fresh], axis=2)[:, :, -(width - 1):]

    return (xp.reshape(batch, seq, d), xs_bm.reshape(dec_batch, dec_seq, d),
            jnp.stack(p_h), jnp.stack(p_rg), jnp.stack(p_cf),
            mem_k, mem_v,
            jnp.stack(s_h),
            new_conv_state(state_rglru_conv, s_xr, RG_CONV_W),
            new_conv_state(state_conf_conv, s_c, CF_CONV_W))
```

```python
import functools
import math

import jax
import jax.numpy as jnp
from jax import lax
from jax.experimental import pallas as pl
from jax.experimental.pallas import tpu as pltpu

F32 = jnp.float32
BF16 = jnp.bfloat16

EPS = 1e-6
LRU_C = 8.0
RG_CONV_W = 4
CF_CONV_W = 31
SUBLANES = 8
MXU_DIM = 256
CF_HIST = 32
RG_HIST = SUBLANES
CHUNK = 16
OUT_CHUNK = 32
CONV_LANES = 256
CONV_GROUPS_PER_SEGMENT = 6
FFN_COLS = 512
PROMPT_TILE = 256
VMEM_LIMIT = 56 * 1024 * 1024


def _dot(a, b):
    return jnp.dot(a, b, preferred_element_type=F32)


def _rms(x, g):
    return x * lax.rsqrt(jnp.mean(x * x, axis=-1, keepdims=True) + EPS) * g


def _gelu_tanh(x):
    return 0.5 * x * (1.0 + jnp.tanh(math.sqrt(2.0 / math.pi) * (x + 0.044715 * (x * x * x))))


def _sigmoid(x):
    return 0.5 * (jnp.tanh(0.5 * x) + 1.0)


def _silu(x):
    return x * _sigmoid(x)


def _softplus(z):
    return jnp.maximum(z, 0.0) + jnp.log1p(jnp.exp(-jnp.abs(z)))


def _lru_coeffs(r_pre, i_pre, xc, ba, bx, nsp):
    r = _sigmoid(r_pre + ba)
    i = _sigmoid(i_pre + bx)
    log_a = r * nsp
    a = jnp.exp(log_a)
    t = jnp.tanh(log_a)
    mult = jnp.sqrt(-2.0 * t / (1.0 - t))
    return a, mult * (i * xc)


def _ln_silu(y, g, b):
    mu = jnp.mean(y, axis=-1, keepdims=True)
    yc = y - mu
    var = jnp.mean(yc * yc, axis=-1, keepdims=True)
    return _silu(yc * lax.rsqrt(var + EPS) * g + b)


def _attend(q, k, v, heads, head_dim):
    outs = []
    for h in range(heads):
        sl = slice(h * head_dim, (h + 1) * head_dim)
        s = lax.dot_general(q[:, sl], k[:, sl], (((1,), (1,)), ((), ())), preferred_element_type=F32)
        p = jnp.exp(s - jnp.max(s, axis=-1, keepdims=True))
        inv = 1.0 / jnp.sum(p, axis=-1, keepdims=True)
        outs.append(_dot(p.astype(BF16), v[:, sl]) * inv)
    return outs


def _ffn_cols(d_ff):
    return tuple((lo, min(lo + FFN_COLS, d_ff)) for lo in range(0, d_ff, FFN_COLS))


def _swiglu(x, g, wg_ref, wu_ref, wd_ref, h_s):
    z = _rms(x, g).astype(BF16)
    for lo, hi in _ffn_cols(wg_ref.shape[1]):
        h_s[:, lo:hi] = (_silu(_dot(z, wg_ref[:, lo:hi])) * _dot(z, wu_ref[:, lo:hi])).astype(BF16)
    return x + _dot(h_s[...], wd_ref[...])


def _mxu_cost(rows, k, n):
    return (rows // 2) * pl.cdiv(k, MXU_DIM) * pl.cdiv(n, MXU_DIM) // 2


def _full(shape):
    return pl.BlockSpec(shape, lambda *_: (0,) * len(shape))


def _layer_spec(arr, layer):
    zeros = (0,) * (arr.ndim - 1)
    return pl.BlockSpec((None,) + arr.shape[1:], lambda *_: (layer,) + zeros, pipeline_mode=pl.Buffered(1))


def _memkv_kernel(mem_ref, g_ref, wk_ref, wv_ref, k_ref, v_ref, kb_ref, vb_ref):
    m = _rms(mem_ref[...], g_ref[...]).astype(BF16)
    k = _dot(m, wk_ref[...])
    v = _dot(m, wv_ref[...])
    k_ref[...] = k.reshape(k_ref.shape)
    v_ref[...] = v.reshape(v_ref.shape)
    kb_ref[...] = k.astype(BF16)
    vb_ref[...] = v.astype(BF16)


def _memkv(mem2d, g, wk, wv, tile, n_mem, heads):
    depth, d, _ = wk.shape
    rows = mem2d.shape[0]
    row_spec = pl.BlockSpec((None, tile, d), lambda l, i: (l, i, 0))
    out5_spec = pl.BlockSpec((None, tile // n_mem, n_mem, heads, d // heads), lambda l, i: (l, i, 0, 0, 0))
    out5_shape = jax.ShapeDtypeStruct((depth, rows // n_mem, n_mem, heads, d // heads), F32)
    w_spec = pl.BlockSpec((None, d, d), lambda l, i: (l, 0, 0))
    return pl.pallas_call(
        _memkv_kernel,
        grid=(depth, rows // tile),
        in_specs=[
            pl.BlockSpec((tile, d), lambda l, i: (i, 0)),
            pl.BlockSpec((None, 1, d), lambda l, i: (l, 0, 0)),
            w_spec, w_spec,
        ],
        out_specs=[out5_spec, out5_spec, row_spec, row_spec],
        out_shape=[
            out5_shape,
            out5_shape,
            jax.ShapeDtypeStruct((depth, rows, d), BF16),
            jax.ShapeDtypeStruct((depth, rows, d), BF16),
        ],
        compiler_params=pltpu.CompilerParams(
            dimension_semantics=("arbitrary", "arbitrary"), vmem_limit_bytes=VMEM_LIMIT),
        name="memkv",
    )(mem2d, g, wk, wv)


MIX_PARAMS = ("g_mix", "w_in", "wa", "wx", "vec")
VEC_ROWS = (("w4", RG_CONV_W), ("b4", None), ("ba", None), ("bx", None), ("lam", None),
            ("w31", CF_CONV_W), ("b31", None), ("ln_g", None), ("ln_b", None))


def _unpack_vec(vec_ref):
    views, row = {}, 0
    for name, count in VEC_ROWS:
        views[name] = vec_ref.at[row] if count is None else vec_ref.at[row:row + count]
        row += count or 1
    return views


def _gate_products(lhs, wa_ref, wx_ref, p_s, q_s):
    n_tiles, width = wa_ref.shape[0], wa_ref.shape[1]
    for j in range(n_tiles):
        cols = slice(j * width, (j + 1) * width)
        p_s[:, cols] = _dot(lhs[:, cols], wa_ref[j])
        q_s[:, cols] = _dot(lhs[:, cols], wx_ref[j])
BLOCK_PARAMS = ("w_out", "g_attn", "w_q", "w_o", "g_ffn", "w_gate", "w_up", "w_down")


def _group_rows(gi, offset=0):
    return pl.ds(gi * SUBLANES + offset, SUBLANES)


def _mix_stage(x_ref, prm, xr_hist, c_hist, hcar, gr_s, xc_s, p_s, q_s, yc_s, xn_s, cat_s, *, tile, c_w):
    win_ref = prm["w_in"]
    n_groups = tile // SUBLANES
    sub8 = lax.broadcasted_iota(jnp.int32, (SUBLANES, c_w), 0)

    yield 2 * _mxu_cost(tile, win_ref.shape[0], c_w), 2 * tile
    xn_s[...] = _rms(x_ref[...], prm["g_mix"][...]).astype(BF16)
    c_hist[pl.ds(CF_HIST, tile), :] = (
        _dot(xn_s[...], win_ref[:, 2 * c_w:3 * c_w]) * _sigmoid(_dot(xn_s[...], win_ref[:, 3 * c_w:4 * c_w])))
    yield 2 * _mxu_cost(tile, win_ref.shape[0], c_w), tile // 4
    xr_hist[pl.ds(RG_HIST, tile), :] = _dot(xn_s[...], win_ref[:, 0:c_w])
    gr_s[...] = _dot(xn_s[...], win_ref[:, c_w:2 * c_w])

    first_tap_off = CF_HIST - (CF_CONV_W - 1)
    n_hist_groups = CF_HIST // SUBLANES + 1
    w31_ref = prm["w31"]
    for lane0 in range(0, c_w, CONV_LANES):
        lanes = pl.ds(lane0, CONV_LANES)
        sub_l = lax.broadcasted_iota(jnp.int32, (SUBLANES, CONV_LANES), 0)
        zero = jnp.zeros((SUBLANES, CONV_LANES), F32)
        p0_prev, rolled_prev = zero, (zero,) * (SUBLANES - 1)
        for si in range(n_groups + 1):
            if si % CONV_GROUPS_PER_SEGMENT == 0:
                yield 0, 26 * (CONV_LANES // 128) * min(CONV_GROUPS_PER_SEGMENT, n_groups + 1 - si)
            hist = [c_hist[_group_rows(si, SUBLANES * q), lanes] for q in range(n_hist_groups)]
            parts = []
            for res in range(SUBLANES):
                part = None
                for q in range(n_hist_groups):
                    k = SUBLANES * q + res - first_tap_off
                    if 0 <= k < CF_CONV_W:
                        term = hist[q] * w31_ref[k, :, lanes]
                        part = term if part is None else part + term
                parts.append(part)
            rolled = tuple(pltpu.roll(parts[res], SUBLANES - res, axis=0) for res in range(1, SUBLANES))
            y = p0_prev
            for res in range(1, SUBLANES):
                y = y + jnp.where(sub_l < SUBLANES - res, rolled_prev[res - 1], rolled[res - 1])
            yc_s[_group_rows(si), lanes] = y
            p0_prev, rolled_prev = parts[0], rolled

    yield 0, 20 * n_groups
    w4_ref = prm["w4"]
    x_hist = xr_hist[pl.ds(0, RG_HIST), :]
    prev_rolled = [pltpu.roll(x_hist, dly, axis=0) for dly in range(1, RG_CONV_W)]
    for gi in range(n_groups):
        xg = xr_hist[_group_rows(gi, RG_HIST), :]
        acc = xg * w4_ref[RG_CONV_W - 1] + prm["b4"][...]
        rolled = []
        for dly in range(1, RG_CONV_W):
            rolled.append(pltpu.roll(xg, dly, axis=0))
            acc = acc + jnp.where(sub8 >= dly, rolled[-1], prev_rolled[dly - 1]) * w4_ref[RG_CONV_W - 1 - dly]
        xc_s[_group_rows(gi), :] = acc
        prev_rolled = rolled

    n_gate_tiles, gate_width = prm["wa"].shape[0], prm["wa"].shape[1]
    yield 2 * n_gate_tiles * _mxu_cost(tile, gate_width, gate_width), tile // 4
    xcb_s = xn_s.at[:, 0:c_w]
    xcb_s[...] = xc_s[...].astype(BF16)
    _gate_products(xcb_s, prm["wa"], prm["wx"], p_s, q_s)

    nsp = -LRU_C * _softplus(-prm["lam"][...])
    n_grp = CHUNK // SUBLANES
    grp = (n_grp, SUBLANES, c_w)
    sub = lax.broadcasted_iota(jnp.int32, grp, 1)
    for ci in range(tile // CHUNK):
        if ci % 2 == 0:
            yield 0, 240
        rows = pl.ds(ci * CHUNK, CHUNK)
        a, b = _lru_coeffs(p_s[rows, :].reshape(grp), q_s[rows, :].reshape(grp),
                           xc_s[rows, :].reshape(grp), prm["ba"][...], prm["bx"][...], nsp)
        for dly in (1, 2, 4):
            ra = pltpu.roll(a, dly, axis=1)
            rb = pltpu.roll(b, dly, axis=1)
            keep = sub >= dly
            b = a * jnp.where(keep, rb, 0.0) + b
            a = a * jnp.where(keep, ra, 1.0)
        p_s[rows, :] = a.reshape(CHUNK, c_w)
        q_s[rows, :] = b.reshape(CHUNK, c_w)

    yield 0, 21 * n_groups
    hb = hcar[...]
    for gi in range(n_groups):
        rows = _group_rows(gi)
        h = p_s[rows, :] * hb + q_s[rows, :]
        q_s[rows, :] = h
        hb = jnp.broadcast_to(h[SUBLANES - 1:SUBLANES, :], (SUBLANES, c_w))
    hcar[...] = hb

    for ci in range(tile // OUT_CHUNK):
        yield 0, 220
        rows = pl.ds(ci * OUT_CHUNK, OUT_CHUNK)
        y_r = q_s[rows, :] * _gelu_tanh(gr_s[rows, :])
        cat_s[rows, 0:c_w] = y_r.astype(BF16)
        conv = yc_s[pl.ds(ci * OUT_CHUNK + SUBLANES, OUT_CHUNK), :].reshape(
            OUT_CHUNK // SUBLANES, SUBLANES, c_w)
        y_c = _ln_silu(conv + prm["b31"][...], prm["ln_g"][...], prm["ln_b"][...])
        cat_s[rows, c_w:2 * c_w] = y_c.reshape(OUT_CHUNK, c_w).astype(BF16)


def _block_stage(x_ref, cat_s, k_ref, v_ref, prm, gf_ref, xo_ref, xb_s, zb_s, qb_s, o_s, h_s,
                 *, heads, final_norm):
    d = x_ref.shape[1]
    head_dim = d // heads
    rows = x_ref.shape[0]
    yield _mxu_cost(rows, d, d), rows // 2
    xb_s[...] = x_ref[...] + _dot(cat_s[...], prm["w_out"][...])
    yield _mxu_cost(rows, d, d), 2 * rows
    zb_s[...] = _rms(xb_s[...], prm["g_attn"][...]).astype(BF16)
    qb_s[...] = (_dot(zb_s[...], prm["w_q"][...]) * (head_dim ** -0.5)).astype(BF16)
    for h in range(heads):
        yield rows // 2, rows
        sl = slice(h * head_dim, (h + 1) * head_dim)
        (o,) = _attend(qb_s[:, sl], k_ref[:, sl], v_ref[:, sl], 1, head_dim)
        o_s[:, sl] = o.astype(BF16)
    yield _mxu_cost(rows, d, d), rows // 2
    xb_s[...] = xb_s[...] + _dot(o_s[...], prm["w_o"][...])
    yield 0, rows
    zb_s[...] = _rms(xb_s[...], prm["g_ffn"][...]).astype(BF16)
    wg_ref, wu_ref, wd_ref = prm["w_gate"], prm["w_up"], prm["w_down"]
    for lo, hi in _ffn_cols(wg_ref.shape[1]):
        yield 2 * _mxu_cost(rows, d, hi - lo), (hi - lo) // 2
        h_s[:, lo:hi] = (_silu(_dot(zb_s[...], wg_ref[:, lo:hi])) * _dot(zb_s[...], wu_ref[:, lo:hi])
                         ).astype(BF16)
    for lo, hi in _ffn_cols(d):
        yield _mxu_cost(rows, wd_ref.shape[0], hi - lo), rows // 2
        xo_ref[:, lo:hi] = xb_s[:, lo:hi] + _dot(h_s[...], wd_ref[:, lo:hi])
    if final_norm:
        yield 0, rows
        xo_ref[...] = _rms(xo_ref[...], gf_ref[...])


def _interleave(*stages):
    pending = {stage: next(stage) for stage in stages}
    clock = {stage: i for i, stage in enumerate(stages)}
    while pending:
        stage = min(pending, key=clock.get)
        clock[stage] += max(pending.pop(stage))
        try:
            pending[stage] = next(stage)
        except StopIteration:
            pass


def _layer_prompt_kernel(*refs, tile, tiles_per_seq, c_w, heads, final_norm, pad_t):
    n_mix, n_blk = len(MIX_PARAMS), len(BLOCK_PARAMS)
    xa_ref, xb_ref, k_ref, v_ref, sq_ref, sk_ref, sv_ref = refs[:7]
    refs = refs[7:]
    mix_prm = dict(zip(MIX_PARAMS, refs[:n_mix]))
    mix_prm.update(_unpack_vec(mix_prm["vec"]))
    blk_prm = dict(zip(BLOCK_PARAMS, refs[n_mix:n_mix + n_blk]))
    gf_ref = refs[n_mix + n_blk]
    xo_ref, h_ref, rg_ref, cf_ref, so_ref = refs[1 + n_mix + n_blk:6 + n_mix + n_blk]
    (xr_hist, c_hist, hcar, gr_s, xc_s, p_s, q_s, yc_s, xn_s, cat_s,
     xb_s, zb_s, qb_s, o_s, h_s) = refs[6 + n_mix + n_blk:]

    g = pl.program_id(0)
    pos = g % tiles_per_seq

    @pl.when(g == 0)
    def _():
        cat_s[...] = jnp.zeros(cat_s.shape, BF16)
        c_hist[pl.ds(CF_HIST + tile, SUBLANES), :] = jnp.zeros((SUBLANES, c_w), F32)

    @pl.when(pos == 0)
    def _():
        xr_hist[pl.ds(0, RG_HIST), :] = jnp.zeros((RG_HIST, c_w), F32)
        c_hist[pl.ds(0, CF_HIST), :] = jnp.zeros((CF_HIST, c_w), F32)
        hcar[...] = jnp.zeros((SUBLANES, c_w), F32)

    _interleave(
        _block_stage(xb_ref, cat_s, k_ref, v_ref, blk_prm, gf_ref, xo_ref, xb_s, zb_s, qb_s, o_s, h_s,
                     heads=heads, final_norm=final_norm),
        _mix_stage(xa_ref, mix_prm, xr_hist, c_hist, hcar, gr_s, xc_s, p_s, q_s, yc_s, xn_s, cat_s,
                   tile=tile, c_w=c_w),
        _sample_attn_stage(sq_ref, sk_ref, sv_ref, so_ref, heads=heads, pad_t=pad_t))

    @pl.when(pos == tiles_per_seq - 1)
    def _():
        h_ref[...] = hcar[0:1, :]
        rg_ref[...] = xr_hist[pl.ds(RG_HIST + tile - (RG_CONV_W - 1), RG_CONV_W - 1), :]
        cf_ref[...] = c_hist[pl.ds(CF_HIST + tile - (CF_CONV_W - 1), CF_CONV_W - 1), :]

    xr_hist[pl.ds(0, RG_HIST), :] = xr_hist[pl.ds(tile, RG_HIST), :]
    c_hist[pl.ds(0, CF_HIST), :] = c_hist[pl.ds(tile, CF_HIST), :]


def _layer_prompt(x2d, kb, vb, sq_pad, cache_k, cache_v, lw, gf, layer,
                  *, batch, seq, n_mem, heads, final_norm, pad_t):
    rows, d = x2d.shape
    tile = PROMPT_TILE
    c_w = lw["vec"].shape[-1]
    d_ff = lw["w_gate"].shape[-1]
    tps = seq // tile
    n_tiles = batch * tps
    dec_batch = cache_k.shape[1]
    nb = dec_batch // n_tiles
    assert nb * n_tiles == dec_batch and sq_pad.shape[0] == dec_batch * pad_t

    def mix_tile(g):
        return jnp.minimum(g, n_tiles - 1)

    def blk_tile(g):
        return jnp.maximum(g - 1, 0)

    kv_spec = pl.BlockSpec((None, n_mem, d), lambda g: (layer, blk_tile(g) // tps, 0))
    sq_spec = pl.BlockSpec((nb * pad_t, d), lambda g: (mix_tile(g), 0))
    cache_spec = pl.BlockSpec((None, nb, n_mem, heads, d // heads), lambda g: (layer, mix_tile(g), 0, 0, 0))
    params = [lw[name] for name in MIX_PARAMS + BLOCK_PARAMS]
    kern = functools.partial(_layer_prompt_kernel, tile=tile, tiles_per_seq=tps, c_w=c_w,
                             heads=heads, final_norm=final_norm, pad_t=pad_t)
    return pl.pallas_call(
        kern,
        grid=(n_tiles + 1,),
        in_specs=[
            pl.BlockSpec((tile, d), lambda g: (mix_tile(g), 0)),
            pl.BlockSpec((tile, d), lambda g: (blk_tile(g), 0)),
            kv_spec, kv_spec,
            sq_spec, cache_spec, cache_spec,
            *[_layer_spec(p, layer) for p in params],
            _full(gf.shape),
        ],
        out_specs=[
            pl.BlockSpec((tile, d), lambda g: (blk_tile(g), 0)),
            pl.BlockSpec((None, 1, c_w), lambda g: (mix_tile(g) // tps, 0, 0)),
            pl.BlockSpec((None, RG_CONV_W - 1, c_w), lambda g: (mix_tile(g) // tps, 0, 0)),
            pl.BlockSpec((None, CF_CONV_W - 1, c_w), lambda g: (mix_tile(g) // tps, 0, 0)),
            sq_spec,
        ],
        out_shape=[
            jax.ShapeDtypeStruct((rows, d), F32),
            jax.ShapeDtypeStruct((batch, 1, c_w), F32),
            jax.ShapeDtypeStruct((batch, RG_CONV_W - 1, c_w), F32),
            jax.ShapeDtypeStruct((batch, CF_CONV_W - 1, c_w), F32),
            jax.ShapeDtypeStruct(sq_pad.shape, F32),
        ],
        scratch_shapes=[
            pltpu.VMEM((RG_HIST + tile, c_w), F32),
            pltpu.VMEM((CF_HIST + tile + SUBLANES, c_w), F32),
            pltpu.VMEM((SUBLANES, c_w), F32),
            pltpu.VMEM((tile, c_w), F32),
            pltpu.VMEM((tile, c_w), F32),
            pltpu.VMEM((tile, c_w), F32),
            pltpu.VMEM((tile, c_w), F32),
            pltpu.VMEM((tile + SUBLANES, c_w), F32),
            pltpu.VMEM((tile, d), BF16),
            pltpu.VMEM((tile, 2 * c_w), BF16),
            pltpu.VMEM((tile, d), F32),
            pltpu.VMEM((tile, d), BF16),
            pltpu.VMEM((tile, d), BF16),
            pltpu.VMEM((tile, d), BF16),
            pltpu.VMEM((tile, d_ff), BF16),
        ],
        compiler_params=pltpu.CompilerParams(
            dimension_semantics=("arbitrary",), vmem_limit_bytes=VMEM_LIMIT),
        name="layer_prompt",
    )(x2d, x2d, kb, vb, sq_pad, cache_k, cache_v, *params, gf)


def _mix_sample_kernel(x_ref, h0_ref, rb_ref, cb_ref, g_ref, win_ref, wa_ref, wx_ref, vec_ref, wout_ref,
                       ga_ref, wq_ref,
                       xo_ref, h_ref, xr_ref, c_ref, qo_ref,
                       gr_s, xc_s, p_s, q_s, cat_s, *, steps, batch, d_rnn, heads):
    c_w = d_rnn
    vec = _unpack_vec(vec_ref)
    w4_ref, b4_ref, ba_ref, bx_ref, lam_ref = vec["w4"], vec["b4"], vec["ba"], vec["bx"], vec["lam"]
    w31_ref, b31_ref, lng_ref, lnb_ref = vec["w31"], vec["b31"], vec["ln_g"], vec["ln_b"]
    xn = _rms(x_ref[...], g_ref[...]).astype(BF16)
    xr_ref[...] = _dot(xn, win_ref[:, 0:c_w])
    gr_s[...] = _dot(xn, win_ref[:, c_w:2 * c_w])
    c_ref[...] = (
        _dot(xn, win_ref[:, 2 * c_w:3 * c_w]) * _sigmoid(_dot(xn, win_ref[:, 3 * c_w:4 * c_w])))

    def rows_of(t, s):
        return pl.ds(pl.multiple_of(t * batch + s, SUBLANES), SUBLANES)

    def conv4_blk(bi, carry):
        s = bi * SUBLANES
        for t in range(steps):
            acc = b4_ref[...]
            for k in range(RG_CONV_W):
                jj = t + k
                if jj < RG_CONV_W - 1:
                    src = rb_ref[jj, pl.ds(pl.multiple_of(s, SUBLANES), SUBLANES), :]
                else:
                    src = xr_ref[rows_of(jj - (RG_CONV_W - 1), s), :]
                acc = acc + src * w4_ref[k]
            xc_s[rows_of(t, s), :] = acc
        return carry

    lax.fori_loop(0, batch // SUBLANES, conv4_blk, 0)

    _gate_products(xc_s[...].astype(BF16), wa_ref, wx_ref, p_s, q_s)
    nsp = -LRU_C * _softplus(-lam_ref[...])

    def scan_blk(bi, carry):
        s = bi * SUBLANES
        b_rows = pl.ds(pl.multiple_of(s, SUBLANES), SUBLANES)
        h = h0_ref[b_rows, :]
        for t in range(steps):
            rows = rows_of(t, s)
            a, b = _lru_coeffs(p_s[rows, :], q_s[rows, :], xc_s[rows, :], ba_ref[...], bx_ref[...], nsp)
            h = a * h + b
            cat_s[rows, 0:c_w] = h * _gelu_tanh(gr_s[rows, :])
        h_ref[b_rows, :] = h
        return carry

    lax.fori_loop(0, batch // SUBLANES, scan_blk, 0)

    def conv31_blk(bi, carry):
        s = bi * SUBLANES
        for t in range(steps):
            acc = b31_ref[...]
            for k in range(CF_CONV_W):
                jj = t + k
                if jj < CF_CONV_W - 1:
                    src = cb_ref[jj, pl.ds(pl.multiple_of(s, SUBLANES), SUBLANES), :]
                else:
                    src = c_ref[rows_of(jj - (CF_CONV_W - 1), s), :]
                acc = acc + src * w31_ref[k]
            cat_s[rows_of(t, s), c_w:2 * c_w] = _ln_silu(acc, lng_ref[...], lnb_ref[...])
        return carry

    lax.fori_loop(0, batch // SUBLANES, conv31_blk, 0)

    xo = x_ref[...] + _dot(cat_s[...].astype(BF16), wout_ref[...])
    xo_ref[...] = xo
    head_dim = xo.shape[1] // heads
    qo_ref[...] = _dot(_rms(xo, ga_ref[...]).astype(BF16), wq_ref[...]) * (head_dim ** -0.5)


def _mix_sample(x_tm, h0, rb_tm, cb_tm, lw, layer, *, steps, batch, heads):
    rows, d = x_tm.shape
    c_w = lw["vec"].shape[-1]
    kern = functools.partial(_mix_sample_kernel, steps=steps, batch=batch, d_rnn=c_w, heads=heads)
    params = [lw[name] for name in MIX_PARAMS + ("w_out", "g_attn", "w_q")]
    return pl.pallas_call(
        kern,
        grid=(1,),
        in_specs=[_full(x_tm.shape), _layer_spec(h0, layer), _layer_spec(rb_tm, layer),
                  _layer_spec(cb_tm, layer),
                  *[_layer_spec(p, layer) for p in params]],
        out_specs=[_full((rows, d)), _full((batch, c_w)), _full((rows, c_w)), _full((rows, c_w)),
                   _full((rows, d))],
        out_shape=[
            jax.ShapeDtypeStruct((rows, d), F32),
            jax.ShapeDtypeStruct((batch, c_w), F32),
            jax.ShapeDtypeStruct((rows, c_w), F32),
            jax.ShapeDtypeStruct((rows, c_w), F32),
            jax.ShapeDtypeStruct((rows, d), F32),
        ],
        scratch_shapes=[
            pltpu.VMEM((rows, c_w), F32),
            pltpu.VMEM((rows, c_w), F32),
            pltpu.VMEM((rows, c_w), F32),
            pltpu.VMEM((rows, c_w), F32),
            pltpu.VMEM((rows, 2 * c_w), F32),
        ],
        compiler_params=pltpu.CompilerParams(
            dimension_semantics=("arbitrary",), vmem_limit_bytes=VMEM_LIMIT),
        name="mix_sample",
    )(x_tm, h0, rb_tm, cb_tm, *params)


def _sample_attn_stage(q_ref, k_ref, v_ref, o_ref, *, heads, pad_t):
    nb, n_mem, _, head_dim = k_ref.shape
    n_kv = n_mem * heads
    col_head = lax.broadcasted_iota(jnp.int32, (heads * pad_t, n_kv), 1) % heads
    row_head = lax.broadcasted_iota(jnp.int32, (heads * pad_t, n_kv), 0) // pad_t
    own_head = col_head == row_head
    for bb in range(nb):
        yield 2 * _mxu_cost(heads * pad_t, head_dim, n_kv) + n_kv // 2, n_kv // 2
        rows = pl.ds(bb * pad_t, pad_t)
        q = q_ref[rows, :]
        qs = jnp.concatenate([q[:, h * head_dim:(h + 1) * head_dim] for h in range(heads)], axis=0)
        k2 = k_ref[bb].reshape(n_kv, head_dim).astype(BF16)
        v2 = v_ref[bb].reshape(n_kv, head_dim).astype(BF16)
        s = lax.dot_general(qs.astype(BF16), k2, (((1,), (1,)), ((), ())), preferred_element_type=F32)
        s = jnp.where(own_head, s, -1e30)
        p = jnp.exp(s - jnp.max(s, axis=-1, keepdims=True))
        inv = 1.0 / jnp.sum(p, axis=-1, keepdims=True)
        o = _dot(p.astype(BF16), v2) * inv
        for h in range(heads):
            o_ref[rows, h * head_dim:(h + 1) * head_dim] = o[h * pad_t:(h + 1) * pad_t, :]


def _ffn_kernel(x_ref, o_ref, wo_ref, g_ref, wg_ref, wu_ref, wd_ref, gf_ref, xo_ref, h_s, *, final_norm):
    x = x_ref[...] + _dot(o_ref[...].astype(BF16), wo_ref[...])
    y = _swiglu(x, g_ref[...], wg_ref, wu_ref, wd_ref, h_s)
    if final_norm:
        y = _rms(y, gf_ref[...])
    xo_ref[...] = y


def _ffn(x2d, o2d, lw, gf, layer, *, tile, final_norm):
    rows, d = x2d.shape
    d_ff = lw["w_gate"].shape[-1]
    row_spec = pl.BlockSpec((tile, d), lambda i: (i, 0))
    kern = functools.partial(_ffn_kernel, final_norm=final_norm)
    return pl.pallas_call(
        kern,
        grid=(rows // tile,),
        in_specs=[row_spec, row_spec, _layer_spec(lw["w_o"], layer), _layer_spec(lw["g_ffn"], layer),
                  _layer_spec(lw["w_gate"], layer), _layer_spec(lw["w_up"], layer),
                  _layer_spec(lw["w_down"], layer), _full(gf.shape)],
        out_specs=row_spec,
        out_shape=jax.ShapeDtypeStruct((rows, d), F32),
        scratch_shapes=[pltpu.VMEM((tile, d_ff), BF16)],
        compiler_params=pltpu.CompilerParams(
            dimension_semantics=("arbitrary",), vmem_limit_bytes=VMEM_LIMIT),
        name="ffn",
    )(x2d, o2d, lw["w_o"], lw["g_ffn"], lw["w_gate"], lw["w_up"], lw["w_down"], gf)


def _rows8(v):
    return jnp.broadcast_to(v[..., None, :], v.shape[:-1] + (SUBLANES, v.shape[-1]))


def _block_diag_tiles(w, width):
    l, h, i, _ = w.shape
    per_tile = width // i
    eye = jnp.eye(per_tile, dtype=w.dtype)
    tiles = jnp.einsum("lthij,hg->lthigj", w.reshape(l, h // per_tile, per_tile, i, i), eye)
    return tiles.reshape(l, h // per_tile, width, width)


def kernel(x_prompt, x_sample, state_rglru_h, state_rglru_conv, state_conf_conv, cache_mem_k, cache_mem_v, mem_prompt, norm_mix_g, w_in, rg_conv_w, rg_conv_b, rg_wa, rg_ba, rg_wx, rg_bx, rg_lambda, cf_conv_w, cf_conv_b, cf_ln_g, cf_ln_b, w_out, norm_attn_g, norm_mem_g, w_q, w_k, w_v, w_o, norm_ffn_g, w_gate, w_up, w_down, norm_final_g):
    batch, seq, d = x_prompt.shape
    dec_batch, dec_seq, _ = x_sample.shape
    depth = w_in.shape[0]
    n_mem = mem_prompt.shape[1]
    heads = cache_mem_k.shape[3]
    pad_t = SUBLANES

    lw = {
        "g_mix": norm_mix_g[:, None, :],
        "w_in": w_in.astype(BF16),
        "wa": _block_diag_tiles(rg_wa, MXU_DIM).astype(BF16),
        "wx": _block_diag_tiles(rg_wx, MXU_DIM).astype(BF16),
        "vec": _rows8(jnp.concatenate(
            [v if v.ndim == 3 else v[:, None, :] for v in
             (rg_conv_w, rg_conv_b, rg_ba, rg_bx, rg_lambda, cf_conv_w, cf_conv_b, cf_ln_g, cf_ln_b)],
            axis=1)),
        "w_out": w_out.astype(BF16),
        "g_attn": norm_attn_g[:, None, :],
        "w_q": w_q.astype(BF16), "w_o": w_o.astype(BF16),
        "g_ffn": norm_ffn_g[:, None, :],
        "w_gate": w_gate.astype(BF16), "w_up": w_up.astype(BF16), "w_down": w_down.astype(BF16),
    }
    gf = norm_final_g[None, :]

    mem_k, mem_v, mem_kb, mem_vb = _memkv(
        mem_prompt.reshape(batch * n_mem, d), norm_mem_g[:, None, :],
        w_k.astype(BF16), w_v.astype(BF16), tile=512, n_mem=n_mem, heads=heads)

    rb_tm = jnp.transpose(state_rglru_conv, (0, 2, 1, 3))
    cb_tm = jnp.transpose(state_conf_conv, (0, 2, 1, 3))

    xp = x_prompt.reshape(batch * seq, d)
    xs_tm = jnp.transpose(x_sample, (1, 0, 2)).reshape(dec_seq * dec_batch, d)
    p_h, p_rg, p_cf, s_h, s_xr, s_c = [], [], [], [], [], []
    for l in range(depth):
        last = l == depth - 1

        xs_tm, hs, xr_tm, c_tm, sq_tm = _mix_sample(xs_tm, state_rglru_h, rb_tm, cb_tm, lw, l,
                                                    steps=dec_seq, batch=dec_batch, heads=heads)
        s_h.append(hs); s_xr.append(xr_tm); s_c.append(c_tm)
        sq_pad = jnp.pad(jnp.transpose(sq_tm.reshape(dec_seq, dec_batch, d), (1, 0, 2)),
                         ((0, 0), (0, pad_t - dec_seq), (0, 0))).reshape(dec_batch * pad_t, d)

        xp, hp, rgp, cfp, so_pad = _layer_prompt(
            xp, mem_kb, mem_vb, sq_pad, cache_mem_k, cache_mem_v, lw, gf, l, batch=batch, seq=seq,
            n_mem=n_mem, heads=heads, final_norm=last, pad_t=pad_t)
        p_h.append(hp[:, 0, :]); p_rg.append(rgp); p_cf.append(cfp)

        so_tm = jnp.transpose(so_pad.reshape(dec_batch, pad_t, d)[:, :dec_seq], (1, 0, 2)).reshape(
            dec_seq * dec_batch, d)
        xs_tm = _ffn(xs_tm, so_tm, lw, gf, l, tile=dec_seq * dec_batch, final_norm=last)

    def new_conv_state(old, fresh_tm, width):
        fresh = jnp.transpose(jnp.stack(fresh_tm).reshape(depth, dec_seq, dec_batch, -1), (0, 2, 1, 3))
        return jnp.concatenate([old, fresh], axis=2)[:, :, -(width - 1):]

    y_sample = jnp.transpose(xs_tm.reshape(dec_seq, dec_batch, d), (1, 0, 2))
    return (xp.reshape(batch, seq, d), y_sample,
            jnp.stack(p_h), jnp.stack(p_rg), jnp.stack(p_cf),
            mem_k, mem_v,
            jnp.stack(s_h),
            new_conv_state(state_rglru_conv, s_xr, RG_CONV_W),
            new_conv_state(state_conf_conv, s_c, CF_CONV_W))
```

```python
import functools
import math

import jax
import jax.numpy as jnp
from jax import lax
from jax.experimental import pallas as pl
from jax.experimental.pallas import tpu as pltpu

F32 = jnp.float32
BF16 = jnp.bfloat16

EPS = 1e-6
LRU_C = 8.0
RG_CONV_W = 4
CF_CONV_W = 31
SUBLANES = 8
MXU_DIM = 256
CF_HIST = 32
RG_HIST = SUBLANES
CHUNK = 16
OUT_CHUNK = 32
CONV_LANES = 256
CONV_GROUPS_PER_SEGMENT = 6
FFN_COLS = 512
PROMPT_TILE = 256
VMEM_LIMIT = 56 * 1024 * 1024


def _dot(a, b):
    return jnp.dot(a, b, preferred_element_type=F32)


def _rms(x, g):
    return x * lax.rsqrt(jnp.mean(x * x, axis=-1, keepdims=True) + EPS) * g


def _gelu_tanh(x):
    return 0.5 * x * (1.0 + jnp.tanh(math.sqrt(2.0 / math.pi) * (x + 0.044715 * (x * x * x))))


def _sigmoid(x):
    return 0.5 * (jnp.tanh(0.5 * x) + 1.0)


def _silu(x):
    return x * _sigmoid(x)


def _softplus(z):
    return jnp.maximum(z, 0.0) + jnp.log1p(jnp.exp(-jnp.abs(z)))


def _lru_coeffs(r_pre, i_pre, xc, ba, bx, nsp):
    r = _sigmoid(r_pre + ba)
    i = _sigmoid(i_pre + bx)
    log_a = r * nsp
    a = jnp.exp(log_a)
    t = jnp.tanh(log_a)
    mult = jnp.sqrt(-2.0 * t / (1.0 - t))
    return a, mult * (i * xc)


def _ln_silu(y, g, b):
    mu = jnp.mean(y, axis=-1, keepdims=True)
    yc = y - mu
    var = jnp.mean(yc * yc, axis=-1, keepdims=True)
    return _silu(yc * lax.rsqrt(var + EPS) * g + b)


def _attend(q, k, v, heads, head_dim):
    outs = []
    for h in range(heads):
        sl = slice(h * head_dim, (h + 1) * head_dim)
        s = lax.dot_general(q[:, sl], k[:, sl], (((1,), (1,)), ((), ())), preferred_element_type=F32)
        p = jnp.exp(s - jnp.max(s, axis=-1, keepdims=True))
        inv = 1.0 / jnp.sum(p, axis=-1, keepdims=True)
        outs.append(_dot(p.astype(BF16), v[:, sl]) * inv)
    return outs


def _ffn_cols(d_ff):
    return tuple((lo, min(lo + FFN_COLS, d_ff)) for lo in range(0, d_ff, FFN_COLS))


def _swiglu(x, g, wg_ref, wu_ref, wd_ref, h_s):
    z = _rms(x, g).astype(BF16)
    for lo, hi in _ffn_cols(wg_ref.shape[1]):
        h_s[:, lo:hi] = (_silu(_dot(z, wg_ref[:, lo:hi])) * _dot(z, wu_ref[:, lo:hi])).astype(BF16)
    return x + _dot(h_s[...], wd_ref[...])


def _mxu_cost(rows, k, n):
    return (rows // 2) * pl.cdiv(k, MXU_DIM) * pl.cdiv(n, MXU_DIM) // 2


def _full(shape):
    return pl.BlockSpec(shape, lambda *_: (0,) * len(shape))


def _layer_spec(arr, layer):
    zeros = (0,) * (arr.ndim - 1)
    return pl.BlockSpec((None,) + arr.shape[1:], lambda *_: (layer,) + zeros, pipeline_mode=pl.Buffered(1))


def _memkv_kernel(mem_ref, g_ref, wk_ref, wv_ref, k_ref, v_ref, kb_ref, vb_ref):
    m = _rms(mem_ref[...], g_ref[...]).astype(BF16)
    k = _dot(m, wk_ref[...].astype(BF16))
    v = _dot(m, wv_ref[...].astype(BF16))
    k_ref[...] = k.reshape(k_ref.shape)
    v_ref[...] = v.reshape(v_ref.shape)
    kb_ref[...] = k.astype(BF16)
    vb_ref[...] = v.astype(BF16)


def _memkv(mem2d, g, wk, wv, tile, n_mem, heads):
    depth, d, _ = wk.shape
    rows = mem2d.shape[0]
    row_spec = pl.BlockSpec((None, tile, d), lambda l, i: (l, i, 0))
    out5_spec = pl.BlockSpec((None, tile // n_mem, n_mem, heads, d // heads), lambda l, i: (l, i, 0, 0, 0))
    out5_shape = jax.ShapeDtypeStruct((depth, rows // n_mem, n_mem, heads, d // heads), F32)
    w_spec = pl.BlockSpec((None, d, d), lambda l, i: (l, 0, 0))
    return pl.pallas_call(
        _memkv_kernel,
        grid=(depth, rows // tile),
        in_specs=[
            pl.BlockSpec((tile, d), lambda l, i: (i, 0)),
            pl.BlockSpec((None, 1, d), lambda l, i: (l, 0, 0)),
            w_spec, w_spec,
        ],
        out_specs=[out5_spec, out5_spec, row_spec, row_spec],
        out_shape=[
            out5_shape,
            out5_shape,
            jax.ShapeDtypeStruct((depth, rows, d), BF16),
            jax.ShapeDtypeStruct((depth, rows, d), BF16),
        ],
        compiler_params=pltpu.CompilerParams(
            dimension_semantics=("arbitrary", "arbitrary"), vmem_limit_bytes=VMEM_LIMIT),
        name="memkv",
    )(mem2d, g, wk, wv)


MIX_PARAMS = ("g_mix", "w_in", "wa", "wx", "vec")
VEC_ROWS = (("w4", RG_CONV_W), ("b4", None), ("ba", None), ("bx", None), ("lam", None),
            ("w31", CF_CONV_W), ("b31", None), ("ln_g", None), ("ln_b", None))


def _unpack_vec(vec_ref):
    views, row = {}, 0
    for name, count in VEC_ROWS:
        views[name] = vec_ref.at[row] if count is None else vec_ref.at[row:row + count]
        row += count or 1
    return views


def _gate_products(lhs, wa_ref, wx_ref, p_s, q_s):
    n_tiles, width = wa_ref.shape[0], wa_ref.shape[1]
    for j in range(n_tiles):
        cols = slice(j * width, (j + 1) * width)
        p_s[:, cols] = _dot(lhs[:, cols], wa_ref[j])
        q_s[:, cols] = _dot(lhs[:, cols], wx_ref[j])
BLOCK_PARAMS = ("w_out", "g_attn", "w_q", "w_o", "g_ffn", "w_gate", "w_up", "w_down")


def _group_rows(gi, offset=0):
    return pl.ds(gi * SUBLANES + offset, SUBLANES)


def _mix_stage(x_ref, prm, xr_hist, c_hist, hcar, gr_s, xc_s, p_s, q_s, yc_s, xn_s, cat_s, *, tile, c_w):
    win_ref = prm["w_in"]
    n_groups = tile // SUBLANES
    sub8 = lax.broadcasted_iota(jnp.int32, (SUBLANES, c_w), 0)

    yield 2 * _mxu_cost(tile, win_ref.shape[0], c_w), 2 * tile
    xn_s[...] = _rms(x_ref[...], prm["g_mix"][...]).astype(BF16)
    c_hist[pl.ds(CF_HIST, tile), :] = (
        _dot(xn_s[...], win_ref[:, 2 * c_w:3 * c_w]) * _sigmoid(_dot(xn_s[...], win_ref[:, 3 * c_w:4 * c_w])))
    yield 2 * _mxu_cost(tile, win_ref.shape[0], c_w), tile // 4
    xr_hist[pl.ds(RG_HIST, tile), :] = _dot(xn_s[...], win_ref[:, 0:c_w])
    gr_s[...] = _dot(xn_s[...], win_ref[:, c_w:2 * c_w])

    first_tap_off = CF_HIST - (CF_CONV_W - 1)
    n_hist_groups = CF_HIST // SUBLANES + 1
    w31_ref = prm["w31"]
    for lane0 in range(0, c_w, CONV_LANES):
        lanes = pl.ds(lane0, CONV_LANES)
        sub_l = lax.broadcasted_iota(jnp.int32, (SUBLANES, CONV_LANES), 0)
        zero = jnp.zeros((SUBLANES, CONV_LANES), F32)
        p0_prev, rolled_prev = zero, (zero,) * (SUBLANES - 1)
        for si in range(n_groups + 1):
            if si % CONV_GROUPS_PER_SEGMENT == 0:
                yield 0, 26 * (CONV_LANES // 128) * min(CONV_GROUPS_PER_SEGMENT, n_groups + 1 - si)
            hist = [c_hist[_group_rows(si, SUBLANES * q), lanes] for q in range(n_hist_groups)]
            parts = []
            for res in range(SUBLANES):
                part = None
                for q in range(n_hist_groups):
                    k = SUBLANES * q + res - first_tap_off
                    if 0 <= k < CF_CONV_W:
                        term = hist[q] * w31_ref[k, :, lanes]
                        part = term if part is None else part + term
                parts.append(part)
            rolled = tuple(pltpu.roll(parts[res], SUBLANES - res, axis=0) for res in range(1, SUBLANES))
            y = p0_prev
            for res in range(1, SUBLANES):
                y = y + jnp.where(sub_l < SUBLANES - res, rolled_prev[res - 1], rolled[res - 1])
            yc_s[_group_rows(si), lanes] = y
            p0_prev, rolled_prev = parts[0], rolled

    yield 0, 20 * n_groups
    w4_ref = prm["w4"]
    x_hist = xr_hist[pl.ds(0, RG_HIST), :]
    prev_rolled = [pltpu.roll(x_hist, dly, axis=0) for dly in range(1, RG_CONV_W)]
    for gi in range(n_groups):
        xg = xr_hist[_group_rows(gi, RG_HIST), :]
        acc = xg * w4_ref[RG_CONV_W - 1] + prm["b4"][...]
        rolled = []
        for dly in range(1, RG_CONV_W):
            rolled.append(pltpu.roll(xg, dly, axis=0))
            acc = acc + jnp.where(sub8 >= dly, rolled[-1], prev_rolled[dly - 1]) * w4_ref[RG_CONV_W - 1 - dly]
        xc_s[_group_rows(gi), :] = acc
        prev_rolled = rolled

    n_gate_tiles, gate_width = prm["wa"].shape[0], prm["wa"].shape[1]
    yield 2 * n_gate_tiles * _mxu_cost(tile, gate_width, gate_width), tile // 4
    xcb_s = xn_s.at[:, 0:c_w]
    xcb_s[...] = xc_s[...].astype(BF16)
    _gate_products(xcb_s, prm["wa"], prm["wx"], p_s, q_s)

    nsp = -LRU_C * _softplus(-prm["lam"][...])
    n_grp = CHUNK // SUBLANES
    grp = (n_grp, SUBLANES, c_w)
    sub = lax.broadcasted_iota(jnp.int32, grp, 1)
    for ci in range(tile // CHUNK):
        if ci % 2 == 0:
            yield 0, 240
        rows = pl.ds(ci * CHUNK, CHUNK)
        a, b = _lru_coeffs(p_s[rows, :].reshape(grp), q_s[rows, :].reshape(grp),
                           xc_s[rows, :].reshape(grp), prm["ba"][...], prm["bx"][...], nsp)
        for dly in (1, 2, 4):
            ra = pltpu.roll(a, dly, axis=1)
            rb = pltpu.roll(b, dly, axis=1)
            keep = sub >= dly
            b = a * jnp.where(keep, rb, 0.0) + b
            a = a * jnp.where(keep, ra, 1.0)
        p_s[rows, :] = a.reshape(CHUNK, c_w)
        q_s[rows, :] = b.reshape(CHUNK, c_w)

    yield 0, 21 * n_groups
    hb = hcar[...]
    for gi in range(n_groups):
        rows = _group_rows(gi)
        h = p_s[rows, :] * hb + q_s[rows, :]
        q_s[rows, :] = h
        hb = jnp.broadcast_to(h[SUBLANES - 1:SUBLANES, :], (SUBLANES, c_w))
    hcar[...] = hb

    for ci in range(tile // OUT_CHUNK):
        yield 0, 220
        rows = pl.ds(ci * OUT_CHUNK, OUT_CHUNK)
        y_r = q_s[rows, :] * _gelu_tanh(gr_s[rows, :])
        cat_s[rows, 0:c_w] = y_r.astype(BF16)
        conv = yc_s[pl.ds(ci * OUT_CHUNK + SUBLANES, OUT_CHUNK), :].reshape(
            OUT_CHUNK // SUBLANES, SUBLANES, c_w)
        y_c = _ln_silu(conv + prm["b31"][...], prm["ln_g"][...], prm["ln_b"][...])
        cat_s[rows, c_w:2 * c_w] = y_c.reshape(OUT_CHUNK, c_w).astype(BF16)


def _block_stage(x_ref, cat_s, k_ref, v_ref, prm, gf_ref, xo_ref, xb_s, zb_s, qb_s, o_s, h_s,
                 *, heads, final_norm):
    d = x_ref.shape[1]
    head_dim = d // heads
    rows = x_ref.shape[0]
    yield _mxu_cost(rows, d, d), rows // 2
    xb_s[...] = x_ref[...] + _dot(cat_s[...], prm["w_out"][...])
    yield _mxu_cost(rows, d, d), 2 * rows
    zb_s[...] = _rms(xb_s[...], prm["g_attn"][...]).astype(BF16)
    qb_s[...] = (_dot(zb_s[...], prm["w_q"][...]) * (head_dim ** -0.5)).astype(BF16)
    for h in range(heads):
        yield rows // 2, rows
        sl = slice(h * head_dim, (h + 1) * head_dim)
        (o,) = _attend(qb_s[:, sl], k_ref[:, sl], v_ref[:, sl], 1, head_dim)
        o_s[:, sl] = o.astype(BF16)
    yield _mxu_cost(rows, d, d), rows // 2
    xb_s[...] = xb_s[...] + _dot(o_s[...], prm["w_o"][...])
    yield 0, rows
    zb_s[...] = _rms(xb_s[...], prm["g_ffn"][...]).astype(BF16)
    wg_ref, wu_ref, wd_ref = prm["w_gate"], prm["w_up"], prm["w_down"]
    for lo, hi in _ffn_cols(wg_ref.shape[1]):
        yield 2 * _mxu_cost(rows, d, hi - lo), (hi - lo) // 2
        h_s[:, lo:hi] = (_silu(_dot(zb_s[...], wg_ref[:, lo:hi])) * _dot(zb_s[...], wu_ref[:, lo:hi])
                         ).astype(BF16)
    for lo, hi in _ffn_cols(d):
        yield _mxu_cost(rows, wd_ref.shape[0], hi - lo), rows // 2
        xo_ref[:, lo:hi] = xb_s[:, lo:hi] + _dot(h_s[...], wd_ref[:, lo:hi])
    if final_norm:
        yield 0, rows
        xo_ref[...] = _rms(xo_ref[...], gf_ref[...])


def _interleave(*stages):
    pending = {stage: next(stage) for stage in stages}
    clock = {stage: i for i, stage in enumerate(stages)}
    while pending:
        stage = min(pending, key=clock.get)
        clock[stage] += max(pending.pop(stage))
        try:
            pending[stage] = next(stage)
        except StopIteration:
            pass


def _layer_prompt_kernel(*refs, tile, tiles_per_seq, c_w, heads, final_norm, pad_t):
    n_mix, n_blk = len(MIX_PARAMS), len(BLOCK_PARAMS)
    xa_ref, xb_ref, k_ref, v_ref, sq_ref, sk_ref, sv_ref = refs[:7]
    refs = refs[7:]
    mix_prm = dict(zip(MIX_PARAMS, refs[:n_mix]))
    mix_prm.update(_unpack_vec(mix_prm["vec"]))
    blk_prm = dict(zip(BLOCK_PARAMS, refs[n_mix:n_mix + n_blk]))
    gf_ref = refs[n_mix + n_blk]
    xo_ref, h_ref, rg_ref, cf_ref, so_ref = refs[1 + n_mix + n_blk:6 + n_mix + n_blk]
    (xr_hist, c_hist, hcar, gr_s, xc_s, p_s, q_s, yc_s, xn_s, cat_s,
     xb_s, zb_s, qb_s, o_s, h_s) = refs[6 + n_mix + n_blk:]

    g = pl.program_id(0)
    pos = g % tiles_per_seq

    @pl.when(g == 0)
    def _():
        cat_s[...] = jnp.zeros(cat_s.shape, BF16)
        c_hist[pl.ds(CF_HIST + tile, SUBLANES), :] = jnp.zeros((SUBLANES, c_w), F32)

    @pl.when(pos == 0)
    def _():
        xr_hist[pl.ds(0, RG_HIST), :] = jnp.zeros((RG_HIST, c_w), F32)
        c_hist[pl.ds(0, CF_HIST), :] = jnp.zeros((CF_HIST, c_w), F32)
        hcar[...] = jnp.zeros((SUBLANES, c_w), F32)

    _interleave(
        _block_stage(xb_ref, cat_s, k_ref, v_ref, blk_prm, gf_ref, xo_ref, xb_s, zb_s, qb_s, o_s, h_s,
                     heads=heads, final_norm=final_norm),
        _mix_stage(xa_ref, mix_prm, xr_hist, c_hist, hcar, gr_s, xc_s, p_s, q_s, yc_s, xn_s, cat_s,
                   tile=tile, c_w=c_w),
        _sample_attn_stage(sq_ref, sk_ref, sv_ref, so_ref, heads=heads, pad_t=pad_t))

    @pl.when(pos == tiles_per_seq - 1)
    def _():
        h_ref[...] = hcar[0:1, :]
        rg_ref[...] = xr_hist[pl.ds(RG_HIST + tile - (RG_CONV_W - 1), RG_CONV_W - 1), :]
        cf_ref[...] = c_hist[pl.ds(CF_HIST + tile - (CF_CONV_W - 1), CF_CONV_W - 1), :]

    xr_hist[pl.ds(0, RG_HIST), :] = xr_hist[pl.ds(tile, RG_HIST), :]
    c_hist[pl.ds(0, CF_HIST), :] = c_hist[pl.ds(tile, CF_HIST), :]


def _layer_prompt(x2d, kb, vb, sq_pad, cache_k, cache_v, lw, gf, layer,
                  *, batch, seq, n_mem, heads, final_norm, pad_t):
    rows, d = x2d.shape
    tile = PROMPT_TILE
    c_w = lw["vec"].shape[-1]
    d_ff = lw["w_gate"].shape[-1]
    tps = seq // tile
    n_tiles = batch * tps
    dec_batch = cache_k.shape[1]
    nb = dec_batch // n_tiles
    assert nb * n_tiles == dec_batch and sq_pad.shape[0] == dec_batch * pad_t

    def mix_tile(g):
        return jnp.minimum(g, n_tiles - 1)

    def blk_tile(g):
        return jnp.maximum(g - 1, 0)

    kv_spec = pl.BlockSpec((None, n_mem, d), lambda g: (layer, blk_tile(g) // tps, 0))
    sq_spec = pl.BlockSpec((nb * pad_t, d), lambda g: (mix_tile(g), 0))
    cache_spec = pl.BlockSpec((None, nb, n_mem, heads, d // heads), lambda g: (layer, mix_tile(g), 0, 0, 0))
    params = [lw[name] for name in MIX_PARAMS + BLOCK_PARAMS]
    kern = functools.partial(_layer_prompt_kernel, tile=tile, tiles_per_seq=tps, c_w=c_w,
                             heads=heads, final_norm=final_norm, pad_t=pad_t)
    return pl.pallas_call(
        kern,
        grid=(n_tiles + 1,),
        in_specs=[
            pl.BlockSpec((tile, d), lambda g: (mix_tile(g), 0)),
            pl.BlockSpec((tile, d), lambda g: (blk_tile(g), 0)),
            kv_spec, kv_spec,
            sq_spec, cache_spec, cache_spec,
            *[_layer_spec(p, layer) for p in params],
            _full(gf.shape),
        ],
        out_specs=[
            pl.BlockSpec((tile, d), lambda g: (blk_tile(g), 0)),
            pl.BlockSpec((None, 1, c_w), lambda g: (mix_tile(g) // tps, 0, 0)),
            pl.BlockSpec((None, RG_CONV_W - 1, c_w), lambda g: (mix_tile(g) // tps, 0, 0)),
            pl.BlockSpec((None, CF_CONV_W - 1, c_w), lambda g: (mix_tile(g) // tps, 0, 0)),
            sq_spec,
        ],
        out_shape=[
            jax.ShapeDtypeStruct((rows, d), F32),
            jax.ShapeDtypeStruct((batch, 1, c_w), F32),
            jax.ShapeDtypeStruct((batch, RG_CONV_W - 1, c_w), F32),
            jax.ShapeDtypeStruct((batch, CF_CONV_W - 1, c_w), F32),
            jax.ShapeDtypeStruct(sq_pad.shape, F32),
        ],
        scratch_shapes=[
            pltpu.VMEM((RG_HIST + tile, c_w), F32),
            pltpu.VMEM((CF_HIST + tile + SUBLANES, c_w), F32),
            pltpu.VMEM((SUBLANES, c_w), F32),
            pltpu.VMEM((tile, c_w), F32),
            pltpu.VMEM((tile, c_w), F32),
            pltpu.VMEM((tile, c_w), F32),
            pltpu.VMEM((tile, c_w), F32),
            pltpu.VMEM((tile + SUBLANES, c_w), F32),
            pltpu.VMEM((tile, d), BF16),
            pltpu.VMEM((tile, 2 * c_w), BF16),
            pltpu.VMEM((tile, d), F32),
            pltpu.VMEM((tile, d), BF16),
            pltpu.VMEM((tile, d), BF16),
            pltpu.VMEM((tile, d), BF16),
            pltpu.VMEM((tile, d_ff), BF16),
        ],
        compiler_params=pltpu.CompilerParams(
            dimension_semantics=("arbitrary",), vmem_limit_bytes=VMEM_LIMIT),
        name="layer_prompt",
    )(x2d, x2d, kb, vb, sq_pad, cache_k, cache_v, *params, gf)


def _mix_sample_kernel(x_ref, h0_ref, rb_ref, cb_ref, g_ref, win_ref, wa_ref, wx_ref, vec_ref, wout_ref,
                       ga_ref, wq_ref,
                       xo_ref, h_ref, xr_ref, c_ref, qo_ref,
                       gr_s, xc_s, p_s, q_s, cat_s, *, steps, batch, d_rnn, heads):
    c_w = d_rnn
    vec = _unpack_vec(vec_ref)
    w4_ref, b4_ref, ba_ref, bx_ref, lam_ref = vec["w4"], vec["b4"], vec["ba"], vec["bx"], vec["lam"]
    w31_ref, b31_ref, lng_ref, lnb_ref = vec["w31"], vec["b31"], vec["ln_g"], vec["ln_b"]
    xn = _rms(x_ref[...], g_ref[...]).astype(BF16)
    xr_ref[...] = _dot(xn, win_ref[:, 0:c_w])
    gr_s[...] = _dot(xn, win_ref[:, c_w:2 * c_w])
    c_ref[...] = (
        _dot(xn, win_ref[:, 2 * c_w:3 * c_w]) * _sigmoid(_dot(xn, win_ref[:, 3 * c_w:4 * c_w])))

    def rows_of(t, s):
        return pl.ds(pl.multiple_of(t * batch + s, SUBLANES), SUBLANES)

    def conv4_blk(bi, carry):
        s = bi * SUBLANES
        for t in range(steps):
            acc = b4_ref[...]
            for k in range(RG_CONV_W):
                jj = t + k
                if jj < RG_CONV_W - 1:
                    src = rb_ref[jj, pl.ds(pl.multiple_of(s, SUBLANES), SUBLANES), :]
                else:
                    src = xr_ref[rows_of(jj - (RG_CONV_W - 1), s), :]
                acc = acc + src * w4_ref[k]
            xc_s[rows_of(t, s), :] = acc
        return carry

    lax.fori_loop(0, batch // SUBLANES, conv4_blk, 0)

    _gate_products(xc_s[...].astype(BF16), wa_ref, wx_ref, p_s, q_s)
    nsp = -LRU_C * _softplus(-lam_ref[...])

    def scan_blk(bi, carry):
        s = bi * SUBLANES
        b_rows = pl.ds(pl.multiple_of(s, SUBLANES), SUBLANES)
        h = h0_ref[b_rows, :]
        for t in range(steps):
            rows = rows_of(t, s)
            a, b = _lru_coeffs(p_s[rows, :], q_s[rows, :], xc_s[rows, :], ba_ref[...], bx_ref[...], nsp)
            h = a * h + b
            cat_s[rows, 0:c_w] = h * _gelu_tanh(gr_s[rows, :])
        h_ref[b_rows, :] = h
        return carry

    lax.fori_loop(0, batch // SUBLANES, scan_blk, 0)

    def conv31_blk(bi, carry):
        s = bi * SUBLANES
        for t in range(steps):
            acc = b31_ref[...]
            for k in range(CF_CONV_W):
                jj = t + k
                if jj < CF_CONV_W - 1:
                    src = cb_ref[jj, pl.ds(pl.multiple_of(s, SUBLANES), SUBLANES), :]
                else:
                    src = c_ref[rows_of(jj - (CF_CONV_W - 1), s), :]
                acc = acc + src * w31_ref[k]
            cat_s[rows_of(t, s), c_w:2 * c_w] = _ln_silu(acc, lng_ref[...], lnb_ref[...])
        return carry

    lax.fori_loop(0, batch // SUBLANES, conv31_blk, 0)

    xo = x_ref[...] + _dot(cat_s[...].astype(BF16), wout_ref[...])
    xo_ref[...] = xo
    head_dim = xo.shape[1] // heads
    qo_ref[...] = _dot(_rms(xo, ga_ref[...]).astype(BF16), wq_ref[...]) * (head_dim ** -0.5)


def _mix_sample(x_tm, h0, rb_tm, cb_tm, lw, layer, *, steps, batch, heads):
    rows, d = x_tm.shape
    c_w = lw["vec"].shape[-1]
    kern = functools.partial(_mix_sample_kernel, steps=steps, batch=batch, d_rnn=c_w, heads=heads)
    params = [lw[name] for name in MIX_PARAMS + ("w_out", "g_attn", "w_q")]
    return pl.pallas_call(
        kern,
        grid=(1,),
        in_specs=[_full(x_tm.shape), _layer_spec(h0, layer), _layer_spec(rb_tm, layer),
                  _layer_spec(cb_tm, layer),
                  *[_layer_spec(p, layer) for p in params]],
        out_specs=[_full((rows, d)), _full((batch, c_w)), _full((rows, c_w)), _full((rows, c_w)),
                   _full((rows, d))],
        out_shape=[
            jax.ShapeDtypeStruct((rows, d), F32),
            jax.ShapeDtypeStruct((batch, c_w), F32),
            jax.ShapeDtypeStruct((rows, c_w), F32),
            jax.ShapeDtypeStruct((rows, c_w), F32),
            jax.ShapeDtypeStruct((rows, d), F32),
        ],
        scratch_shapes=[
            pltpu.VMEM((rows, c_w), F32),
            pltpu.VMEM((rows, c_w), F32),
            pltpu.VMEM((rows, c_w), F32),
            pltpu.VMEM((rows, c_w), F32),
            pltpu.VMEM((rows, 2 * c_w), F32),
        ],
        compiler_params=pltpu.CompilerParams(
            dimension_semantics=("arbitrary",), vmem_limit_bytes=VMEM_LIMIT),
        name="mix_sample",
    )(x_tm, h0, rb_tm, cb_tm, *params)


def _sample_attn_stage(q_ref, k_ref, v_ref, o_ref, *, heads, pad_t):
    nb, n_mem, _, head_dim = k_ref.shape
    n_kv = n_mem * heads
    col_head = lax.broadcasted_iota(jnp.int32, (heads * pad_t, n_kv), 1) % heads
    row_head = lax.broadcasted_iota(jnp.int32, (heads * pad_t, n_kv), 0) // pad_t
    own_head = col_head == row_head
    for bb in range(nb):
        yield 2 * _mxu_cost(heads * pad_t, head_dim, n_kv) + n_kv // 2, n_kv // 2
        rows = pl.ds(bb * pad_t, pad_t)
        q = q_ref[rows, :]
        qs = jnp.concatenate([q[:, h * head_dim:(h + 1) * head_dim] for h in range(heads)], axis=0)
        k2 = k_ref[bb].reshape(n_kv, head_dim).astype(BF16)
        v2 = v_ref[bb].reshape(n_kv, head_dim).astype(BF16)
        s = lax.dot_general(qs.astype(BF16), k2, (((1,), (1,)), ((), ())), preferred_element_type=F32)
        s = jnp.where(own_head, s, -1e30)
        p = jnp.exp(s - jnp.max(s, axis=-1, keepdims=True))
        inv = 1.0 / jnp.sum(p, axis=-1, keepdims=True)
        o = _dot(p.astype(BF16), v2) * inv
        for h in range(heads):
            o_ref[rows, h * head_dim:(h + 1) * head_dim] = o[h * pad_t:(h + 1) * pad_t, :]


def _ffn_kernel(x_ref, o_ref, wo_ref, g_ref, wg_ref, wu_ref, wd_ref, gf_ref, xo_ref, h_s, *, final_norm):
    x = x_ref[...] + _dot(o_ref[...].astype(BF16), wo_ref[...])
    y = _swiglu(x, g_ref[...], wg_ref, wu_ref, wd_ref, h_s)
    if final_norm:
        y = _rms(y, gf_ref[...])
    xo_ref[...] = y


def _ffn(x2d, o2d, lw, gf, layer, *, tile, final_norm):
    rows, d = x2d.shape
    d_ff = lw["w_gate"].shape[-1]
    row_spec = pl.BlockSpec((tile, d), lambda i: (i, 0))
    kern = functools.partial(_ffn_kernel, final_norm=final_norm)
    return pl.pallas_call(
        kern,
        grid=(rows // tile,),
        in_specs=[row_spec, row_spec, _layer_spec(lw["w_o"], layer), _layer_spec(lw["g_ffn"], layer),
                  _layer_spec(lw["w_gate"], layer), _layer_spec(lw["w_up"], layer),
                  _layer_spec(lw["w_down"], layer), _full(gf.shape)],
        out_specs=row_spec,
        out_shape=jax.ShapeDtypeStruct((rows, d), F32),
        scratch_shapes=[pltpu.VMEM((tile, d_ff), BF16)],
        compiler_params=pltpu.CompilerParams(
            dimension_semantics=("arbitrary",), vmem_limit_bytes=VMEM_LIMIT),
        name="ffn",
    )(x2d, o2d, lw["w_o"], lw["g_ffn"], lw["w_gate"], lw["w_up"], lw["w_down"], gf)


def _rows8(v):
    return jnp.broadcast_to(v[..., None, :], v.shape[:-1] + (SUBLANES, v.shape[-1]))


def _block_diag_tiles(w, width):
    l, h, i, _ = w.shape
    per_tile = width // i
    eye = jnp.eye(per_tile, dtype=w.dtype)
    tiles = jnp.einsum("lthij,hg->lthigj", w.reshape(l, h // per_tile, per_tile, i, i), eye)
    return tiles.reshape(l, h // per_tile, width, width)


def kernel(x_prompt, x_sample, state_rglru_h, state_rglru_conv, state_conf_conv, cache_mem_k, cache_mem_v, mem_prompt, norm_mix_g, w_in, rg_conv_w, rg_conv_b, rg_wa, rg_ba, rg_wx, rg_bx, rg_lambda, cf_conv_w, cf_conv_b, cf_ln_g, cf_ln_b, w_out, norm_attn_g, norm_mem_g, w_q, w_k, w_v, w_o, norm_ffn_g, w_gate, w_up, w_down, norm_final_g):
    batch, seq, d = x_prompt.shape
    dec_batch, dec_seq, _ = x_sample.shape
    depth = w_in.shape[0]
    n_mem = mem_prompt.shape[1]
    heads = cache_mem_k.shape[3]
    pad_t = SUBLANES

    lw = {
        "g_mix": norm_mix_g[:, None, :],
        "w_in": w_in.astype(BF16),
        "wa": _block_diag_tiles(rg_wa, MXU_DIM).astype(BF16),
        "wx": _block_diag_tiles(rg_wx, MXU_DIM).astype(BF16),
        "vec": _rows8(jnp.concatenate(
            [v if v.ndim == 3 else v[:, None, :] for v in
             (rg_conv_w, rg_conv_b, rg_ba, rg_bx, rg_lambda, cf_conv_w, cf_conv_b, cf_ln_g, cf_ln_b)],
            axis=1)),
        "w_out": w_out.astype(BF16),
        "g_attn": norm_attn_g[:, None, :],
        "w_q": w_q.astype(BF16), "w_o": w_o.astype(BF16),
        "g_ffn": norm_ffn_g[:, None, :],
        "w_gate": w_gate.astype(BF16), "w_up": w_up.astype(BF16), "w_down": w_down.astype(BF16),
    }
    gf = norm_final_g[None, :]

    mem_k, mem_v, mem_kb, mem_vb = _memkv(
        mem_prompt.reshape(batch * n_mem, d), norm_mem_g[:, None, :],
        w_k, w_v, tile=512, n_mem=n_mem, heads=heads)

    rb_tm = jnp.transpose(state_rglru_conv, (0, 2, 1, 3))
    cb_tm = jnp.transpose(state_conf_conv, (0, 2, 1, 3))

    xp = x_prompt.reshape(batch * seq, d)
    xs_tm = jnp.transpose(x_sample, (1, 0, 2)).reshape(dec_seq * dec_batch, d)
    p_h, p_rg, p_cf, s_h, s_xr, s_c = [], [], [], [], [], []
    for l in range(depth):
        last = l == depth - 1

        xs_tm, hs, xr_tm, c_tm, sq_tm = _mix_sample(xs_tm, state_rglru_h, rb_tm, cb_tm, lw, l,
                                                    steps=dec_seq, batch=dec_batch, heads=heads)
        s_h.append(hs); s_xr.append(xr_tm); s_c.append(c_tm)
        sq_pad = jnp.pad(jnp.transpose(sq_tm.reshape(dec_seq, dec_batch, d), (1, 0, 2)),
                         ((0, 0), (0, pad_t - dec_seq), (0, 0))).reshape(dec_batch * pad_t, d)

        xp, hp, rgp, cfp, so_pad = _layer_prompt(
            xp, mem_kb, mem_vb, sq_pad, cache_mem_k, cache_mem_v, lw, gf, l, batch=batch, seq=seq,
            n_mem=n_mem, heads=heads, final_norm=last, pad_t=pad_t)
        p_h.append(hp[:, 0, :]); p_rg.append(rgp); p_cf.append(cfp)

        so_tm = jnp.transpose(so_pad.reshape(dec_batch, pad_t, d)[:, :dec_seq], (1, 0, 2)).reshape(
            dec_seq * dec_batch, d)
        xs_tm = _ffn(xs_tm, so_tm, lw, gf, l, tile=dec_seq * dec_batch, final_norm=last)

    def new_conv_state(old, fresh_tm, width):
        fresh = jnp.transpose(jnp.stack(fresh_tm).reshape(depth, dec_seq, dec_batch, -1), (0, 2, 1, 3))
        return jnp.concatenate([old, fresh], axis=2)[:, :, -(width - 1):]

    y_sample = jnp.transpose(xs_tm.reshape(dec_seq, dec_batch, d), (1, 0, 2))
    return (xp.reshape(batch, seq, d), y_sample,
            jnp.stack(p_h), jnp.stack(p_rg), jnp.stack(p_cf),
            mem_k, mem_v,
            jnp.stack(s_h),
            new_conv_state(state_rglru_conv, s_xr, RG_CONV_W),
            new_conv_state(state_conf_conv, s_c, CF_CONV_W))
```

```python
import functools
import math

import jax
import jax.numpy as jnp
from jax import lax
from jax.experimental import pallas as pl
from jax.experimental.pallas import tpu as pltpu

F32 = jnp.float32
BF16 = jnp.bfloat16

EPS = 1e-6
LRU_C = 8.0
RG_CONV_W = 4
CF_CONV_W = 31
SUBLANES = 8
MXU_DIM = 256
CF_HIST = 32
RG_HIST = SUBLANES
CHUNK = 16
OUT_CHUNK = 32
CONV_LANES = 256
CONV_GROUPS_PER_SEGMENT = 6
FFN_COLS = 512
PROMPT_TILE = 256
VMEM_LIMIT = 56 * 1024 * 1024


def _dot(a, b):
    return jnp.dot(a, b, preferred_element_type=F32)


def _rms(x, g):
    return x * lax.rsqrt(jnp.mean(x * x, axis=-1, keepdims=True) + EPS) * g


def _gelu_tanh(x):
    c = math.sqrt(2.0 / math.pi)
    return (0.5 * x) * (1.0 + jnp.tanh(x * (c + (c * 0.044715) * (x * x))))


def _sigmoid(x):
    return 0.5 * (jnp.tanh(0.5 * x) + 1.0)


def _silu(x):
    h = 0.5 * x
    return h * (jnp.tanh(h) + 1.0)


def _softplus(z):
    return jnp.maximum(z, 0.0) + jnp.log1p(jnp.exp(-jnp.abs(z)))


def _lru_coeffs(r_pre, i_pre, xc, ba, bx, half_nsp):
    log_a = (jnp.tanh(0.5 * (r_pre + ba)) + 1.0) * half_nsp
    i = _sigmoid(i_pre + bx)
    a = jnp.exp(log_a)
    t = jnp.tanh(log_a)
    mult = jnp.sqrt(-2.0 * t / (1.0 - t))
    return a, mult * (i * xc)


def _ln_silu(y, g, b):
    mu = jnp.mean(y, axis=-1, keepdims=True)
    yc = y - mu
    var = jnp.mean(yc * yc, axis=-1, keepdims=True)
    return _silu(yc * lax.rsqrt(var + EPS) * g + b)


def _query_scale(head_dim):
    return head_dim ** -0.5 * math.log2(math.e)


def _attend(q, k, v, heads, head_dim):
    outs = []
    for h in range(heads):
        sl = slice(h * head_dim, (h + 1) * head_dim)
        s = lax.dot_general(q[:, sl], k[:, sl], (((1,), (1,)), ((), ())), preferred_element_type=F32)
        p = jnp.exp2(s - jnp.max(s, axis=-1, keepdims=True))
        inv = 1.0 / jnp.sum(p, axis=-1, keepdims=True)
        outs.append(_dot(p.astype(BF16), v[:, sl]) * inv)
    return outs


def _ffn_cols(d_ff):
    return tuple((lo, min(lo + FFN_COLS, d_ff)) for lo in range(0, d_ff, FFN_COLS))


def _swiglu(x, g, wg_ref, wu_ref, wd_ref, h_s):
    z = _rms(x, g).astype(BF16)
    for lo, hi in _ffn_cols(wg_ref.shape[1]):
        h_s[:, lo:hi] = (_silu(_dot(z, wg_ref[:, lo:hi])) * _dot(z, wu_ref[:, lo:hi])).astype(BF16)
    return x + _dot(h_s[...], wd_ref[...])


def _mxu_cost(rows, k, n):
    return (rows // 2) * pl.cdiv(k, MXU_DIM) * pl.cdiv(n, MXU_DIM) // 2


def _full(shape):
    return pl.BlockSpec(shape, lambda *_: (0,) * len(shape))


def _layer_spec(arr, layer):
    zeros = (0,) * (arr.ndim - 1)
    return pl.BlockSpec((None,) + arr.shape[1:], lambda *_: (layer,) + zeros, pipeline_mode=pl.Buffered(1))


def _memkv_kernel(mem_ref, g_ref, wk_ref, wv_ref, k_ref, v_ref, kb_ref, vb_ref):
    m = _rms(mem_ref[...], g_ref[...]).astype(BF16)
    k = _dot(m, wk_ref[...].astype(BF16))
    v = _dot(m, wv_ref[...].astype(BF16))
    k_ref[...] = k.reshape(k_ref.shape)
    v_ref[...] = v.reshape(v_ref.shape)
    kb_ref[...] = k.astype(BF16)
    vb_ref[...] = v.astype(BF16)


def _memkv(mem2d, g, wk, wv, tile, n_mem, heads):
    depth, d, _ = wk.shape
    rows = mem2d.shape[0]
    row_spec = pl.BlockSpec((None, tile, d), lambda l, i: (l, i, 0))
    out5_spec = pl.BlockSpec((None, tile // n_mem, n_mem, heads, d // heads), lambda l, i: (l, i, 0, 0, 0))
    out5_shape = jax.ShapeDtypeStruct((depth, rows // n_mem, n_mem, heads, d // heads), F32)
    w_spec = pl.BlockSpec((None, d, d), lambda l, i: (l, 0, 0))
    return pl.pallas_call(
        _memkv_kernel,
        grid=(depth, rows // tile),
        in_specs=[
            pl.BlockSpec((tile, d), lambda l, i: (i, 0)),
            pl.BlockSpec((None, 1, d), lambda l, i: (l, 0, 0)),
            w_spec, w_spec,
        ],
        out_specs=[out5_spec, out5_spec, row_spec, row_spec],
        out_shape=[
            out5_shape,
            out5_shape,
            jax.ShapeDtypeStruct((depth, rows, d), BF16),
            jax.ShapeDtypeStruct((depth, rows, d), BF16),
        ],
        compiler_params=pltpu.CompilerParams(
            dimension_semantics=("arbitrary", "arbitrary"), vmem_limit_bytes=VMEM_LIMIT),
        name="memkv",
    )(mem2d, g, wk, wv)


MIX_PARAMS = ("g_mix", "w_in", "wa", "wx", "vec")
VEC_ROWS = (("w4", RG_CONV_W), ("b4", None), ("ba", None), ("bx", None), ("lam", None),
            ("w31", CF_CONV_W), ("b31", None), ("ln_g", None), ("ln_b", None))


def _unpack_vec(vec_ref):
    views, row = {}, 0
    for name, count in VEC_ROWS:
        views[name] = vec_ref.at[row] if count is None else vec_ref.at[row:row + count]
        row += count or 1
    return views


def _gate_products(lhs, wa_ref, wx_ref, p_s, q_s):
    n_tiles, width = wa_ref.shape[0], wa_ref.shape[1]
    for j in range(n_tiles):
        cols = slice(j * width, (j + 1) * width)
        p_s[:, cols] = _dot(lhs[:, cols], wa_ref[j])
        q_s[:, cols] = _dot(lhs[:, cols], wx_ref[j])
BLOCK_PARAMS = ("w_out", "g_attn", "w_q", "w_o", "g_ffn", "w_gate", "w_up", "w_down")


def _group_rows(gi, offset=0):
    return pl.ds(gi * SUBLANES + offset, SUBLANES)


def _mix_stage(x_ref, prm, xr_hist, c_hist, hcar, gr_s, xc_s, p_s, q_s, yc_s, xn_s, cat_s, *, tile, c_w):
    win_ref = prm["w_in"]
    n_groups = tile // SUBLANES
    sub8 = lax.broadcasted_iota(jnp.int32, (SUBLANES, c_w), 0)

    yield 2 * _mxu_cost(tile, win_ref.shape[0], c_w), 2 * tile
    xn_s[...] = _rms(x_ref[...], prm["g_mix"][...]).astype(BF16)
    c_hist[pl.ds(CF_HIST, tile), :] = (
        _dot(xn_s[...], win_ref[:, 2 * c_w:3 * c_w]) * _sigmoid(_dot(xn_s[...], win_ref[:, 3 * c_w:4 * c_w])))
    yield 2 * _mxu_cost(tile, win_ref.shape[0], c_w), tile // 4
    xr_hist[pl.ds(RG_HIST, tile), :] = _dot(xn_s[...], win_ref[:, 0:c_w])
    gr_s[...] = _dot(xn_s[...], win_ref[:, c_w:2 * c_w])

    first_tap_off = CF_HIST - (CF_CONV_W - 1)
    n_hist_groups = CF_HIST // SUBLANES + 1
    w31_ref = prm["w31"]
    for lane0 in range(0, c_w, CONV_LANES):
        lanes = pl.ds(lane0, CONV_LANES)
        sub_l = lax.broadcasted_iota(jnp.int32, (SUBLANES, CONV_LANES), 0)
        zero = jnp.zeros((SUBLANES, CONV_LANES), F32)
        p0_prev, rolled_prev = zero, (zero,) * (SUBLANES - 1)
        for si in range(n_groups + 1):
            if si % CONV_GROUPS_PER_SEGMENT == 0:
                yield 0, 26 * (CONV_LANES // 128) * min(CONV_GROUPS_PER_SEGMENT, n_groups + 1 - si)
            hist = [c_hist[_group_rows(si, SUBLANES * q), lanes] for q in range(n_hist_groups)]
            parts = []
            for res in range(SUBLANES):
                part = None
                for q in range(n_hist_groups):
                    k = SUBLANES * q + res - first_tap_off
                    if 0 <= k < CF_CONV_W:
                        term = hist[q] * w31_ref[k, :, lanes]
                        part = term if part is None else part + term
                parts.append(part)
            rolled = tuple(pltpu.roll(parts[res], SUBLANES - res, axis=0) for res in range(1, SUBLANES))
            y = p0_prev
            for res in range(1, SUBLANES):
                y = y + jnp.where(sub_l < SUBLANES - res, rolled_prev[res - 1], rolled[res - 1])
            yc_s[_group_rows(si), lanes] = y
            p0_prev, rolled_prev = parts[0], rolled

    yield 0, 20 * n_groups
    w4_ref = prm["w4"]
    x_hist = xr_hist[pl.ds(0, RG_HIST), :]
    prev_rolled = [pltpu.roll(x_hist, dly, axis=0) for dly in range(1, RG_CONV_W)]
    for gi in range(n_groups):
        xg = xr_hist[_group_rows(gi, RG_HIST), :]
        acc = xg * w4_ref[RG_CONV_W - 1] + prm["b4"][...]
        rolled = []
        for dly in range(1, RG_CONV_W):
            rolled.append(pltpu.roll(xg, dly, axis=0))
            acc = acc + jnp.where(sub8 >= dly, rolled[-1], prev_rolled[dly - 1]) * w4_ref[RG_CONV_W - 1 - dly]
        xc_s[_group_rows(gi), :] = acc
        prev_rolled = rolled

    n_gate_tiles, gate_width = prm["wa"].shape[0], prm["wa"].shape[1]
    yield 2 * n_gate_tiles * _mxu_cost(tile, gate_width, gate_width), tile // 4
    xcb_s = xn_s.at[:, 0:c_w]
    xcb_s[...] = xc_s[...].astype(BF16)
    _gate_products(xcb_s, prm["wa"], prm["wx"], p_s, q_s)

    half_nsp = (-0.5 * LRU_C) * _softplus(-prm["lam"][...])
    n_grp = CHUNK // SUBLANES
    grp = (n_grp, SUBLANES, c_w)
    sub = lax.broadcasted_iota(jnp.int32, grp, 1)
    for ci in range(tile // CHUNK):
        if ci % 2 == 0:
            yield 0, 240
        rows = pl.ds(ci * CHUNK, CHUNK)
        a, b = _lru_coeffs(p_s[rows, :].reshape(grp), q_s[rows, :].reshape(grp),
                           xc_s[rows, :].reshape(grp), prm["ba"][...], prm["bx"][...], half_nsp)
        for dly in (1, 2, 4):
            ra = pltpu.roll(a, dly, axis=1)
            rb = pltpu.roll(b, dly, axis=1)
            keep = sub >= dly
            b = a * jnp.where(keep, rb, 0.0) + b
            a = a * jnp.where(keep, ra, 1.0)
        p_s[rows, :] = a.reshape(CHUNK, c_w)
        q_s[rows, :] = b.reshape(CHUNK, c_w)

    yield 0, 21 * n_groups
    hb = hcar[...]
    for gi in range(n_groups):
        rows = _group_rows(gi)
        h = p_s[rows, :] * hb + q_s[rows, :]
        q_s[rows, :] = h
        hb = jnp.broadcast_to(h[SUBLANES - 1:SUBLANES, :], (SUBLANES, c_w))
    hcar[...] = hb

    for ci in range(tile // OUT_CHUNK):
        yield 0, 220
        rows = pl.ds(ci * OUT_CHUNK, OUT_CHUNK)
        y_r = q_s[rows, :] * _gelu_tanh(gr_s[rows, :])
        cat_s[rows, 0:c_w] = y_r.astype(BF16)
        conv = yc_s[pl.ds(ci * OUT_CHUNK + SUBLANES, OUT_CHUNK), :].reshape(
            OUT_CHUNK // SUBLANES, SUBLANES, c_w)
        y_c = _ln_silu(conv + prm["b31"][...], prm["ln_g"][...], prm["ln_b"][...])
        cat_s[rows, c_w:2 * c_w] = y_c.reshape(OUT_CHUNK, c_w).astype(BF16)


def _block_stage(x_ref, cat_s, k_ref, v_ref, prm, gf_ref, xo_ref, xb_s, zb_s, qb_s, o_s, h_s,
                 *, heads, final_norm):
    d = x_ref.shape[1]
    head_dim = d // heads
    rows = x_ref.shape[0]
    yield _mxu_cost(rows, d, d), rows // 2
    xb_s[...] = x_ref[...] + _dot(cat_s[...], prm["w_out"][...])
    yield _mxu_cost(rows, d, d), 2 * rows
    zb_s[...] = _rms(xb_s[...], prm["g_attn"][...]).astype(BF16)
    qb_s[...] = (_dot(zb_s[...], prm["w_q"][...]) * _query_scale(head_dim)).astype(BF16)
    for h in range(heads):
        yield rows // 2, rows
        sl = slice(h * head_dim, (h + 1) * head_dim)
        (o,) = _attend(qb_s[:, sl], k_ref[:, sl], v_ref[:, sl], 1, head_dim)
        o_s[:, sl] = o.astype(BF16)
    yield _mxu_cost(rows, d, d), rows // 2
    xb_s[...] = xb_s[...] + _dot(o_s[...], prm["w_o"][...])
    yield 0, rows
    zb_s[...] = _rms(xb_s[...], prm["g_ffn"][...]).astype(BF16)
    wg_ref, wu_ref, wd_ref = prm["w_gate"], prm["w_up"], prm["w_down"]
    for lo, hi in _ffn_cols(wg_ref.shape[1]):
        yield 2 * _mxu_cost(rows, d, hi - lo), (hi - lo) // 2
        h_s[:, lo:hi] = (_silu(_dot(zb_s[...], wg_ref[:, lo:hi])) * _dot(zb_s[...], wu_ref[:, lo:hi])
                         ).astype(BF16)
    for lo, hi in _ffn_cols(d):
        yield _mxu_cost(rows, wd_ref.shape[0], hi - lo), rows // 2
        xo_ref[:, lo:hi] = xb_s[:, lo:hi] + _dot(h_s[...], wd_ref[:, lo:hi])
    if final_norm:
        yield 0, rows
        xo_ref[...] = _rms(xo_ref[...], gf_ref[...])


def _interleave(*stages):
    pending = {stage: next(stage) for stage in stages}
    clock = {stage: i for i, stage in enumerate(stages)}
    while pending:
        stage = min(pending, key=clock.get)
        clock[stage] += max(pending.pop(stage))
        try:
            pending[stage] = next(stage)
        except StopIteration:
            pass


def _layer_prompt_kernel(*refs, tile, tiles_per_seq, c_w, heads, final_norm, pad_t):
    n_mix, n_blk = len(MIX_PARAMS), len(BLOCK_PARAMS)
    xa_ref, xb_ref, k_ref, v_ref, sq_ref, sk_ref, sv_ref = refs[:7]
    refs = refs[7:]
    mix_prm = dict(zip(MIX_PARAMS, refs[:n_mix]))
    mix_prm.update(_unpack_vec(mix_prm["vec"]))
    blk_prm = dict(zip(BLOCK_PARAMS, refs[n_mix:n_mix + n_blk]))
    gf_ref = refs[n_mix + n_blk]
    xo_ref, h_ref, rg_ref, cf_ref, so_ref = refs[1 + n_mix + n_blk:6 + n_mix + n_blk]
    (xr_hist, c_hist, hcar, gr_s, xc_s, p_s, q_s, yc_s, xn_s, cat_s,
     xb_s, zb_s, qb_s, o_s, h_s) = refs[6 + n_mix + n_blk:]

    g = pl.program_id(0)
    pos = g % tiles_per_seq

    @pl.when(g == 0)
    def _():
        cat_s[...] = jnp.zeros(cat_s.shape, BF16)
        c_hist[pl.ds(CF_HIST + tile, SUBLANES), :] = jnp.zeros((SUBLANES, c_w), F32)

    @pl.when(pos == 0)
    def _():
        xr_hist[pl.ds(0, RG_HIST), :] = jnp.zeros((RG_HIST, c_w), F32)
        c_hist[pl.ds(0, CF_HIST), :] = jnp.zeros((CF_HIST, c_w), F32)
        hcar[...] = jnp.zeros((SUBLANES, c_w), F32)

    _interleave(
        _block_stage(xb_ref, cat_s, k_ref, v_ref, blk_prm, gf_ref, xo_ref, xb_s, zb_s, qb_s, o_s, h_s,
                     heads=heads, final_norm=final_norm),
        _mix_stage(xa_ref, mix_prm, xr_hist, c_hist, hcar, gr_s, xc_s, p_s, q_s, yc_s, xn_s, cat_s,
                   tile=tile, c_w=c_w),
        _sample_attn_stage(sq_ref, sk_ref, sv_ref, so_ref, heads=heads, pad_t=pad_t))

    @pl.when(pos == tiles_per_seq - 1)
    def _():
        h_ref[...] = hcar[0:1, :]
        rg_ref[...] = xr_hist[pl.ds(RG_HIST + tile - (RG_CONV_W - 1), RG_CONV_W - 1), :]
        cf_ref[...] = c_hist[pl.ds(CF_HIST + tile - (CF_CONV_W - 1), CF_CONV_W - 1), :]

    xr_hist[pl.ds(0, RG_HIST), :] = xr_hist[pl.ds(tile, RG_HIST), :]
    c_hist[pl.ds(0, CF_HIST), :] = c_hist[pl.ds(tile, CF_HIST), :]


def _layer_prompt(x2d, kb, vb, sq_pad, cache_k, cache_v, lw, gf, layer,
                  *, batch, seq, n_mem, heads, final_norm, pad_t):
    rows, d = x2d.shape
    tile = PROMPT_TILE
    c_w = lw["vec"].shape[-1]
    d_ff = lw["w_gate"].shape[-1]
    tps = seq // tile
    n_tiles = batch * tps
    dec_batch = cache_k.shape[1]
    nb = dec_batch // n_tiles
    assert nb * n_tiles == dec_batch and sq_pad.shape[0] == dec_batch * pad_t

    def mix_tile(g):
        return jnp.minimum(g, n_tiles - 1)

    def blk_tile(g):
        return jnp.maximum(g - 1, 0)

    kv_spec = pl.BlockSpec((None, n_mem, d), lambda g: (layer, blk_tile(g) // tps, 0))
    sq_spec = pl.BlockSpec((nb * pad_t, d), lambda g: (mix_tile(g), 0))
    cache_spec = pl.BlockSpec((None, nb, n_mem, heads, d // heads), lambda g: (layer, mix_tile(g), 0, 0, 0))
    params = [lw[name] for name in MIX_PARAMS + BLOCK_PARAMS]
    kern = functools.partial(_layer_prompt_kernel, tile=tile, tiles_per_seq=tps, c_w=c_w,
                             heads=heads, final_norm=final_norm, pad_t=pad_t)
    return pl.pallas_call(
        kern,
        grid=(n_tiles + 1,),
        in_specs=[
            pl.BlockSpec((tile, d), lambda g: (mix_tile(g), 0)),
            pl.BlockSpec((tile, d), lambda g: (blk_tile(g), 0)),
            kv_spec, kv_spec,
            sq_spec, cache_spec, cache_spec,
            *[_layer_spec(p, layer) for p in params],
            _full(gf.shape),
        ],
        out_specs=[
            pl.BlockSpec((tile, d), lambda g: (blk_tile(g), 0)),
            pl.BlockSpec((None, 1, c_w), lambda g: (mix_tile(g) // tps, 0, 0)),
            pl.BlockSpec((None, RG_CONV_W - 1, c_w), lambda g: (mix_tile(g) // tps, 0, 0)),
            pl.BlockSpec((None, CF_CONV_W - 1, c_w), lambda g: (mix_tile(g) // tps, 0, 0)),
            sq_spec,
        ],
        out_shape=[
            jax.ShapeDtypeStruct((rows, d), F32),
            jax.ShapeDtypeStruct((batch, 1, c_w), F32),
            jax.ShapeDtypeStruct((batch, RG_CONV_W - 1, c_w), F32),
            jax.ShapeDtypeStruct((batch, CF_CONV_W - 1, c_w), F32),
            jax.ShapeDtypeStruct(sq_pad.shape, F32),
        ],
        scratch_shapes=[
            pltpu.VMEM((RG_HIST + tile, c_w), F32),
            pltpu.VMEM((CF_HIST + tile + SUBLANES, c_w), F32),
            pltpu.VMEM((SUBLANES, c_w), F32),
            pltpu.VMEM((tile, c_w), F32),
            pltpu.VMEM((tile, c_w), F32),
            pltpu.VMEM((tile, c_w), F32),
            pltpu.VMEM((tile, c_w), F32),
            pltpu.VMEM((tile + SUBLANES, c_w), F32),
            pltpu.VMEM((tile, d), BF16),
            pltpu.VMEM((tile, 2 * c_w), BF16),
            pltpu.VMEM((tile, d), F32),
            pltpu.VMEM((tile, d), BF16),
            pltpu.VMEM((tile, d), BF16),
            pltpu.VMEM((tile, d), BF16),
            pltpu.VMEM((tile, d_ff), BF16),
        ],
        compiler_params=pltpu.CompilerParams(
            dimension_semantics=("arbitrary",), vmem_limit_bytes=VMEM_LIMIT),
        name="layer_prompt",
    )(x2d, x2d, kb, vb, sq_pad, cache_k, cache_v, *params, gf)


def _mix_sample_kernel(x_ref, h0_ref, rb_ref, cb_ref, g_ref, win_ref, wa_ref, wx_ref, vec_ref, wout_ref,
                       ga_ref, wq_ref,
                       xo_ref, h_ref, xr_ref, c_ref, qo_ref,
                       gr_s, xc_s, p_s, q_s, cat_s, *, steps, batch, d_rnn, heads):
    c_w = d_rnn
    vec = _unpack_vec(vec_ref)
    w4_ref, b4_ref, ba_ref, bx_ref, lam_ref = vec["w4"], vec["b4"], vec["ba"], vec["bx"], vec["lam"]
    w31_ref, b31_ref, lng_ref, lnb_ref = vec["w31"], vec["b31"], vec["ln_g"], vec["ln_b"]
    xn = _rms(x_ref[...], g_ref[...]).astype(BF16)
    xr_ref[...] = _dot(xn, win_ref[:, 0:c_w])
    gr_s[...] = _dot(xn, win_ref[:, c_w:2 * c_w])
    c_ref[...] = (
        _dot(xn, win_ref[:, 2 * c_w:3 * c_w]) * _sigmoid(_dot(xn, win_ref[:, 3 * c_w:4 * c_w])))

    def rows_of(t, s):
        return pl.ds(pl.multiple_of(t * batch + s, SUBLANES), SUBLANES)

    def conv4_blk(bi, carry):
        s = bi * SUBLANES
        for t in range(steps):
            acc = b4_ref[...]
            for k in range(RG_CONV_W):
                jj = t + k
                if jj < RG_CONV_W - 1:
                    src = rb_ref[jj, pl.ds(pl.multiple_of(s, SUBLANES), SUBLANES), :]
                else:
                    src = xr_ref[rows_of(jj - (RG_CONV_W - 1), s), :]
                acc = acc + src * w4_ref[k]
            xc_s[rows_of(t, s), :] = acc
        return carry

    lax.fori_loop(0, batch // SUBLANES, conv4_blk, 0)

    _gate_products(xc_s[...].astype(BF16), wa_ref, wx_ref, p_s, q_s)
    half_nsp = (-0.5 * LRU_C) * _softplus(-lam_ref[...])

    def scan_blk(bi, carry):
        s = bi * SUBLANES
        b_rows = pl.ds(pl.multiple_of(s, SUBLANES), SUBLANES)
        h = h0_ref[b_rows, :]
        for t in range(steps):
            rows = rows_of(t, s)
            a, b = _lru_coeffs(p_s[rows, :], q_s[rows, :], xc_s[rows, :], ba_ref[...], bx_ref[...], half_nsp)
            h = a * h + b
            cat_s[rows, 0:c_w] = h * _gelu_tanh(gr_s[rows, :])
        h_ref[b_rows, :] = h
        return carry

    lax.fori_loop(0, batch // SUBLANES, scan_blk, 0)

    def conv31_blk(bi, carry):
        s = bi * SUBLANES
        for t in range(steps):
            acc = b31_ref[...]
            for k in range(CF_CONV_W):
                jj = t + k
                if jj < CF_CONV_W - 1:
                    src = cb_ref[jj, pl.ds(pl.multiple_of(s, SUBLANES), SUBLANES), :]
                else:
                    src = c_ref[rows_of(jj - (CF_CONV_W - 1), s), :]
                acc = acc + src * w31_ref[k]
            cat_s[rows_of(t, s), c_w:2 * c_w] = _ln_silu(acc, lng_ref[...], lnb_ref[...])
        return carry

    lax.fori_loop(0, batch // SUBLANES, conv31_blk, 0)

    xo = x_ref[...] + _dot(cat_s[...].astype(BF16), wout_ref[...])
    xo_ref[...] = xo
    head_dim = xo.shape[1] // heads
    qo_ref[...] = _dot(_rms(xo, ga_ref[...]).astype(BF16), wq_ref[...]) * _query_scale(head_dim)


def _mix_sample(x_tm, h0, rb_tm, cb_tm, lw, layer, *, steps, batch, heads):
    rows, d = x_tm.shape
    c_w = lw["vec"].shape[-1]
    kern = functools.partial(_mix_sample_kernel, steps=steps, batch=batch, d_rnn=c_w, heads=heads)
    params = [lw[name] for name in MIX_PARAMS + ("w_out", "g_attn", "w_q")]
    return pl.pallas_call(
        kern,
        grid=(1,),
        in_specs=[_full(x_tm.shape), _layer_spec(h0, layer), _layer_spec(rb_tm, layer),
                  _layer_spec(cb_tm, layer),
                  *[_layer_spec(p, layer) for p in params]],
        out_specs=[_full((rows, d)), _full((batch, c_w)), _full((rows, c_w)), _full((rows, c_w)),
                   _full((rows, d))],
        out_shape=[
            jax.ShapeDtypeStruct((rows, d), F32),
            jax.ShapeDtypeStruct((batch, c_w), F32),
            jax.ShapeDtypeStruct((rows, c_w), F32),
            jax.ShapeDtypeStruct((rows, c_w), F32),
            jax.ShapeDtypeStruct((rows, d), F32),
        ],
        scratch_shapes=[
            pltpu.VMEM((rows, c_w), F32),
            pltpu.VMEM((rows, c_w), F32),
            pltpu.VMEM((rows, c_w), F32),
            pltpu.VMEM((rows, c_w), F32),
            pltpu.VMEM((rows, 2 * c_w), F32),
        ],
        compiler_params=pltpu.CompilerParams(
            dimension_semantics=("arbitrary",), vmem_limit_bytes=VMEM_LIMIT),
        name="mix_sample",
    )(x_tm, h0, rb_tm, cb_tm, *params)


def _sample_attn_stage(q_ref, k_ref, v_ref, o_ref, *, heads, pad_t):
    nb, n_mem, _, head_dim = k_ref.shape
    n_kv = n_mem * heads
    col_head = lax.broadcasted_iota(jnp.int32, (heads * pad_t, n_kv), 1) % heads
    row_head = lax.broadcasted_iota(jnp.int32, (heads * pad_t, n_kv), 0) // pad_t
    own_head = col_head == row_head
    for bb in range(nb):
        yield 2 * _mxu_cost(heads * pad_t, head_dim, n_kv) + n_kv // 2, n_kv // 2
        rows = pl.ds(bb * pad_t, pad_t)
        q = q_ref[rows, :]
        qs = jnp.concatenate([q[:, h * head_dim:(h + 1) * head_dim] for h in range(heads)], axis=0)
        k2 = k_ref[bb].reshape(n_kv, head_dim).astype(BF16)
        v2 = v_ref[bb].reshape(n_kv, head_dim).astype(BF16)
        s = lax.dot_general(qs.astype(BF16), k2, (((1,), (1,)), ((), ())), preferred_element_type=F32)
        s = jnp.where(own_head, s, -1e30)
        p = jnp.exp2(s - jnp.max(s, axis=-1, keepdims=True))
        inv = 1.0 / jnp.sum(p, axis=-1, keepdims=True)
        o = _dot(p.astype(BF16), v2) * inv
        for h in range(heads):
            o_ref[rows, h * head_dim:(h + 1) * head_dim] = o[h * pad_t:(h + 1) * pad_t, :]


def _ffn_kernel(x_ref, o_ref, wo_ref, g_ref, wg_ref, wu_ref, wd_ref, gf_ref, xo_ref, h_s, *, final_norm):
    x = x_ref[...] + _dot(o_ref[...].astype(BF16), wo_ref[...])
    y = _swiglu(x, g_ref[...], wg_ref, wu_ref, wd_ref, h_s)
    if final_norm:
        y = _rms(y, gf_ref[...])
    xo_ref[...] = y


def _ffn(x2d, o2d, lw, gf, layer, *, tile, final_norm):
    rows, d = x2d.shape
    d_ff = lw["w_gate"].shape[-1]
    row_spec = pl.BlockSpec((tile, d), lambda i: (i, 0))
    kern = functools.partial(_ffn_kernel, final_norm=final_norm)
    return pl.pallas_call(
        kern,
        grid=(rows // tile,),
        in_specs=[row_spec, row_spec, _layer_spec(lw["w_o"], layer), _layer_spec(lw["g_ffn"], layer),
                  _layer_spec(lw["w_gate"], layer), _layer_spec(lw["w_up"], layer),
                  _layer_spec(lw["w_down"], layer), _full(gf.shape)],
        out_specs=row_spec,
        out_shape=jax.ShapeDtypeStruct((rows, d), F32),
        scratch_shapes=[pltpu.VMEM((tile, d_ff), BF16)],
        compiler_params=pltpu.CompilerParams(
            dimension_semantics=("arbitrary",), vmem_limit_bytes=VMEM_LIMIT),
        name="ffn",
    )(x2d, o2d, lw["w_o"], lw["g_ffn"], lw["w_gate"], lw["w_up"], lw["w_down"], gf)


def _rows8(v):
    return jnp.broadcast_to(v[..., None, :], v.shape[:-1] + (SUBLANES, v.shape[-1]))


def _block_diag_tiles(w, width):
    l, h, i, _ = w.shape
    per_tile = width // i
    eye = jnp.eye(per_tile, dtype=w.dtype)
    tiles = jnp.einsum("lthij,hg->lthigj", w.reshape(l, h // per_tile, per_tile, i, i), eye)
    return tiles.reshape(l, h // per_tile, width, width)


def kernel(x_prompt, x_sample, state_rglru_h, state_rglru_conv, state_conf_conv, cache_mem_k, cache_mem_v, mem_prompt, norm_mix_g, w_in, rg_conv_w, rg_conv_b, rg_wa, rg_ba, rg_wx, rg_bx, rg_lambda, cf_conv_w, cf_conv_b, cf_ln_g, cf_ln_b, w_out, norm_attn_g, norm_mem_g, w_q, w_k, w_v, w_o, norm_ffn_g, w_gate, w_up, w_down, norm_final_g):
    batch, seq, d = x_prompt.shape
    dec_batch, dec_seq, _ = x_sample.shape
    depth = w_in.shape[0]
    n_mem = mem_prompt.shape[1]
    heads = cache_mem_k.shape[3]
    pad_t = SUBLANES

    lw = {
        "g_mix": norm_mix_g[:, None, :],
        "w_in": w_in.astype(BF16),
        "wa": _block_diag_tiles(rg_wa, MXU_DIM).astype(BF16),
        "wx": _block_diag_tiles(rg_wx, MXU_DIM).astype(BF16),
        "vec": _rows8(jnp.concatenate(
            [v if v.ndim == 3 else v[:, None, :] for v in
             (rg_conv_w, rg_conv_b, rg_ba, rg_bx, rg_lambda, cf_conv_w, cf_conv_b, cf_ln_g, cf_ln_b)],
            axis=1)),
        "w_out": w_out.astype(BF16),
        "g_attn": norm_attn_g[:, None, :],
        "w_q": w_q.astype(BF16), "w_o": w_o.astype(BF16),
        "g_ffn": norm_ffn_g[:, None, :],
        "w_gate": w_gate.astype(BF16), "w_up": w_up.astype(BF16), "w_down": w_down.astype(BF16),
    }
    gf = norm_final_g[None, :]

    mem_k, mem_v, mem_kb, mem_vb = _memkv(
        mem_prompt.reshape(batch * n_mem, d), norm_mem_g[:, None, :],
        w_k, w_v, tile=512, n_mem=n_mem, heads=heads)

    rb_tm = jnp.transpose(state_rglru_conv, (0, 2, 1, 3))
    cb_tm = jnp.transpose(state_conf_conv, (0, 2, 1, 3))

    xp = x_prompt.reshape(batch * seq, d)
    xs_tm = jnp.transpose(x_sample, (1, 0, 2)).reshape(dec_seq * dec_batch, d)
    p_h, p_rg, p_cf, s_h, s_xr, s_c = [], [], [], [], [], []
    for l in range(depth):
        last = l == depth - 1

        xs_tm, hs, xr_tm, c_tm, sq_tm = _mix_sample(xs_tm, state_rglru_h, rb_tm, cb_tm, lw, l,
                                                    steps=dec_seq, batch=dec_batch, heads=heads)
        s_h.append(hs); s_xr.append(xr_tm); s_c.append(c_tm)
        sq_pad = jnp.pad(jnp.transpose(sq_tm.reshape(dec_seq, dec_batch, d), (1, 0, 2)),
                         ((0, 0), (0, pad_t - dec_seq), (0, 0))).reshape(dec_batch * pad_t, d)

        xp, hp, rgp, cfp, so_pad = _layer_prompt(
            xp, mem_kb, mem_vb, sq_pad, cache_mem_k, cache_mem_v, lw, gf, l, batch=batch, seq=seq,
            n_mem=n_mem, heads=heads, final_norm=last, pad_t=pad_t)
        p_h.append(hp[:, 0, :]); p_rg.append(rgp); p_cf.append(cfp)

        so_tm = jnp.transpose(so_pad.reshape(dec_batch, pad_t, d)[:, :dec_seq], (1, 0, 2)).reshape(
            dec_seq * dec_batch, d)
        xs_tm = _ffn(xs_tm, so_tm, lw, gf, l, tile=dec_seq * dec_batch, final_norm=last)

    def new_conv_state(old, fresh_tm, width):
        fresh = jnp.transpose(jnp.stack(fresh_tm).reshape(depth, dec_seq, dec_batch, -1), (0, 2, 1, 3))
        return jnp.concatenate([old, fresh], axis=2)[:, :, -(width - 1):]

    y_sample = jnp.transpose(xs_tm.reshape(dec_seq, dec_batch, d), (1, 0, 2))
    return (xp.reshape(batch, seq, d), y_sample,
            jnp.stack(p_h), jnp.stack(p_rg), jnp.stack(p_cf),
            mem_k, mem_v,
            jnp.stack(s_h),
            new_conv_state(state_rglru_conv, s_xr, RG_CONV_W),
            new_conv_state(state_conf_conv, s_c, CF_CONV_W))
```

```python
import functools
import math

import jax
import jax.numpy as jnp
from jax import lax
from jax.experimental import pallas as pl
from jax.experimental.pallas import tpu as pltpu

F32 = jnp.float32
BF16 = jnp.bfloat16

EPS = 1e-6
LRU_C = 8.0
RG_CONV_W = 4
CF_CONV_W = 31
SUBLANES = 8
MXU_DIM = 256
CF_HIST = 32
RG_HIST = SUBLANES
CHUNK = 16
OUT_CHUNK = 32
CONV_LANES = 256
CONV_GROUPS_PER_SEGMENT = 6
FFN_COLS = 512
PROMPT_TILE = 256
VMEM_LIMIT = 56 * 1024 * 1024


def _dot(a, b):
    return jnp.dot(a, b, preferred_element_type=F32)


def _rms(x, g):
    return x * lax.rsqrt(jnp.mean(x * x, axis=-1, keepdims=True) + EPS) * g


def _gelu_tanh(x):
    c = math.sqrt(2.0 / math.pi)
    return (0.5 * x) * (1.0 + jnp.tanh(x * (c + (c * 0.044715) * (x * x))))


def _sigmoid(x):
    return 0.5 * (jnp.tanh(0.5 * x) + 1.0)


def _silu(x):
    h = 0.5 * x
    return h * (jnp.tanh(h) + 1.0)


def _softplus(z):
    return jnp.maximum(z, 0.0) + jnp.log1p(jnp.exp(-jnp.abs(z)))


def _lru_coeffs(r_pre, i_pre, xc, ba, bx, half_nsp):
    log_a = (jnp.tanh(0.5 * (r_pre + ba)) + 1.0) * half_nsp
    i = _sigmoid(i_pre + bx)
    a = jnp.exp(log_a)
    t = jnp.tanh(log_a)
    mult = jnp.sqrt(-2.0 * t / (1.0 - t))
    return a, mult * (i * xc)


def _ln_silu(y, g, b):
    mu = jnp.mean(y, axis=-1, keepdims=True)
    yc = y - mu
    var = jnp.mean(yc * yc, axis=-1, keepdims=True)
    return _silu(yc * lax.rsqrt(var + EPS) * g + b)


def _query_scale(head_dim):
    return head_dim ** -0.5 * math.log2(math.e)


def _attend(q, k, v, heads, head_dim):
    outs = []
    for h in range(heads):
        sl = slice(h * head_dim, (h + 1) * head_dim)
        s = lax.dot_general(q[:, sl], k[:, sl], (((1,), (1,)), ((), ())), preferred_element_type=F32)
        p = jnp.exp2(s - jnp.max(s, axis=-1, keepdims=True))
        inv = 1.0 / jnp.sum(p, axis=-1, keepdims=True)
        outs.append(_dot(p.astype(BF16), v[:, sl]) * inv)
    return outs


def _ffn_cols(d_ff):
    return tuple((lo, min(lo + FFN_COLS, d_ff)) for lo in range(0, d_ff, FFN_COLS))


def _swiglu(x, g, wg_ref, wu_ref, wd_ref, h_s):
    z = _rms(x, g).astype(BF16)
    for lo, hi in _ffn_cols(wg_ref.shape[1]):
        h_s[:, lo:hi] = (_silu(_dot(z, wg_ref[:, lo:hi])) * _dot(z, wu_ref[:, lo:hi])).astype(BF16)
    return x + _dot(h_s[...], wd_ref[...])


def _mxu_cost(rows, k, n):
    return (rows // 2) * pl.cdiv(k, MXU_DIM) * pl.cdiv(n, MXU_DIM) // 2


def _full(shape):
    return pl.BlockSpec(shape, lambda *_: (0,) * len(shape))


def _layer_spec(arr, layer):
    zeros = (0,) * (arr.ndim - 1)
    return pl.BlockSpec((None,) + arr.shape[1:], lambda *_: (layer,) + zeros, pipeline_mode=pl.Buffered(1))


def _memkv_kernel(mem_ref, g_ref, wk_ref, wv_ref, *refs, n_cast):
    cast_in = refs[:n_cast]
    k_ref, v_ref, kb_ref, vb_ref = refs[n_cast:n_cast + 4]
    cast_out = refs[n_cast + 4:]
    for src, dst in zip(cast_in, cast_out):
        dst[...] = src[...].astype(BF16)
    m = _rms(mem_ref[...], g_ref[...]).astype(BF16)
    k = _dot(m, wk_ref[...].astype(BF16))
    v = _dot(m, wv_ref[...].astype(BF16))
    k_ref[...] = k.reshape(k_ref.shape)
    v_ref[...] = v.reshape(v_ref.shape)
    kb_ref[...] = k.astype(BF16)
    vb_ref[...] = v.astype(BF16)


def _memkv(mem2d, g, wk, wv, to_cast, tile, n_mem, heads):
    depth, d, _ = wk.shape
    rows = mem2d.shape[0]
    n_i = rows // tile
    n_steps = depth * n_i
    row_spec = pl.BlockSpec((None, tile, d), lambda l, i: (l, i, 0))
    out5_spec = pl.BlockSpec((None, tile // n_mem, n_mem, heads, d // heads), lambda l, i: (l, i, 0, 0, 0))
    out5_shape = jax.ShapeDtypeStruct((depth, rows // n_mem, n_mem, heads, d // heads), F32)
    w_spec = pl.BlockSpec((None, d, d), lambda l, i: (l, 0, 0))
    flat = [w.reshape(-1, w.shape[-1]) for w in to_cast]
    slab_specs = [pl.BlockSpec((f.shape[0] // n_steps, f.shape[1]), lambda l, i: (l * n_i + i, 0)) for f in flat]
    assert all(f.shape[0] % (n_steps * 2 * SUBLANES) == 0 for f in flat)
    outs = pl.pallas_call(
        functools.partial(_memkv_kernel, n_cast=len(flat)),
        grid=(depth, n_i),
        in_specs=[
            pl.BlockSpec((tile, d), lambda l, i: (i, 0)),
            pl.BlockSpec((None, 1, d), lambda l, i: (l, 0, 0)),
            w_spec, w_spec,
            *slab_specs,
        ],
        out_specs=[out5_spec, out5_spec, row_spec, row_spec, *slab_specs],
        out_shape=[
            out5_shape,
            out5_shape,
            jax.ShapeDtypeStruct((depth, rows, d), BF16),
            jax.ShapeDtypeStruct((depth, rows, d), BF16),
            *[jax.ShapeDtypeStruct(f.shape, BF16) for f in flat],
        ],
        compiler_params=pltpu.CompilerParams(
            dimension_semantics=("arbitrary", "arbitrary"), vmem_limit_bytes=VMEM_LIMIT),
        name="memkv",
    )(mem2d, g, wk, wv, *flat)
    casted = [o.reshape(w.shape) for o, w in zip(outs[4:], to_cast)]
    return outs[0], outs[1], outs[2], outs[3], casted


MIX_PARAMS = ("g_mix", "w_in", "wa", "wx", "vec")
VEC_ROWS = (("w4", RG_CONV_W), ("b4", None), ("ba", None), ("bx", None), ("lam", None),
            ("w31", CF_CONV_W), ("b31", None), ("ln_g", None), ("ln_b", None))


def _unpack_vec(vec_ref):
    views, row = {}, 0
    for name, count in VEC_ROWS:
        views[name] = vec_ref.at[row] if count is None else vec_ref.at[row:row + count]
        row += count or 1
    return views


def _gate_products(lhs, wa_ref, wx_ref, p_s, q_s):
    n_tiles, width = wa_ref.shape[0], wa_ref.shape[1]
    for j in range(n_tiles):
        cols = slice(j * width, (j + 1) * width)
        p_s[:, cols] = _dot(lhs[:, cols], wa_ref[j])
        q_s[:, cols] = _dot(lhs[:, cols], wx_ref[j])
BLOCK_PARAMS = ("w_out", "g_attn", "w_q", "w_o", "g_ffn", "w_gate", "w_up", "w_down")


def _group_rows(gi, offset=0):
    return pl.ds(gi * SUBLANES + offset, SUBLANES)


def _mix_stage(x_ref, prm, xr_hist, c_hist, hcar, gr_s, xc_s, p_s, q_s, yc_s, xn_s, cat_s, *, tile, c_w):
    win_ref = prm["w_in"]
    n_groups = tile // SUBLANES
    sub8 = lax.broadcasted_iota(jnp.int32, (SUBLANES, c_w), 0)

    yield 2 * _mxu_cost(tile, win_ref.shape[0], c_w), 2 * tile
    xn_s[...] = _rms(x_ref[...], prm["g_mix"][...]).astype(BF16)
    c_hist[pl.ds(CF_HIST, tile), :] = (
        _dot(xn_s[...], win_ref[:, 2 * c_w:3 * c_w]) * _sigmoid(_dot(xn_s[...], win_ref[:, 3 * c_w:4 * c_w])))
    yield 2 * _mxu_cost(tile, win_ref.shape[0], c_w), tile // 4
    xr_hist[pl.ds(RG_HIST, tile), :] = _dot(xn_s[...], win_ref[:, 0:c_w])
    gr_s[...] = _dot(xn_s[...], win_ref[:, c_w:2 * c_w])

    first_tap_off = CF_HIST - (CF_CONV_W - 1)
    n_hist_groups = CF_HIST // SUBLANES + 1
    w31_ref = prm["w31"]
    for lane0 in range(0, c_w, CONV_LANES):
        lanes = pl.ds(lane0, CONV_LANES)
        sub_l = lax.broadcasted_iota(jnp.int32, (SUBLANES, CONV_LANES), 0)
        zero = jnp.zeros((SUBLANES, CONV_LANES), F32)
        p0_prev, rolled_prev = zero, (zero,) * (SUBLANES - 1)
        for si in range(n_groups + 1):
            if si % CONV_GROUPS_PER_SEGMENT == 0:
                yield 0, 26 * (CONV_LANES // 128) * min(CONV_GROUPS_PER_SEGMENT, n_groups + 1 - si)
            hist = [c_hist[_group_rows(si, SUBLANES * q), lanes] for q in range(n_hist_groups)]
            parts = []
            for res in range(SUBLANES):
                part = None
                for q in range(n_hist_groups):
                    k = SUBLANES * q + res - first_tap_off
                    if 0 <= k < CF_CONV_W:
                        term = hist[q] * w31_ref[k, :, lanes]
                        part = term if part is None else part + term
                parts.append(part)
            rolled = tuple(pltpu.roll(parts[res], SUBLANES - res, axis=0) for res in range(1, SUBLANES))
            y = p0_prev
            for res in range(1, SUBLANES):
                y = y + jnp.where(sub_l < SUBLANES - res, rolled_prev[res - 1], rolled[res - 1])
            yc_s[_group_rows(si), lanes] = y
            p0_prev, rolled_prev = parts[0], rolled

    yield 0, 20 * n_groups
    w4_ref = prm["w4"]
    x_hist = xr_hist[pl.ds(0, RG_HIST), :]
    prev_rolled = [pltpu.roll(x_hist, dly, axis=0) for dly in range(1, RG_CONV_W)]
    for gi in range(n_groups):
        xg = xr_hist[_group_rows(gi, RG_HIST), :]
        acc = xg * w4_ref[RG_CONV_W - 1] + prm["b4"][...]
        rolled = []
        for dly in range(1, RG_CONV_W):
            rolled.append(pltpu.roll(xg, dly, axis=0))
            acc = acc + jnp.where(sub8 >= dly, rolled[-1], prev_rolled[dly - 1]) * w4_ref[RG_CONV_W - 1 - dly]
        xc_s[_group_rows(gi), :] = acc
        prev_rolled = rolled

    n_gate_tiles, gate_width = prm["wa"].shape[0], prm["wa"].shape[1]
    yield 2 * n_gate_tiles * _mxu_cost(tile, gate_width, gate_width), tile // 4
    xcb_s = xn_s.at[:, 0:c_w]
    xcb_s[...] = xc_s[...].astype(BF16)
    _gate_products(xcb_s, prm["wa"], prm["wx"], p_s, q_s)

    half_nsp = (-0.5 * LRU_C) * _softplus(-prm["lam"][...])
    n_grp = CHUNK // SUBLANES
    grp = (n_grp, SUBLANES, c_w)
    sub = lax.broadcasted_iota(jnp.int32, grp, 1)
    for ci in range(tile // CHUNK):
        if ci % 2 == 0:
            yield 0, 240
        rows = pl.ds(ci * CHUNK, CHUNK)
        a, b = _lru_coeffs(p_s[rows, :].reshape(grp), q_s[rows, :].reshape(grp),
                           xc_s[rows, :].reshape(grp), prm["ba"][...], prm["bx"][...], half_nsp)
        for dly in (1, 2, 4):
            ra = pltpu.roll(a, dly, axis=1)
            rb = pltpu.roll(b, dly, axis=1)
            keep = sub >= dly
            b = a * jnp.where(keep, rb, 0.0) + b
            a = a * jnp.where(keep, ra, 1.0)
        p_s[rows, :] = a.reshape(CHUNK, c_w)
        q_s[rows, :] = b.reshape(CHUNK, c_w)

    yield 0, 21 * n_groups
    hb = hcar[...]
    for gi in range(n_groups):
        rows = _group_rows(gi)
        h = p_s[rows, :] * hb + q_s[rows, :]
        q_s[rows, :] = h
        hb = jnp.broadcast_to(h[SUBLANES - 1:SUBLANES, :], (SUBLANES, c_w))
    hcar[...] = hb

    for ci in range(tile // OUT_CHUNK):
        yield 0, 220
        rows = pl.ds(ci * OUT_CHUNK, OUT_CHUNK)
        y_r = q_s[rows, :] * _gelu_tanh(gr_s[rows, :])
        cat_s[rows, 0:c_w] = y_r.astype(BF16)
        conv = yc_s[pl.ds(ci * OUT_CHUNK + SUBLANES, OUT_CHUNK), :].reshape(
            OUT_CHUNK // SUBLANES, SUBLANES, c_w)
        y_c = _ln_silu(conv + prm["b31"][...], prm["ln_g"][...], prm["ln_b"][...])
        cat_s[rows, c_w:2 * c_w] = y_c.reshape(OUT_CHUNK, c_w).astype(BF16)


def _block_stage(x_ref, cat_s, k_ref, v_ref, prm, gf_ref, xo_ref, xb_s, zb_s, qb_s, o_s, h_s,
                 *, heads, final_norm):
    d = x_ref.shape[1]
    head_dim = d // heads
    rows = x_ref.shape[0]
    yield _mxu_cost(rows, d, d), rows // 2
    xb_s[...] = x_ref[...] + _dot(cat_s[...], prm["w_out"][...])
    yield _mxu_cost(rows, d, d), 2 * rows
    zb_s[...] = _rms(xb_s[...], prm["g_attn"][...]).astype(BF16)
    qb_s[...] = (_dot(zb_s[...], prm["w_q"][...]) * _query_scale(head_dim)).astype(BF16)
    for h in range(heads):
        yield rows // 2, rows
        sl = slice(h * head_dim, (h + 1) * head_dim)
        (o,) = _attend(qb_s[:, sl], k_ref[:, sl], v_ref[:, sl], 1, head_dim)
        o_s[:, sl] = o.astype(BF16)
    yield _mxu_cost(rows, d, d), rows // 2
    xb_s[...] = xb_s[...] + _dot(o_s[...], prm["w_o"][...])
    yield 0, rows
    zb_s[...] = _rms(xb_s[...], prm["g_ffn"][...]).astype(BF16)
    wg_ref, wu_ref, wd_ref = prm["w_gate"], prm["w_up"], prm["w_down"]
    for lo, hi in _ffn_cols(wg_ref.shape[1]):
        yield 2 * _mxu_cost(rows, d, hi - lo), (hi - lo) // 2
        h_s[:, lo:hi] = (_silu(_dot(zb_s[...], wg_ref[:, lo:hi])) * _dot(zb_s[...], wu_ref[:, lo:hi])
                         ).astype(BF16)
    for lo, hi in _ffn_cols(d):
        yield _mxu_cost(rows, wd_ref.shape[0], hi - lo), rows // 2
        xo_ref[:, lo:hi] = xb_s[:, lo:hi] + _dot(h_s[...], wd_ref[:, lo:hi])
    if final_norm:
        yield 0, rows
        xo_ref[...] = _rms(xo_ref[...], gf_ref[...])


def _interleave(*stages):
    pending = {stage: next(stage) for stage in stages}
    clock = {stage: i for i, stage in enumerate(stages)}
    while pending:
        stage = min(pending, key=clock.get)
        clock[stage] += max(pending.pop(stage))
        try:
            pending[stage] = next(stage)
        except StopIteration:
            pass


def _layer_prompt_kernel(*refs, tile, tiles_per_seq, c_w, heads, final_norm, pad_t):
    n_mix, n_blk = len(MIX_PARAMS), len(BLOCK_PARAMS)
    xa_ref, xb_ref, k_ref, v_ref, sq_ref, sk_ref, sv_ref = refs[:7]
    refs = refs[7:]
    mix_prm = dict(zip(MIX_PARAMS, refs[:n_mix]))
    mix_prm.update(_unpack_vec(mix_prm["vec"]))
    blk_prm = dict(zip(BLOCK_PARAMS, refs[n_mix:n_mix + n_blk]))
    gf_ref = refs[n_mix + n_blk]
    xo_ref, h_ref, rg_ref, cf_ref, so_ref = refs[1 + n_mix + n_blk:6 + n_mix + n_blk]
    (xr_hist, c_hist, hcar, gr_s, xc_s, p_s, q_s, yc_s, xn_s, cat_s,
     xb_s, zb_s, qb_s, o_s, h_s) = refs[6 + n_mix + n_blk:]

    g = pl.program_id(0)
    pos = g % tiles_per_seq

    @pl.when(g == 0)
    def _():
        cat_s[...] = jnp.zeros(cat_s.shape, BF16)
        c_hist[pl.ds(CF_HIST + tile, SUBLANES), :] = jnp.zeros((SUBLANES, c_w), F32)

    @pl.when(pos == 0)
    def _():
        xr_hist[pl.ds(0, RG_HIST), :] = jnp.zeros((RG_HIST, c_w), F32)
        c_hist[pl.ds(0, CF_HIST), :] = jnp.zeros((CF_HIST, c_w), F32)
        hcar[...] = jnp.zeros((SUBLANES, c_w), F32)

    _interleave(
        _block_stage(xb_ref, cat_s, k_ref, v_ref, blk_prm, gf_ref, xo_ref, xb_s, zb_s, qb_s, o_s, h_s,
                     heads=heads, final_norm=final_norm),
        _mix_stage(xa_ref, mix_prm, xr_hist, c_hist, hcar, gr_s, xc_s, p_s, q_s, yc_s, xn_s, cat_s,
                   tile=tile, c_w=c_w),
        _sample_attn_stage(sq_ref, sk_ref, sv_ref, so_ref, heads=heads, pad_t=pad_t))

    @pl.when(pos == tiles_per_seq - 1)
    def _():
        h_ref[...] = hcar[0:1, :]
        rg_ref[...] = xr_hist[pl.ds(RG_HIST + tile - (RG_CONV_W - 1), RG_CONV_W - 1), :]
        cf_ref[...] = c_hist[pl.ds(CF_HIST + tile - (CF_CONV_W - 1), CF_CONV_W - 1), :]

    xr_hist[pl.ds(0, RG_HIST), :] = xr_hist[pl.ds(tile, RG_HIST), :]
    c_hist[pl.ds(0, CF_HIST), :] = c_hist[pl.ds(tile, CF_HIST), :]


def _layer_prompt(x2d, kb, vb, sq_pad, cache_k, cache_v, lw, gf, layer,
                  *, batch, seq, n_mem, heads, final_norm, pad_t):
    rows, d = x2d.shape
    tile = PROMPT_TILE
    c_w = lw["vec"].shape[-1]
    d_ff = lw["w_gate"].shape[-1]
    tps = seq // tile
    n_tiles = batch * tps
    dec_batch = cache_k.shape[1]
    nb = dec_batch // n_tiles
    assert nb * n_tiles == dec_batch and sq_pad.shape[0] == dec_batch * pad_t

    def mix_tile(g):
        return jnp.minimum(g, n_tiles - 1)

    def blk_tile(g):
        return jnp.maximum(g - 1, 0)

    kv_spec = pl.BlockSpec((None, n_mem, d), lambda g: (layer, blk_tile(g) // tps, 0))
    sq_spec = pl.BlockSpec((nb * pad_t, d), lambda g: (mix_tile(g), 0))
    cache_spec = pl.BlockSpec((None, nb, n_mem, heads, d // heads), lambda g: (layer, mix_tile(g), 0, 0, 0))
    params = [lw[name] for name in MIX_PARAMS + BLOCK_PARAMS]
    kern = functools.partial(_layer_prompt_kernel, tile=tile, tiles_per_seq=tps, c_w=c_w,
                             heads=heads, final_norm=final_norm, pad_t=pad_t)
    return pl.pallas_call(
        kern,
        grid=(n_tiles + 1,),
        in_specs=[
            pl.BlockSpec((tile, d), lambda g: (mix_tile(g), 0)),
            pl.BlockSpec((tile, d), lambda g: (blk_tile(g), 0)),
            kv_spec, kv_spec,
            sq_spec, cache_spec, cache_spec,
            *[_layer_spec(p, layer) for p in params],
            _full(gf.shape),
        ],
        out_specs=[
            pl.BlockSpec((tile, d), lambda g: (blk_tile(g), 0)),
            pl.BlockSpec((None, 1, c_w), lambda g: (mix_tile(g) // tps, 0, 0)),
            pl.BlockSpec((None, RG_CONV_W - 1, c_w), lambda g: (mix_tile(g) // tps, 0, 0)),
            pl.BlockSpec((None, CF_CONV_W - 1, c_w), lambda g: (mix_tile(g) // tps, 0, 0)),
            sq_spec,
        ],
        out_shape=[
            jax.ShapeDtypeStruct((rows, d), F32),
            jax.ShapeDtypeStruct((batch, 1, c_w), F32),
            jax.ShapeDtypeStruct((batch, RG_CONV_W - 1, c_w), F32),
            jax.ShapeDtypeStruct((batch, CF_CONV_W - 1, c_w), F32),
            jax.ShapeDtypeStruct(sq_pad.shape, F32),
        ],
        scratch_shapes=[
            pltpu.VMEM((RG_HIST + tile, c_w), F32),
            pltpu.VMEM((CF_HIST + tile + SUBLANES, c_w), F32),
            pltpu.VMEM((SUBLANES, c_w), F32),
            pltpu.VMEM((tile, c_w), F32),
            pltpu.VMEM((tile, c_w), F32),
            pltpu.VMEM((tile, c_w), F32),
            pltpu.VMEM((tile, c_w), F32),
            pltpu.VMEM((tile + SUBLANES, c_w), F32),
            pltpu.VMEM((tile, d), BF16),
            pltpu.VMEM((tile, 2 * c_w), BF16),
            pltpu.VMEM((tile, d), F32),
            pltpu.VMEM((tile, d), BF16),
            pltpu.VMEM((tile, d), BF16),
            pltpu.VMEM((tile, d), BF16),
            pltpu.VMEM((tile, d_ff), BF16),
        ],
        compiler_params=pltpu.CompilerParams(
            dimension_semantics=("arbitrary",), vmem_limit_bytes=VMEM_LIMIT),
        name="layer_prompt",
    )(x2d, x2d, kb, vb, sq_pad, cache_k, cache_v, *params, gf)


def _mix_sample_kernel(x_ref, h0_ref, rb_ref, cb_ref, g_ref, win_ref, wa_ref, wx_ref, vec_ref, wout_ref,
                       ga_ref, wq_ref,
                       xo_ref, h_ref, xr_ref, c_ref, qo_ref,
                       gr_s, xc_s, p_s, q_s, cat_s, *, steps, batch, d_rnn, heads):
    c_w = d_rnn
    vec = _unpack_vec(vec_ref)
    w4_ref, b4_ref, ba_ref, bx_ref, lam_ref = vec["w4"], vec["b4"], vec["ba"], vec["bx"], vec["lam"]
    w31_ref, b31_ref, lng_ref, lnb_ref = vec["w31"], vec["b31"], vec["ln_g"], vec["ln_b"]
    xn = _rms(x_ref[...], g_ref[...]).astype(BF16)
    xr_ref[...] = _dot(xn, win_ref[:, 0:c_w])
    gr_s[...] = _dot(xn, win_ref[:, c_w:2 * c_w])
    c_ref[...] = (
        _dot(xn, win_ref[:, 2 * c_w:3 * c_w]) * _sigmoid(_dot(xn, win_ref[:, 3 * c_w:4 * c_w])))

    def rows_of(t, s):
        return pl.ds(pl.multiple_of(t * batch + s, SUBLANES), SUBLANES)

    def conv4_blk(bi, carry):
        s = bi * SUBLANES
        for t in range(steps):
            acc = b4_ref[...]
            for k in range(RG_CONV_W):
                jj = t + k
                if jj < RG_CONV_W - 1:
                    src = rb_ref[jj, pl.ds(pl.multiple_of(s, SUBLANES), SUBLANES), :]
                else:
                    src = xr_ref[rows_of(jj - (RG_CONV_W - 1), s), :]
                acc = acc + src * w4_ref[k]
            xc_s[rows_of(t, s), :] = acc
        return carry

    lax.fori_loop(0, batch // SUBLANES, conv4_blk, 0)

    _gate_products(xc_s[...].astype(BF16), wa_ref, wx_ref, p_s, q_s)
    half_nsp = (-0.5 * LRU_C) * _softplus(-lam_ref[...])

    def scan_blk(bi, carry):
        s = bi * SUBLANES
        b_rows = pl.ds(pl.multiple_of(s, SUBLANES), SUBLANES)
        h = h0_ref[b_rows, :]
        for t in range(steps):
            rows = rows_of(t, s)
            a, b = _lru_coeffs(p_s[rows, :], q_s[rows, :], xc_s[rows, :], ba_ref[...], bx_ref[...], half_nsp)
            h = a * h + b
            cat_s[rows, 0:c_w] = h * _gelu_tanh(gr_s[rows, :])
        h_ref[b_rows, :] = h
        return carry

    lax.fori_loop(0, batch // SUBLANES, scan_blk, 0)

    def conv31_blk(bi, carry):
        s = bi * SUBLANES
        for t in range(steps):
            acc = b31_ref[...]
            for k in range(CF_CONV_W):
                jj = t + k
                if jj < CF_CONV_W - 1:
                    src = cb_ref[jj, pl.ds(pl.multiple_of(s, SUBLANES), SUBLANES), :]
                else:
                    src = c_ref[rows_of(jj - (CF_CONV_W - 1), s), :]
                acc = acc + src * w31_ref[k]
            cat_s[rows_of(t, s), c_w:2 * c_w] = _ln_silu(acc, lng_ref[...], lnb_ref[...])
        return carry

    lax.fori_loop(0, batch // SUBLANES, conv31_blk, 0)

    xo = x_ref[...] + _dot(cat_s[...].astype(BF16), wout_ref[...])
    xo_ref[...] = xo
    head_dim = xo.shape[1] // heads
    qo_ref[...] = _dot(_rms(xo, ga_ref[...]).astype(BF16), wq_ref[...]) * _query_scale(head_dim)


def _mix_sample(x_tm, h0, rb_tm, cb_tm, lw, layer, *, steps, batch, heads):
    rows, d = x_tm.shape
    c_w = lw["vec"].shape[-1]
    kern = functools.partial(_mix_sample_kernel, steps=steps, batch=batch, d_rnn=c_w, heads=heads)
    params = [lw[name] for name in MIX_PARAMS + ("w_out", "g_attn", "w_q")]
    return pl.pallas_call(
        kern,
        grid=(1,),
        in_specs=[_full(x_tm.shape), _layer_spec(h0, layer), _layer_spec(rb_tm, layer),
                  _layer_spec(cb_tm, layer),
                  *[_layer_spec(p, layer) for p in params]],
        out_specs=[_full((rows, d)), _full((batch, c_w)), _full((rows, c_w)), _full((rows, c_w)),
                   _full((rows, d))],
        out_shape=[
            jax.ShapeDtypeStruct((rows, d), F32),
            jax.ShapeDtypeStruct((batch, c_w), F32),
            jax.ShapeDtypeStruct((rows, c_w), F32),
            jax.ShapeDtypeStruct((rows, c_w), F32),
            jax.ShapeDtypeStruct((rows, d), F32),
        ],
        scratch_shapes=[
            pltpu.VMEM((rows, c_w), F32),
            pltpu.VMEM((rows, c_w), F32),
            pltpu.VMEM((rows, c_w), F32),
            pltpu.VMEM((rows, c_w), F32),
            pltpu.VMEM((rows, 2 * c_w), F32),
        ],
        compiler_params=pltpu.CompilerParams(
            dimension_semantics=("arbitrary",), vmem_limit_bytes=VMEM_LIMIT),
        name="mix_sample",
    )(x_tm, h0, rb_tm, cb_tm, *params)


def _sample_attn_stage(q_ref, k_ref, v_ref, o_ref, *, heads, pad_t):
    nb, n_mem, _, head_dim = k_ref.shape
    n_kv = n_mem * heads
    col_head = lax.broadcasted_iota(jnp.int32, (heads * pad_t, n_kv), 1) % heads
    row_head = lax.broadcasted_iota(jnp.int32, (heads * pad_t, n_kv), 0) // pad_t
    own_head = col_head == row_head
    for bb in range(nb):
        yield 2 * _mxu_cost(heads * pad_t, head_dim, n_kv) + n_kv // 2, n_kv // 2
        rows = pl.ds(bb * pad_t, pad_t)
        q = q_ref[rows, :]
        qs = jnp.concatenate([q[:, h * head_dim:(h + 1) * head_dim] for h in range(heads)], axis=0)
        k2 = k_ref[bb].reshape(n_kv, head_dim).astype(BF16)
        v2 = v_ref[bb].reshape(n_kv, head_dim).astype(BF16)
        s = lax.dot_general(qs.astype(BF16), k2, (((1,), (1,)), ((), ())), preferred_element_type=F32)
        s = jnp.where(own_head, s, -1e30)
        p = jnp.exp2(s - jnp.max(s, axis=-1, keepdims=True))
        inv = 1.0 / jnp.sum(p, axis=-1, keepdims=True)
        o = _dot(p.astype(BF16), v2) * inv
        for h in range(heads):
            o_ref[rows, h * head_dim:(h + 1) * head_dim] = o[h * pad_t:(h + 1) * pad_t, :]


def _ffn_kernel(x_ref, o_ref, wo_ref, g_ref, wg_ref, wu_ref, wd_ref, gf_ref, xo_ref, h_s, *, final_norm):
    x = x_ref[...] + _dot(o_ref[...].astype(BF16), wo_ref[...])
    y = _swiglu(x, g_ref[...], wg_ref, wu_ref, wd_ref, h_s)
    if final_norm:
        y = _rms(y, gf_ref[...])
    xo_ref[...] = y


def _ffn(x2d, o2d, lw, gf, layer, *, tile, final_norm):
    rows, d = x2d.shape
    d_ff = lw["w_gate"].shape[-1]
    row_spec = pl.BlockSpec((tile, d), lambda i: (i, 0))
    kern = functools.partial(_ffn_kernel, final_norm=final_norm)
    return pl.pallas_call(
        kern,
        grid=(rows // tile,),
        in_specs=[row_spec, row_spec, _layer_spec(lw["w_o"], layer), _layer_spec(lw["g_ffn"], layer),
                  _layer_spec(lw["w_gate"], layer), _layer_spec(lw["w_up"], layer),
                  _layer_spec(lw["w_down"], layer), _full(gf.shape)],
        out_specs=row_spec,
        out_shape=jax.ShapeDtypeStruct((rows, d), F32),
        scratch_shapes=[pltpu.VMEM((tile, d_ff), BF16)],
        compiler_params=pltpu.CompilerParams(
            dimension_semantics=("arbitrary",), vmem_limit_bytes=VMEM_LIMIT),
        name="ffn",
    )(x2d, o2d, lw["w_o"], lw["g_ffn"], lw["w_gate"], lw["w_up"], lw["w_down"], gf)


def _rows8(v):
    return jnp.broadcast_to(v[..., None, :], v.shape[:-1] + (SUBLANES, v.shape[-1]))


def _block_diag_tiles(w, width):
    l, h, i, _ = w.shape
    per_tile = width // i
    eye = jnp.eye(per_tile, dtype=w.dtype)
    tiles = jnp.einsum("lthij,hg->lthigj", w.reshape(l, h // per_tile, per_tile, i, i), eye)
    return tiles.reshape(l, h // per_tile, width, width)


def kernel(x_prompt, x_sample, state_rglru_h, state_rglru_conv, state_conf_conv, cache_mem_k, cache_mem_v, mem_prompt, norm_mix_g, w_in, rg_conv_w, rg_conv_b, rg_wa, rg_ba, rg_wx, rg_bx, rg_lambda, cf_conv_w, cf_conv_b, cf_ln_g, cf_ln_b, w_out, norm_attn_g, norm_mem_g, w_q, w_k, w_v, w_o, norm_ffn_g, w_gate, w_up, w_down, norm_final_g):
    batch, seq, d = x_prompt.shape
    dec_batch, dec_seq, _ = x_sample.shape
    depth = w_in.shape[0]
    n_mem = mem_prompt.shape[1]
    heads = cache_mem_k.shape[3]
    pad_t = SUBLANES

    lw = {
        "g_mix": norm_mix_g[:, None, :],
        "wa": _block_diag_tiles(rg_wa, MXU_DIM).astype(BF16),
        "wx": _block_diag_tiles(rg_wx, MXU_DIM).astype(BF16),
        "vec": _rows8(jnp.concatenate(
            [v if v.ndim == 3 else v[:, None, :] for v in
             (rg_conv_w, rg_conv_b, rg_ba, rg_bx, rg_lambda, cf_conv_w, cf_conv_b, cf_ln_g, cf_ln_b)],
            axis=1)),
        "g_attn": norm_attn_g[:, None, :],
        "g_ffn": norm_ffn_g[:, None, :],
    }
    gf = norm_final_g[None, :]

    big = {"w_in": w_in, "w_out": w_out, "w_q": w_q, "w_o": w_o,
           "w_gate": w_gate, "w_up": w_up, "w_down": w_down}
    mem_k, mem_v, mem_kb, mem_vb, big_bf16 = _memkv(
        mem_prompt.reshape(batch * n_mem, d), norm_mem_g[:, None, :],
        w_k, w_v, tuple(big.values()), tile=n_mem, n_mem=n_mem, heads=heads)
    lw.update(zip(big, big_bf16))

    rb_tm = jnp.transpose(state_rglru_conv, (0, 2, 1, 3))
    cb_tm = jnp.transpose(state_conf_conv, (0, 2, 1, 3))

    xp = x_prompt.reshape(batch * seq, d)
    xs_tm = jnp.transpose(x_sample, (1, 0, 2)).reshape(dec_seq * dec_batch, d)
    p_h, p_rg, p_cf, s_h, s_xr, s_c = [], [], [], [], [], []
    for l in range(depth):
        last = l == depth - 1

        xs_tm, hs, xr_tm, c_tm, sq_tm = _mix_sample(xs_tm, state_rglru_h, rb_tm, cb_tm, lw, l,
                                                    steps=dec_seq, batch=dec_batch, heads=heads)
        s_h.append(hs); s_xr.append(xr_tm); s_c.append(c_tm)
        sq_pad = jnp.pad(jnp.transpose(sq_tm.reshape(dec_seq, dec_batch, d), (1, 0, 2)),
                         ((0, 0), (0, pad_t - dec_seq), (0, 0))).reshape(dec_batch * pad_t, d)

        xp, hp, rgp, cfp, so_pad = _layer_prompt(
            xp, mem_kb, mem_vb, sq_pad, cache_mem_k, cache_mem_v, lw, gf, l, batch=batch, seq=seq,
            n_mem=n_mem, heads=heads, final_norm=last, pad_t=pad_t)
        p_h.append(hp[:, 0, :]); p_rg.append(rgp); p_cf.append(cfp)

        so_tm = jnp.transpose(so_pad.reshape(dec_batch, pad_t, d)[:, :dec_seq], (1, 0, 2)).reshape(
            dec_seq * dec_batch, d)
        xs_tm = _ffn(xs_tm, so_tm, lw, gf, l, tile=dec_seq * dec_batch, final_norm=last)

    def new_conv_state(old, fresh_tm, width):
        fresh = jnp.transpose(jnp.stack(fresh_tm).reshape(depth, dec_seq, dec_batch, -1), (0, 2, 1, 3))
        return jnp.concatenate([old, fresh], axis=2)[:, :, -(width - 1):]

    y_sample = jnp.transpose(xs_tm.reshape(dec_seq, dec_batch, d), (1, 0, 2))
    return (xp.reshape(batch, seq, d), y_sample,
            jnp.stack(p_h), jnp.stack(p_rg), jnp.stack(p_cf),
            mem_k, mem_v,
            jnp.stack(s_h),
            new_conv_state(state_rglru_conv, s_xr, RG_CONV_W),
            new_conv_state(state_conf_conv, s_c, CF_CONV_W))
```

```python
import functools
import math

import jax
import jax.numpy as jnp
from jax import lax
from jax.experimental import pallas as pl
from jax.experimental.pallas import tpu as pltpu

F32 = jnp.float32
BF16 = jnp.bfloat16

EPS = 1e-6
LRU_C = 8.0
RG_CONV_W = 4
CF_CONV_W = 31
SUBLANES = 8
MXU_DIM = 256
CF_HIST = 32
RG_HIST = SUBLANES
CHUNK = 16
OUT_CHUNK = 32
CONV_LANES = 256
CONV_GROUPS_PER_SEGMENT = 6
FFN_COLS = 512
PROMPT_TILE = 256
VMEM_LIMIT = 56 * 1024 * 1024


def _dot(a, b):
    return jnp.dot(a, b, preferred_element_type=F32)


def _rms(x, g):
    return x * lax.rsqrt(jnp.mean(x * x, axis=-1, keepdims=True) + EPS) * g


def _gelu_tanh(x):
    c = math.sqrt(2.0 / math.pi)
    return (0.5 * x) * (1.0 + jnp.tanh(x * (c + (c * 0.044715) * (x * x))))


def _sigmoid(x):
    return 0.5 * (jnp.tanh(0.5 * x) + 1.0)


def _silu(x):
    h = 0.5 * x
    return h * (jnp.tanh(h) + 1.0)


def _softplus(z):
    return jnp.maximum(z, 0.0) + jnp.log1p(jnp.exp(-jnp.abs(z)))


def _lru_coeffs(r_pre, i_pre, xc, ba, bx, half_nsp):
    log_a = (jnp.tanh(0.5 * (r_pre + ba)) + 1.0) * half_nsp
    i = _sigmoid(i_pre + bx)
    a = jnp.exp(log_a)
    t = jnp.tanh(log_a)
    mult = jnp.sqrt(-2.0 * t / (1.0 - t))
    return a, mult * (i * xc)


def _ln_silu(y, g, b):
    mu = jnp.mean(y, axis=-1, keepdims=True)
    yc = y - mu
    var = jnp.mean(yc * yc, axis=-1, keepdims=True)
    return _silu(yc * lax.rsqrt(var + EPS) * g + b)


def _query_scale(head_dim):
    return head_dim ** -0.5 * math.log2(math.e)


def _attend(q, k, v, heads, head_dim):
    outs = []
    for h in range(heads):
        sl = slice(h * head_dim, (h + 1) * head_dim)
        s = lax.dot_general(q[:, sl], k[:, sl], (((1,), (1,)), ((), ())), preferred_element_type=F32)
        p = jnp.exp2(s - jnp.max(s, axis=-1, keepdims=True))
        inv = 1.0 / jnp.sum(p, axis=-1, keepdims=True)
        outs.append(_dot(p.astype(BF16), v[:, sl]) * inv)
    return outs


def _ffn_cols(d_ff):
    return tuple((lo, min(lo + FFN_COLS, d_ff)) for lo in range(0, d_ff, FFN_COLS))


def _mxu_cost(rows, k, n):
    return (rows // 2) * pl.cdiv(k, MXU_DIM) * pl.cdiv(n, MXU_DIM) // 2


def _full(shape):
    return pl.BlockSpec(shape, lambda *_: (0,) * len(shape))


def _layer_spec(arr, layer):
    zeros = (0,) * (arr.ndim - 1)
    return pl.BlockSpec((None,) + arr.shape[1:], lambda *_: (layer,) + zeros, pipeline_mode=pl.Buffered(1))


def _memkv_kernel(mem_ref, g_ref, wk_ref, wv_ref, *refs, n_cast):
    cast_in = refs[:n_cast]
    k_ref, v_ref, kb_ref, vb_ref = refs[n_cast:n_cast + 4]
    cast_out = refs[n_cast + 4:]
    for src, dst in zip(cast_in, cast_out):
        dst[...] = src[...].astype(BF16)
    m = _rms(mem_ref[...], g_ref[...]).astype(BF16)
    k = _dot(m, wk_ref[...].astype(BF16))
    v = _dot(m, wv_ref[...].astype(BF16))
    k_ref[...] = k.reshape(k_ref.shape)
    v_ref[...] = v.reshape(v_ref.shape)
    kb_ref[...] = k.astype(BF16)
    vb_ref[...] = v.astype(BF16)


def _memkv(mem2d, g, wk, wv, to_cast, tile, n_mem, heads):
    depth, d, _ = wk.shape
    rows = mem2d.shape[0]
    n_i = rows // tile
    n_steps = depth * n_i
    row_spec = pl.BlockSpec((None, tile, d), lambda l, i: (l, i, 0))
    out5_spec = pl.BlockSpec((None, tile // n_mem, n_mem, heads, d // heads), lambda l, i: (l, i, 0, 0, 0))
    out5_shape = jax.ShapeDtypeStruct((depth, rows // n_mem, n_mem, heads, d // heads), F32)
    w_spec = pl.BlockSpec((None, d, d), lambda l, i: (l, 0, 0))
    flat = [w.reshape(-1, w.shape[-1]) for w in to_cast]
    slab_specs = [pl.BlockSpec((f.shape[0] // n_steps, f.shape[1]), lambda l, i: (l * n_i + i, 0)) for f in flat]
    assert all(f.shape[0] % (n_steps * 2 * SUBLANES) == 0 for f in flat)
    outs = pl.pallas_call(
        functools.partial(_memkv_kernel, n_cast=len(flat)),
        grid=(depth, n_i),
        in_specs=[
            pl.BlockSpec((tile, d), lambda l, i: (i, 0)),
            pl.BlockSpec((None, 1, d), lambda l, i: (l, 0, 0)),
            w_spec, w_spec,
            *slab_specs,
        ],
        out_specs=[out5_spec, out5_spec, row_spec, row_spec, *slab_specs],
        out_shape=[
            out5_shape,
            out5_shape,
            jax.ShapeDtypeStruct((depth, rows, d), BF16),
            jax.ShapeDtypeStruct((depth, rows, d), BF16),
            *[jax.ShapeDtypeStruct(f.shape, BF16) for f in flat],
        ],
        compiler_params=pltpu.CompilerParams(
            dimension_semantics=("arbitrary", "arbitrary"), vmem_limit_bytes=VMEM_LIMIT),
        name="memkv",
    )(mem2d, g, wk, wv, *flat)
    casted = [o.reshape(w.shape) for o, w in zip(outs[4:], to_cast)]
    return outs[0], outs[1], outs[2], outs[3], casted


MIX_PARAMS = ("g_mix", "w_in", "wa", "wx", "vec")
VEC_ROWS = (("w4", RG_CONV_W), ("b4", None), ("ba", None), ("bx", None), ("lam", None),
            ("w31", CF_CONV_W), ("b31", None), ("ln_g", None), ("ln_b", None))


def _unpack_vec(vec_ref):
    views, row = {}, 0
    for name, count in VEC_ROWS:
        views[name] = vec_ref.at[row] if count is None else vec_ref.at[row:row + count]
        row += count or 1
    return views


def _gate_products(lhs, wa_ref, wx_ref, p_s, q_s):
    n_tiles, width = wa_ref.shape[0], wa_ref.shape[1]
    for j in range(n_tiles):
        cols = slice(j * width, (j + 1) * width)
        p_s[:, cols] = _dot(lhs[:, cols], wa_ref[j])
        q_s[:, cols] = _dot(lhs[:, cols], wx_ref[j])
BLOCK_PARAMS = ("w_out", "g_attn", "w_q", "w_o", "g_ffn", "w_gate", "w_up", "w_down")


def _group_rows(gi, offset=0):
    return pl.ds(gi * SUBLANES + offset, SUBLANES)


def _mix_stage(x_ref, prm, xr_hist, c_hist, hcar, gr_s, xc_s, p_s, q_s, yc_s, xn_s, cat_s, *, tile, c_w):
    win_ref = prm["w_in"]
    n_groups = tile // SUBLANES
    sub8 = lax.broadcasted_iota(jnp.int32, (SUBLANES, c_w), 0)

    yield 2 * _mxu_cost(tile, win_ref.shape[0], c_w), 2 * tile
    xn_s[...] = _rms(x_ref[...], prm["g_mix"][...]).astype(BF16)
    c_hist[pl.ds(CF_HIST, tile), :] = (
        _dot(xn_s[...], win_ref[:, 2 * c_w:3 * c_w]) * _sigmoid(_dot(xn_s[...], win_ref[:, 3 * c_w:4 * c_w])))
    yield 2 * _mxu_cost(tile, win_ref.shape[0], c_w), tile // 4
    xr_hist[pl.ds(RG_HIST, tile), :] = _dot(xn_s[...], win_ref[:, 0:c_w])
    gr_s[...] = _dot(xn_s[...], win_ref[:, c_w:2 * c_w])

    first_tap_off = CF_HIST - (CF_CONV_W - 1)
    n_hist_groups = CF_HIST // SUBLANES + 1
    w31_ref = prm["w31"]
    for lane0 in range(0, c_w, CONV_LANES):
        lanes = pl.ds(lane0, CONV_LANES)
        sub_l = lax.broadcasted_iota(jnp.int32, (SUBLANES, CONV_LANES), 0)
        zero = jnp.zeros((SUBLANES, CONV_LANES), F32)
        p0_prev, rolled_prev = zero, (zero,) * (SUBLANES - 1)
        for si in range(n_groups + 1):
            if si % CONV_GROUPS_PER_SEGMENT == 0:
                yield 0, 26 * (CONV_LANES // 128) * min(CONV_GROUPS_PER_SEGMENT, n_groups + 1 - si)
            hist = [c_hist[_group_rows(si, SUBLANES * q), lanes] for q in range(n_hist_groups)]
            parts = []
            for res in range(SUBLANES):
                part = None
                for q in range(n_hist_groups):
                    k = SUBLANES * q + res - first_tap_off
                    if 0 <= k < CF_CONV_W:
                        term = hist[q] * w31_ref[k, :, lanes]
                        part = term if part is None else part + term
                parts.append(part)
            rolled = tuple(pltpu.roll(parts[res], SUBLANES - res, axis=0) for res in range(1, SUBLANES))
            y = p0_prev
            for res in range(1, SUBLANES):
                y = y + jnp.where(sub_l < SUBLANES - res, rolled_prev[res - 1], rolled[res - 1])
            yc_s[_group_rows(si), lanes] = y
            p0_prev, rolled_prev = parts[0], rolled

    yield 0, 20 * n_groups
    w4_ref = prm["w4"]
    x_hist = xr_hist[pl.ds(0, RG_HIST), :]
    prev_rolled = [pltpu.roll(x_hist, dly, axis=0) for dly in range(1, RG_CONV_W)]
    for gi in range(n_groups):
        xg = xr_hist[_group_rows(gi, RG_HIST), :]
        acc = xg * w4_ref[RG_CONV_W - 1] + prm["b4"][...]
        rolled = []
        for dly in range(1, RG_CONV_W):
            rolled.append(pltpu.roll(xg, dly, axis=0))
            acc = acc + jnp.where(sub8 >= dly, rolled[-1], prev_rolled[dly - 1]) * w4_ref[RG_CONV_W - 1 - dly]
        xc_s[_group_rows(gi), :] = acc
        prev_rolled = rolled

    n_gate_tiles, gate_width = prm["wa"].shape[0], prm["wa"].shape[1]
    yield 2 * n_gate_tiles * _mxu_cost(tile, gate_width, gate_width), tile // 4
    xcb_s = xn_s.at[:, 0:c_w]
    xcb_s[...] = xc_s[...].astype(BF16)
    _gate_products(xcb_s, prm["wa"], prm["wx"], p_s, q_s)

    half_nsp = (-0.5 * LRU_C) * _softplus(-prm["lam"][...])
    n_grp = CHUNK // SUBLANES
    grp = (n_grp, SUBLANES, c_w)
    sub = lax.broadcasted_iota(jnp.int32, grp, 1)
    for ci in range(tile // CHUNK):
        if ci % 2 == 0:
            yield 0, 240
        rows = pl.ds(ci * CHUNK, CHUNK)
        a, b = _lru_coeffs(p_s[rows, :].reshape(grp), q_s[rows, :].reshape(grp),
                           xc_s[rows, :].reshape(grp), prm["ba"][...], prm["bx"][...], half_nsp)
        for dly in (1, 2, 4):
            ra = pltpu.roll(a, dly, axis=1)
            rb = pltpu.roll(b, dly, axis=1)
            keep = sub >= dly
            b = a * jnp.where(keep, rb, 0.0) + b
            a = a * jnp.where(keep, ra, 1.0)
        p_s[rows, :] = a.reshape(CHUNK, c_w)
        q_s[rows, :] = b.reshape(CHUNK, c_w)

    yield 0, 21 * n_groups
    hb = hcar[...]
    for gi in range(n_groups):
        rows = _group_rows(gi)
        h = p_s[rows, :] * hb + q_s[rows, :]
        q_s[rows, :] = h
        hb = jnp.broadcast_to(h[SUBLANES - 1:SUBLANES, :], (SUBLANES, c_w))
    hcar[...] = hb

    for ci in range(tile // OUT_CHUNK):
        yield 0, 220
        rows = pl.ds(ci * OUT_CHUNK, OUT_CHUNK)
        y_r = q_s[rows, :] * _gelu_tanh(gr_s[rows, :])
        cat_s[rows, 0:c_w] = y_r.astype(BF16)
        conv = yc_s[pl.ds(ci * OUT_CHUNK + SUBLANES, OUT_CHUNK), :].reshape(
            OUT_CHUNK // SUBLANES, SUBLANES, c_w)
        y_c = _ln_silu(conv + prm["b31"][...], prm["ln_g"][...], prm["ln_b"][...])
        cat_s[rows, c_w:2 * c_w] = y_c.reshape(OUT_CHUNK, c_w).astype(BF16)


def _block_stage(x_ref, cat_s, k_ref, v_ref, prm, gf_ref, xo_ref, xb_s, zb_s, qb_s, o_s, h_s,
                 *, heads, final_norm):
    d = x_ref.shape[1]
    head_dim = d // heads
    rows = x_ref.shape[0]
    yield _mxu_cost(rows, d, d), rows // 2
    xb_s[...] = x_ref[...] + _dot(cat_s[...], prm["w_out"][...])
    yield _mxu_cost(rows, d, d), 2 * rows
    zb_s[...] = _rms(xb_s[...], prm["g_attn"][...]).astype(BF16)
    qb_s[...] = (_dot(zb_s[...], prm["w_q"][...]) * _query_scale(head_dim)).astype(BF16)
    for h in range(heads):
        yield rows // 2, rows
        sl = slice(h * head_dim, (h + 1) * head_dim)
        (o,) = _attend(qb_s[:, sl], k_ref[:, sl], v_ref[:, sl], 1, head_dim)
        o_s[:, sl] = o.astype(BF16)
    yield _mxu_cost(rows, d, d), rows // 2
    xb_s[...] = xb_s[...] + _dot(o_s[...], prm["w_o"][...])
    yield 0, rows
    zb_s[...] = _rms(xb_s[...], prm["g_ffn"][...]).astype(BF16)
    wg_ref, wu_ref, wd_ref = prm["w_gate"], prm["w_up"], prm["w_down"]
    for lo, hi in _ffn_cols(wg_ref.shape[1]):
        yield 2 * _mxu_cost(rows, d, hi - lo), (hi - lo) // 2
        h_s[:, lo:hi] = (_silu(_dot(zb_s[...], wg_ref[:, lo:hi])) * _dot(zb_s[...], wu_ref[:, lo:hi])
                         ).astype(BF16)
    for lo, hi in _ffn_cols(d):
        yield _mxu_cost(rows, wd_ref.shape[0], hi - lo), rows // 2
        xo_ref[:, lo:hi] = xb_s[:, lo:hi] + _dot(h_s[...], wd_ref[:, lo:hi])
    if final_norm:
        yield 0, rows
        xo_ref[...] = _rms(xo_ref[...], gf_ref[...])


def _interleave(*stages):
    pending = {stage: next(stage) for stage in stages}
    clock = {stage: i for i, stage in enumerate(stages)}
    while pending:
        stage = min(pending, key=clock.get)
        clock[stage] += max(pending.pop(stage))
        try:
            pending[stage] = next(stage)
        except StopIteration:
            pass


def _layer_prompt_kernel(*refs, tile, tiles_per_seq, c_w, heads, final_norm, pad_t):
    n_mix, n_blk = len(MIX_PARAMS), len(BLOCK_PARAMS)
    xa_ref, xb_ref, k_ref, v_ref, sq_ref, sk_ref, sv_ref = refs[:7]
    refs = refs[7:]
    mix_prm = dict(zip(MIX_PARAMS, refs[:n_mix]))
    mix_prm.update(_unpack_vec(mix_prm["vec"]))
    blk_prm = dict(zip(BLOCK_PARAMS, refs[n_mix:n_mix + n_blk]))
    gf_ref = refs[n_mix + n_blk]
    xo_ref, h_ref, rg_ref, cf_ref, so_ref = refs[1 + n_mix + n_blk:6 + n_mix + n_blk]
    (xr_hist, c_hist, hcar, gr_s, xc_s, p_s, q_s, yc_s, xn_s, cat_s,
     xb_s, zb_s, qb_s, o_s, h_s) = refs[6 + n_mix + n_blk:]

    g = pl.program_id(0)
    pos = g % tiles_per_seq

    @pl.when(g == 0)
    def _():
        cat_s[...] = jnp.zeros(cat_s.shape, BF16)
        c_hist[pl.ds(CF_HIST + tile, SUBLANES), :] = jnp.zeros((SUBLANES, c_w), F32)

    @pl.when(pos == 0)
    def _():
        xr_hist[pl.ds(0, RG_HIST), :] = jnp.zeros((RG_HIST, c_w), F32)
        c_hist[pl.ds(0, CF_HIST), :] = jnp.zeros((CF_HIST, c_w), F32)
        hcar[...] = jnp.zeros((SUBLANES, c_w), F32)

    _interleave(
        _block_stage(xb_ref, cat_s, k_ref, v_ref, blk_prm, gf_ref, xo_ref, xb_s, zb_s, qb_s, o_s, h_s,
                     heads=heads, final_norm=final_norm),
        _mix_stage(xa_ref, mix_prm, xr_hist, c_hist, hcar, gr_s, xc_s, p_s, q_s, yc_s, xn_s, cat_s,
                   tile=tile, c_w=c_w),
        _sample_attn_stage(sq_ref, sk_ref, sv_ref, so_ref, heads=heads, pad_t=pad_t))

    @pl.when(pos == tiles_per_seq - 1)
    def _():
        h_ref[...] = hcar[0:1, :]
        rg_ref[...] = xr_hist[pl.ds(RG_HIST + tile - (RG_CONV_W - 1), RG_CONV_W - 1), :]
        cf_ref[...] = c_hist[pl.ds(CF_HIST + tile - (CF_CONV_W - 1), CF_CONV_W - 1), :]

    xr_hist[pl.ds(0, RG_HIST), :] = xr_hist[pl.ds(tile, RG_HIST), :]
    c_hist[pl.ds(0, CF_HIST), :] = c_hist[pl.ds(tile, CF_HIST), :]


def _layer_prompt(x2d, kb, vb, sq_pad, cache_k, cache_v, lw, gf, layer,
                  *, batch, seq, n_mem, heads, final_norm, pad_t):
    rows, d = x2d.shape
    tile = PROMPT_TILE
    c_w = lw["vec"].shape[-1]
    d_ff = lw["w_gate"].shape[-1]
    tps = seq // tile
    n_tiles = batch * tps
    dec_batch = cache_k.shape[1]
    nb = dec_batch // n_tiles
    assert nb * n_tiles == dec_batch and sq_pad.shape[0] == dec_batch * pad_t

    def mix_tile(g):
        return jnp.minimum(g, n_tiles - 1)

    def blk_tile(g):
        return jnp.maximum(g - 1, 0)

    kv_spec = pl.BlockSpec((None, n_mem, d), lambda g: (layer, blk_tile(g) // tps, 0))
    sq_spec = pl.BlockSpec((nb * pad_t, d), lambda g: (mix_tile(g), 0))
    cache_spec = pl.BlockSpec((None, nb, n_mem, heads, d // heads), lambda g: (layer, mix_tile(g), 0, 0, 0))
    params = [lw[name] for name in MIX_PARAMS + BLOCK_PARAMS]
    kern = functools.partial(_layer_prompt_kernel, tile=tile, tiles_per_seq=tps, c_w=c_w,
                             heads=heads, final_norm=final_norm, pad_t=pad_t)
    return pl.pallas_call(
        kern,
        grid=(n_tiles + 1,),
        in_specs=[
            pl.BlockSpec((tile, d), lambda g: (mix_tile(g), 0)),
            pl.BlockSpec((tile, d), lambda g: (blk_tile(g), 0)),
            kv_spec, kv_spec,
            sq_spec, cache_spec, cache_spec,
            *[_layer_spec(p, layer) for p in params],
            _full(gf.shape),
        ],
        out_specs=[
            pl.BlockSpec((tile, d), lambda g: (blk_tile(g), 0)),
            pl.BlockSpec((None, 1, c_w), lambda g: (mix_tile(g) // tps, 0, 0)),
            pl.BlockSpec((None, RG_CONV_W - 1, c_w), lambda g: (mix_tile(g) // tps, 0, 0)),
            pl.BlockSpec((None, CF_CONV_W - 1, c_w), lambda g: (mix_tile(g) // tps, 0, 0)),
            sq_spec,
        ],
        out_shape=[
            jax.ShapeDtypeStruct((rows, d), F32),
            jax.ShapeDtypeStruct((batch, 1, c_w), F32),
            jax.ShapeDtypeStruct((batch, RG_CONV_W - 1, c_w), F32),
            jax.ShapeDtypeStruct((batch, CF_CONV_W - 1, c_w), F32),
            jax.ShapeDtypeStruct(sq_pad.shape, F32),
        ],
        scratch_shapes=[
            pltpu.VMEM((RG_HIST + tile, c_w), F32),
            pltpu.VMEM((CF_HIST + tile + SUBLANES, c_w), F32),
            pltpu.VMEM((SUBLANES, c_w), F32),
            pltpu.VMEM((tile, c_w), F32),
            pltpu.VMEM((tile, c_w), F32),
            pltpu.VMEM((tile, c_w), F32),
            pltpu.VMEM((tile, c_w), F32),
            pltpu.VMEM((tile + SUBLANES, c_w), F32),
            pltpu.VMEM((tile, d), BF16),
            pltpu.VMEM((tile, 2 * c_w), BF16),
            pltpu.VMEM((tile, d), F32),
            pltpu.VMEM((tile, d), BF16),
            pltpu.VMEM((tile, d), BF16),
            pltpu.VMEM((tile, d), BF16),
            pltpu.VMEM((tile, d_ff), BF16),
        ],
        compiler_params=pltpu.CompilerParams(
            dimension_semantics=("arbitrary",), vmem_limit_bytes=VMEM_LIMIT),
        name="layer_prompt",
    )(x2d, x2d, kb, vb, sq_pad, cache_k, cache_v, *params, gf)


def _mix_sample_kernel(x_ref, h0_ref, rb_ref, cb_ref, g_ref, win_ref, wa_ref, wx_ref, vec_ref, wout_ref,
                       ga_ref, wq_ref,
                       xo_ref, h_ref, xr_ref, c_ref, qo_ref,
                       gr_s, xc_s, p_s, q_s, cat_s, *, steps, batch, d_rnn, heads):
    c_w = d_rnn
    vec = _unpack_vec(vec_ref)
    w4_ref, b4_ref, ba_ref, bx_ref, lam_ref = vec["w4"], vec["b4"], vec["ba"], vec["bx"], vec["lam"]
    w31_ref, b31_ref, lng_ref, lnb_ref = vec["w31"], vec["b31"], vec["ln_g"], vec["ln_b"]
    xn = _rms(x_ref[...], g_ref[...]).astype(BF16)
    xr_ref[...] = _dot(xn, win_ref[:, 0:c_w])
    gr_s[...] = _dot(xn, win_ref[:, c_w:2 * c_w])
    c_ref[...] = (
        _dot(xn, win_ref[:, 2 * c_w:3 * c_w]) * _sigmoid(_dot(xn, win_ref[:, 3 * c_w:4 * c_w])))

    def rows_of(t, s):
        return pl.ds(pl.multiple_of(t * batch + s, SUBLANES), SUBLANES)

    def conv4_blk(bi, carry):
        s = bi * SUBLANES
        for t in range(steps):
            acc = b4_ref[...]
            for k in range(RG_CONV_W):
                jj = t + k
                if jj < RG_CONV_W - 1:
                    src = rb_ref[jj, pl.ds(pl.multiple_of(s, SUBLANES), SUBLANES), :]
                else:
                    src = xr_ref[rows_of(jj - (RG_CONV_W - 1), s), :]
                acc = acc + src * w4_ref[k]
            xc_s[rows_of(t, s), :] = acc
        return carry

    lax.fori_loop(0, batch // SUBLANES, conv4_blk, 0)

    _gate_products(xc_s[...].astype(BF16), wa_ref, wx_ref, p_s, q_s)
    half_nsp = (-0.5 * LRU_C) * _softplus(-lam_ref[...])

    def scan_blk(bi, carry):
        s = bi * SUBLANES
        b_rows = pl.ds(pl.multiple_of(s, SUBLANES), SUBLANES)
        h = h0_ref[b_rows, :]
        for t in range(steps):
            rows = rows_of(t, s)
            a, b = _lru_coeffs(p_s[rows, :], q_s[rows, :], xc_s[rows, :], ba_ref[...], bx_ref[...], half_nsp)
            h = a * h + b
            cat_s[rows, 0:c_w] = h * _gelu_tanh(gr_s[rows, :])
        h_ref[b_rows, :] = h
        return carry

    lax.fori_loop(0, batch // SUBLANES, scan_blk, 0)

    def conv31_blk(bi, carry):
        s = bi * SUBLANES
        for t in range(steps):
            acc = b31_ref[...]
            for k in range(CF_CONV_W):
                jj = t + k
                if jj < CF_CONV_W - 1:
                    src = cb_ref[jj, pl.ds(pl.multiple_of(s, SUBLANES), SUBLANES), :]
                else:
                    src = c_ref[rows_of(jj - (CF_CONV_W - 1), s), :]
                acc = acc + src * w31_ref[k]
            cat_s[rows_of(t, s), c_w:2 * c_w] = _ln_silu(acc, lng_ref[...], lnb_ref[...])
        return carry

    lax.fori_loop(0, batch // SUBLANES, conv31_blk, 0)

    xo = x_ref[...] + _dot(cat_s[...].astype(BF16), wout_ref[...])
    xo_ref[...] = xo
    head_dim = xo.shape[1] // heads
    qo_ref[...] = _dot(_rms(xo, ga_ref[...]).astype(BF16), wq_ref[...]) * _query_scale(head_dim)


def _mix_sample(x_tm, h0, rb_tm, cb_tm, lw, layer, *, steps, batch, heads):
    rows, d = x_tm.shape
    c_w = lw["vec"].shape[-1]
    kern = functools.partial(_mix_sample_kernel, steps=steps, batch=batch, d_rnn=c_w, heads=heads)
    params = [lw[name] for name in MIX_PARAMS + ("w_out", "g_attn", "w_q")]
    return pl.pallas_call(
        kern,
        grid=(1,),
        in_specs=[_full(x_tm.shape), _layer_spec(h0, layer), _layer_spec(rb_tm, layer),
                  _layer_spec(cb_tm, layer),
                  *[_layer_spec(p, layer) for p in params]],
        out_specs=[_full((rows, d)), _full((batch, c_w)), _full((rows, c_w)), _full((rows, c_w)),
                   _full((rows, d))],
        out_shape=[
            jax.ShapeDtypeStruct((rows, d), F32),
            jax.ShapeDtypeStruct((batch, c_w), F32),
            jax.ShapeDtypeStruct((rows, c_w), F32),
            jax.ShapeDtypeStruct((rows, c_w), F32),
            jax.ShapeDtypeStruct((rows, d), F32),
        ],
        scratch_shapes=[
            pltpu.VMEM((rows, c_w), F32),
            pltpu.VMEM((rows, c_w), F32),
            pltpu.VMEM((rows, c_w), F32),
            pltpu.VMEM((rows, c_w), F32),
            pltpu.VMEM((rows, 2 * c_w), F32),
        ],
        compiler_params=pltpu.CompilerParams(
            dimension_semantics=("arbitrary",), vmem_limit_bytes=VMEM_LIMIT),
        name="mix_sample",
    )(x_tm, h0, rb_tm, cb_tm, *params)


def _sample_attn_stage(q_ref, k_ref, v_ref, o_ref, *, heads, pad_t):
    nb, n_mem, _, head_dim = k_ref.shape
    n_kv = n_mem * heads
    col_head = lax.broadcasted_iota(jnp.int32, (heads * pad_t, n_kv), 1) % heads
    row_head = lax.broadcasted_iota(jnp.int32, (heads * pad_t, n_kv), 0) // pad_t
    own_head = col_head == row_head
    for bb in range(nb):
        yield 2 * _mxu_cost(heads * pad_t, head_dim, n_kv) + n_kv // 2, n_kv // 2
        rows = pl.ds(bb * pad_t, pad_t)
        q = q_ref[rows, :]
        qs = jnp.concatenate([q[:, h * head_dim:(h + 1) * head_dim] for h in range(heads)], axis=0)
        k2 = k_ref[bb].reshape(n_kv, head_dim).astype(BF16)
        v2 = v_ref[bb].reshape(n_kv, head_dim).astype(BF16)
        s = lax.dot_general(qs.astype(BF16), k2, (((1,), (1,)), ((), ())), preferred_element_type=F32)
        s = jnp.where(own_head, s, -1e30)
        p = jnp.exp2(s - jnp.max(s, axis=-1, keepdims=True))
        inv = 1.0 / jnp.sum(p, axis=-1, keepdims=True)
        o = _dot(p.astype(BF16), v2) * inv
        for h in range(heads):
            o_ref[rows, h * head_dim:(h + 1) * head_dim] = o[h * pad_t:(h + 1) * pad_t, :]


def _ffn_kernel(x_ref, o_ref, wo_ref, g_ref, wg_ref, wu_ref, wd_ref, gf_ref, xo_ref, z_s, acc_s, *, final_norm):
    c = pl.program_id(0)

    @pl.when(c == 0)
    def _():
        x = x_ref[...] + _dot(o_ref[...].astype(BF16), wo_ref[...])
        z_s[...] = _rms(x, g_ref[...]).astype(BF16)
        acc_s[...] = x

    h = (_silu(_dot(z_s[...], wg_ref[...])) * _dot(z_s[...], wu_ref[...])).astype(BF16)
    acc_s[...] += _dot(h, wd_ref[...])

    @pl.when(c == pl.num_programs(0) - 1)
    def _():
        y = acc_s[...]
        if final_norm:
            y = _rms(y, gf_ref[...])
        xo_ref[...] = y


def _ffn(x2d, o2d, lw, gf, layer, *, final_norm):
    rows, d = x2d.shape
    d_ff = lw["w_gate"].shape[-1]
    chunk = MXU_DIM
    assert d_ff % chunk == 0
    kern = functools.partial(_ffn_kernel, final_norm=final_norm)
    return pl.pallas_call(
        kern,
        grid=(d_ff // chunk,),
        in_specs=[_full((rows, d)), _full((rows, d)), _layer_spec(lw["w_o"], layer),
                  _layer_spec(lw["g_ffn"], layer),
                  pl.BlockSpec((None, d, chunk), lambda c: (layer, 0, c)),
                  pl.BlockSpec((None, d, chunk), lambda c: (layer, 0, c)),
                  pl.BlockSpec((None, chunk, d), lambda c: (layer, c, 0)),
                  _full(gf.shape)],
        out_specs=_full((rows, d)),
        out_shape=jax.ShapeDtypeStruct((rows, d), F32),
        scratch_shapes=[pltpu.VMEM((rows, d), BF16), pltpu.VMEM((rows, d), F32)],
        compiler_params=pltpu.CompilerParams(
            dimension_semantics=("arbitrary",), vmem_limit_bytes=VMEM_LIMIT),
        name="ffn",
    )(x2d, o2d, lw["w_o"], lw["g_ffn"], lw["w_gate"], lw["w_up"], lw["w_down"], gf)


def _rows8(v):
    return jnp.broadcast_to(v[..., None, :], v.shape[:-1] + (SUBLANES, v.shape[-1]))


def _block_diag_tiles(w, width):
    l, h, i, _ = w.shape
    per_tile = width // i
    eye = jnp.eye(per_tile, dtype=w.dtype)
    tiles = jnp.einsum("lthij,hg->lthigj", w.reshape(l, h // per_tile, per_tile, i, i), eye)
    return tiles.reshape(l, h // per_tile, width, width)


def kernel(x_prompt, x_sample, state_rglru_h, state_rglru_conv, state_conf_conv, cache_mem_k, cache_mem_v, mem_prompt, norm_mix_g, w_in, rg_conv_w, rg_conv_b, rg_wa, rg_ba, rg_wx, rg_bx, rg_lambda, cf_conv_w, cf_conv_b, cf_ln_g, cf_ln_b, w_out, norm_attn_g, norm_mem_g, w_q, w_k, w_v, w_o, norm_ffn_g, w_gate, w_up, w_down, norm_final_g):
    batch, seq, d = x_prompt.shape
    dec_batch, dec_seq, _ = x_sample.shape
    depth = w_in.shape[0]
    n_mem = mem_prompt.shape[1]
    heads = cache_mem_k.shape[3]
    pad_t = SUBLANES

    lw = {
        "g_mix": norm_mix_g[:, None, :],
        "wa": _block_diag_tiles(rg_wa, MXU_DIM).astype(BF16),
        "wx": _block_diag_tiles(rg_wx, MXU_DIM).astype(BF16),
        "vec": _rows8(jnp.concatenate(
            [v if v.ndim == 3 else v[:, None, :] for v in
             (rg_conv_w, rg_conv_b, rg_ba, rg_bx, rg_lambda, cf_conv_w, cf_conv_b, cf_ln_g, cf_ln_b)],
            axis=1)),
        "g_attn": norm_attn_g[:, None, :],
        "g_ffn": norm_ffn_g[:, None, :],
    }
    gf = norm_final_g[None, :]

    big = {"w_in": w_in, "w_out": w_out, "w_q": w_q, "w_o": w_o,
           "w_gate": w_gate, "w_up": w_up, "w_down": w_down}
    mem_k, mem_v, mem_kb, mem_vb, big_bf16 = _memkv(
        mem_prompt.reshape(batch * n_mem, d), norm_mem_g[:, None, :],
        w_k, w_v, tuple(big.values()), tile=n_mem, n_mem=n_mem, heads=heads)
    lw.update(zip(big, big_bf16))

    rb_tm = jnp.transpose(state_rglru_conv, (0, 2, 1, 3))
    cb_tm = jnp.transpose(state_conf_conv, (0, 2, 1, 3))

    xp = x_prompt.reshape(batch * seq, d)
    xs_tm = jnp.transpose(x_sample, (1, 0, 2)).reshape(dec_seq * dec_batch, d)
    p_h, p_rg, p_cf, s_h, s_xr, s_c = [], [], [], [], [], []
    for l in range(depth):
        last = l == depth - 1

        xs_tm, hs, xr_tm, c_tm, sq_tm = _mix_sample(xs_tm, state_rglru_h, rb_tm, cb_tm, lw, l,
                                                    steps=dec_seq, batch=dec_batch, heads=heads)
        s_h.append(hs); s_xr.append(xr_tm); s_c.append(c_tm)
        sq_pad = jnp.pad(jnp.transpose(sq_tm.reshape(dec_seq, dec_batch, d), (1, 0, 2)),
                         ((0, 0), (0, pad_t - dec_seq), (0, 0))).reshape(dec_batch * pad_t, d)

        xp, hp, rgp, cfp, so_pad = _layer_prompt(
            xp, mem_kb, mem_vb, sq_pad, cache_mem_k, cache_mem_v, lw, gf, l, batch=batch, seq=seq,
            n_mem=n_mem, heads=heads, final_norm=last, pad_t=pad_t)
        p_h.append(hp[:, 0, :]); p_rg.append(rgp); p_cf.append(cfp)

        so_tm = jnp.transpose(so_pad.reshape(dec_batch, pad_t, d)[:, :dec_seq], (1, 0, 2)).reshape(
            dec_seq * dec_batch, d)
        xs_tm = _ffn(xs_tm, so_tm, lw, gf, l, final_norm=last)

    def new_conv_state(old, fresh_tm, width):
        fresh = jnp.transpose(jnp.stack(fresh_tm).reshape(depth, dec_seq, dec_batch, -1), (0, 2, 1, 3))
        return jnp.concatenate([old, fresh], axis=2)[:, :, -(width - 1):]

    y_sample = jnp.transpose(xs_tm.reshape(dec_seq, dec_batch, d), (1, 0, 2))
    return (xp.reshape(batch, seq, d), y_sample,
            jnp.stack(p_h), jnp.stack(p_rg), jnp.stack(p_cf),
            mem_k, mem_v,
            jnp.stack(s_h),
            new_conv_state(state_rglru_conv, s_xr, RG_CONV_W),
            new_conv_state(state_conf_conv, s_c, CF_CONV_W))
```

```python
import functools
import math

import jax
import jax.numpy as jnp
from jax import lax
from jax.experimental import pallas as pl
from jax.experimental.pallas import tpu as pltpu

F32 = jnp.float32
BF16 = jnp.bfloat16

EPS = 1e-6
LRU_C = 8.0
RG_CONV_W = 4
CF_CONV_W = 31
SUBLANES = 8
MXU_DIM = 256
CF_HIST = 32
RG_HIST = SUBLANES
CHUNK = 16
OUT_CHUNK = 32
CONV_LANES = 256
CONV_GROUPS_PER_SEGMENT = 6
FFN_COLS = 512
PROMPT_TILE = 256
VMEM_LIMIT = 56 * 1024 * 1024


def _dot(a, b):
    return jnp.dot(a, b, preferred_element_type=F32)


def _rms(x, g):
    return x * lax.rsqrt(jnp.mean(x * x, axis=-1, keepdims=True) + EPS) * g


def _gelu_tanh(x):
    c = math.sqrt(2.0 / math.pi)
    return (0.5 * x) * (1.0 + jnp.tanh(x * (c + (c * 0.044715) * (x * x))))


def _sigmoid(x):
    return 0.5 * (jnp.tanh(0.5 * x) + 1.0)


def _silu(x):
    h = 0.5 * x
    return h * (jnp.tanh(h) + 1.0)


def _softplus(z):
    return jnp.maximum(z, 0.0) + jnp.log1p(jnp.exp(-jnp.abs(z)))


def _lru_coeffs(r_pre, i_pre, xc, ba, bx, half_nsp):
    log_a = (jnp.tanh(0.5 * (r_pre + ba)) + 1.0) * half_nsp
    i = _sigmoid(i_pre + bx)
    a = jnp.exp(log_a)
    t = jnp.tanh(log_a)
    mult = jnp.sqrt(-2.0 * t / (1.0 - t))
    return a, mult * (i * xc)


def _ln_silu(y, g, b):
    mu = jnp.mean(y, axis=-1, keepdims=True)
    yc = y - mu
    var = jnp.mean(yc * yc, axis=-1, keepdims=True)
    return _silu(yc * lax.rsqrt(var + EPS) * g + b)


def _query_scale(head_dim):
    return head_dim ** -0.5 * math.log2(math.e)


def _attend(q, k, v, heads, head_dim):
    outs = []
    for h in range(heads):
        sl = slice(h * head_dim, (h + 1) * head_dim)
        s = lax.dot_general(q[:, sl], k[:, sl], (((1,), (1,)), ((), ())), preferred_element_type=F32)
        p = jnp.exp2(s - jnp.max(s, axis=-1, keepdims=True))
        inv = 1.0 / jnp.sum(p, axis=-1, keepdims=True)
        outs.append(_dot(p.astype(BF16), v[:, sl]) * inv)
    return outs


def _ffn_cols(d_ff):
    return tuple((lo, min(lo + FFN_COLS, d_ff)) for lo in range(0, d_ff, FFN_COLS))


def _mxu_cost(rows, k, n):
    return (rows // 2) * pl.cdiv(k, MXU_DIM) * pl.cdiv(n, MXU_DIM) // 2


def _full(shape):
    return pl.BlockSpec(shape, lambda *_: (0,) * len(shape))


def _layer_spec(arr, layer):
    zeros = (0,) * (arr.ndim - 1)
    return pl.BlockSpec((None,) + arr.shape[1:], lambda *_: (layer,) + zeros, pipeline_mode=pl.Buffered(1))


def _memkv_kernel(mem_ref, g_ref, wk_ref, wv_ref, *refs, n_cast):
    cast_in = refs[:n_cast]
    k_ref, v_ref, kb_ref, vb_ref = refs[n_cast:n_cast + 4]
    cast_out = refs[n_cast + 4:]
    for src, dst in zip(cast_in, cast_out):
        dst[...] = src[...].astype(BF16)
    m = _rms(mem_ref[...], g_ref[...]).astype(BF16)
    k = _dot(m, wk_ref[...].astype(BF16))
    v = _dot(m, wv_ref[...].astype(BF16))
    k_ref[...] = k.reshape(k_ref.shape)
    v_ref[...] = v.reshape(v_ref.shape)
    kb_ref[...] = k.astype(BF16)
    vb_ref[...] = v.astype(BF16)


def _memkv(mem2d, g, wk, wv, to_cast, tile, n_mem, heads):
    depth, d, _ = wk.shape
    rows = mem2d.shape[0]
    n_i = rows // tile
    n_steps = depth * n_i
    row_spec = pl.BlockSpec((None, tile, d), lambda l, i: (l, i, 0))
    out5_spec = pl.BlockSpec((None, tile // n_mem, n_mem, heads, d // heads), lambda l, i: (l, i, 0, 0, 0))
    out5_shape = jax.ShapeDtypeStruct((depth, rows // n_mem, n_mem, heads, d // heads), F32)
    w_spec = pl.BlockSpec((None, d, d), lambda l, i: (l, 0, 0))
    flat = [w.reshape(-1, w.shape[-1]) for w in to_cast]
    slab_specs = [pl.BlockSpec((f.shape[0] // n_steps, f.shape[1]), lambda l, i: (l * n_i + i, 0)) for f in flat]
    assert all(f.shape[0] % (n_steps * 2 * SUBLANES) == 0 for f in flat)
    outs = pl.pallas_call(
        functools.partial(_memkv_kernel, n_cast=len(flat)),
        grid=(depth, n_i),
        in_specs=[
            pl.BlockSpec((tile, d), lambda l, i: (i, 0)),
            pl.BlockSpec((None, 1, d), lambda l, i: (l, 0, 0)),
            w_spec, w_spec,
            *slab_specs,
        ],
        out_specs=[out5_spec, out5_spec, row_spec, row_spec, *slab_specs],
        out_shape=[
            out5_shape,
            out5_shape,
            jax.ShapeDtypeStruct((depth, rows, d), BF16),
            jax.ShapeDtypeStruct((depth, rows, d), BF16),
            *[jax.ShapeDtypeStruct(f.shape, BF16) for f in flat],
        ],
        compiler_params=pltpu.CompilerParams(
            dimension_semantics=("arbitrary", "arbitrary"), vmem_limit_bytes=VMEM_LIMIT),
        name="memkv",
    )(mem2d, g, wk, wv, *flat)
    casted = [o.reshape(w.shape) for o, w in zip(outs[4:], to_cast)]
    return outs[0], outs[1], outs[2], outs[3], casted


MIX_PARAMS = ("g_mix", "w_in", "wa", "wx", "vec")
VEC_ROWS = (("w4", RG_CONV_W), ("b4", None), ("ba", None), ("bx", None), ("lam", None),
            ("w31", CF_CONV_W), ("b31", None), ("ln_g", None), ("ln_b", None))


def _unpack_vec(vec_ref):
    views, row = {}, 0
    for name, count in VEC_ROWS:
        views[name] = vec_ref.at[row] if count is None else vec_ref.at[row:row + count]
        row += count or 1
    return views


def _gate_products(lhs, wa_ref, wx_ref, p_s, q_s):
    n_tiles, width = wa_ref.shape[0], wa_ref.shape[1]
    for j in range(n_tiles):
        cols = slice(j * width, (j + 1) * width)
        p_s[:, cols] = _dot(lhs[:, cols], wa_ref[j])
        q_s[:, cols] = _dot(lhs[:, cols], wx_ref[j])
BLOCK_PARAMS = ("w_out", "g_attn", "w_q", "w_o", "g_ffn", "w_gate", "w_up", "w_down")


def _group_rows(gi, offset=0):
    return pl.ds(gi * SUBLANES + offset, SUBLANES)


def _mix_stage(x_ref, prm, xr_hist, c_hist, hcar, gr_s, xc_s, p_s, q_s, yc_s, xn_s, cat_s, *, tile, c_w):
    win_ref = prm["w_in"]
    n_groups = tile // SUBLANES
    sub8 = lax.broadcasted_iota(jnp.int32, (SUBLANES, c_w), 0)

    yield 2 * _mxu_cost(tile, win_ref.shape[0], c_w), 2 * tile
    xn_s[...] = _rms(x_ref[...], prm["g_mix"][...]).astype(BF16)
    c_hist[pl.ds(CF_HIST, tile), :] = (
        _dot(xn_s[...], win_ref[:, 2 * c_w:3 * c_w]) * _sigmoid(_dot(xn_s[...], win_ref[:, 3 * c_w:4 * c_w])))
    yield 2 * _mxu_cost(tile, win_ref.shape[0], c_w), tile // 4
    xr_hist[pl.ds(RG_HIST, tile), :] = _dot(xn_s[...], win_ref[:, 0:c_w])
    gr_s[...] = _dot(xn_s[...], win_ref[:, c_w:2 * c_w])

    first_tap_off = CF_HIST - (CF_CONV_W - 1)
    n_hist_groups = CF_HIST // SUBLANES + 1
    w31_ref = prm["w31"]
    for lane0 in range(0, c_w, CONV_LANES):
        lanes = pl.ds(lane0, CONV_LANES)
        sub_l = lax.broadcasted_iota(jnp.int32, (SUBLANES, CONV_LANES), 0)
        zero = jnp.zeros((SUBLANES, CONV_LANES), F32)
        p0_prev, rolled_prev = zero, (zero,) * (SUBLANES - 1)
        for si in range(n_groups + 1):
            if si % CONV_GROUPS_PER_SEGMENT == 0:
                yield 0, 26 * (CONV_LANES // 128) * min(CONV_GROUPS_PER_SEGMENT, n_groups + 1 - si)
            hist = [c_hist[_group_rows(si, SUBLANES * q), lanes] for q in range(n_hist_groups)]
            parts = []
            for res in range(SUBLANES):
                part = None
                for q in range(n_hist_groups):
                    k = SUBLANES * q + res - first_tap_off
                    if 0 <= k < CF_CONV_W:
                        term = hist[q] * w31_ref[k, :, lanes]
                        part = term if part is None else part + term
                parts.append(part)
            rolled = tuple(pltpu.roll(parts[res], SUBLANES - res, axis=0) for res in range(1, SUBLANES))
            y = p0_prev
            for res in range(1, SUBLANES):
                y = y + jnp.where(sub_l < SUBLANES - res, rolled_prev[res - 1], rolled[res - 1])
            yc_s[_group_rows(si), lanes] = y
            p0_prev, rolled_prev = parts[0], rolled

    yield 0, 20 * n_groups
    w4_ref = prm["w4"]
    x_hist = xr_hist[pl.ds(0, RG_HIST), :]
    prev_rolled = [pltpu.roll(x_hist, dly, axis=0) for dly in range(1, RG_CONV_W)]
    for gi in range(n_groups):
        xg = xr_hist[_group_rows(gi, RG_HIST), :]
        acc = xg * w4_ref[RG_CONV_W - 1] + prm["b4"][...]
        rolled = []
        for dly in range(1, RG_CONV_W):
            rolled.append(pltpu.roll(xg, dly, axis=0))
            acc = acc + jnp.where(sub8 >= dly, rolled[-1], prev_rolled[dly - 1]) * w4_ref[RG_CONV_W - 1 - dly]
        xc_s[_group_rows(gi), :] = acc
        prev_rolled = rolled

    n_gate_tiles, gate_width = prm["wa"].shape[0], prm["wa"].shape[1]
    yield 2 * n_gate_tiles * _mxu_cost(tile, gate_width, gate_width), tile // 4
    xcb_s = xn_s.at[:, 0:c_w]
    xcb_s[...] = xc_s[...].astype(BF16)
    _gate_products(xcb_s, prm["wa"], prm["wx"], p_s, q_s)

    half_nsp = (-0.5 * LRU_C) * _softplus(-prm["lam"][...])
    n_grp = CHUNK // SUBLANES
    grp = (n_grp, SUBLANES, c_w)
    sub = lax.broadcasted_iota(jnp.int32, grp, 1)
    for ci in range(tile // CHUNK):
        if ci % 2 == 0:
            yield 0, 240
        rows = pl.ds(ci * CHUNK, CHUNK)
        a, b = _lru_coeffs(p_s[rows, :].reshape(grp), q_s[rows, :].reshape(grp),
                           xc_s[rows, :].reshape(grp), prm["ba"][...], prm["bx"][...], half_nsp)
        for dly in (1, 2, 4):
            ra = pltpu.roll(a, dly, axis=1)
            rb = pltpu.roll(b, dly, axis=1)
            keep = sub >= dly
            b = a * jnp.where(keep, rb, 0.0) + b
            a = a * jnp.where(keep, ra, 1.0)
        p_s[rows, :] = a.reshape(CHUNK, c_w)
        q_s[rows, :] = b.reshape(CHUNK, c_w)

    yield 0, 21 * n_groups
    hb = hcar[...]
    for gi in range(n_groups):
        rows = _group_rows(gi)
        h = p_s[rows, :] * hb + q_s[rows, :]
        q_s[rows, :] = h
        hb = jnp.broadcast_to(h[SUBLANES - 1:SUBLANES, :], (SUBLANES, c_w))
    hcar[...] = hb

    for ci in range(tile // OUT_CHUNK):
        yield 0, 220
        rows = pl.ds(ci * OUT_CHUNK, OUT_CHUNK)
        y_r = q_s[rows, :] * _gelu_tanh(gr_s[rows, :])
        cat_s[rows, 0:c_w] = y_r.astype(BF16)
        conv = yc_s[pl.ds(ci * OUT_CHUNK + SUBLANES, OUT_CHUNK), :].reshape(
            OUT_CHUNK // SUBLANES, SUBLANES, c_w)
        y_c = _ln_silu(conv + prm["b31"][...], prm["ln_g"][...], prm["ln_b"][...])
        cat_s[rows, c_w:2 * c_w] = y_c.reshape(OUT_CHUNK, c_w).astype(BF16)


def _block_stage(x_ref, cat_s, k_ref, v_ref, prm, gf_ref, xo_ref, xb_s, zb_s, qb_s, o_s, h_s,
                 *, heads, final_norm):
    d = x_ref.shape[1]
    head_dim = d // heads
    rows = x_ref.shape[0]
    yield _mxu_cost(rows, d, d), rows // 2
    xb_s[...] = x_ref[...] + _dot(cat_s[...], prm["w_out"][...])
    yield _mxu_cost(rows, d, d), 2 * rows
    zb_s[...] = _rms(xb_s[...], prm["g_attn"][...]).astype(BF16)
    qb_s[...] = (_dot(zb_s[...], prm["w_q"][...]) * _query_scale(head_dim)).astype(BF16)
    for h in range(heads):
        yield rows // 2, rows
        sl = slice(h * head_dim, (h + 1) * head_dim)
        (o,) = _attend(qb_s[:, sl], k_ref[:, sl], v_ref[:, sl], 1, head_dim)
        o_s[:, sl] = o.astype(BF16)
    yield _mxu_cost(rows, d, d), rows // 2
    xb_s[...] = xb_s[...] + _dot(o_s[...], prm["w_o"][...])
    yield 0, rows
    zb_s[...] = _rms(xb_s[...], prm["g_ffn"][...]).astype(BF16)
    wg_ref, wu_ref, wd_ref = prm["w_gate"], prm["w_up"], prm["w_down"]
    for lo, hi in _ffn_cols(wg_ref.shape[1]):
        yield 2 * _mxu_cost(rows, d, hi - lo), (hi - lo) // 2
        h_s[:, lo:hi] = (_silu(_dot(zb_s[...], wg_ref[:, lo:hi])) * _dot(zb_s[...], wu_ref[:, lo:hi])
                         ).astype(BF16)
    for lo, hi in _ffn_cols(d):
        yield _mxu_cost(rows, wd_ref.shape[0], hi - lo), rows // 2
        xo_ref[:, lo:hi] = xb_s[:, lo:hi] + _dot(h_s[...], wd_ref[:, lo:hi])
    if final_norm:
        yield 0, rows
        xo_ref[...] = _rms(xo_ref[...], gf_ref[...])


def _interleave(*stages):
    pending = {stage: next(stage) for stage in stages}
    clock = {stage: i for i, stage in enumerate(stages)}
    while pending:
        stage = min(pending, key=clock.get)
        clock[stage] += max(pending.pop(stage))
        try:
            pending[stage] = next(stage)
        except StopIteration:
            pass


def _layer_prompt_kernel(*refs, tile, tiles_per_seq, c_w, heads, final_norm, pad_t):
    n_mix, n_blk = len(MIX_PARAMS), len(BLOCK_PARAMS)
    xa_ref, xb_ref, k_ref, v_ref, sq_ref, sk_ref, sv_ref = refs[:7]
    refs = refs[7:]
    mix_prm = dict(zip(MIX_PARAMS, refs[:n_mix]))
    mix_prm.update(_unpack_vec(mix_prm["vec"]))
    blk_prm = dict(zip(BLOCK_PARAMS, refs[n_mix:n_mix + n_blk]))
    gf_ref = refs[n_mix + n_blk]
    xo_ref, h_ref, rg_ref, cf_ref, so_ref = refs[1 + n_mix + n_blk:6 + n_mix + n_blk]
    (xr_hist, c_hist, hcar, gr_s, xc_s, p_s, q_s, yc_s, xn_s, cat_s,
     xb_s, zb_s, qb_s, o_s, h_s) = refs[6 + n_mix + n_blk:]

    g = pl.program_id(0)
    pos = g % tiles_per_seq

    @pl.when(g == 0)
    def _():
        cat_s[...] = jnp.zeros(cat_s.shape, BF16)
        c_hist[pl.ds(CF_HIST + tile, SUBLANES), :] = jnp.zeros((SUBLANES, c_w), F32)

    @pl.when(pos == 0)
    def _():
        xr_hist[pl.ds(0, RG_HIST), :] = jnp.zeros((RG_HIST, c_w), F32)
        c_hist[pl.ds(0, CF_HIST), :] = jnp.zeros((CF_HIST, c_w), F32)
        hcar[...] = jnp.zeros((SUBLANES, c_w), F32)

    _interleave(
        _block_stage(xb_ref, cat_s, k_ref, v_ref, blk_prm, gf_ref, xo_ref, xb_s, zb_s, qb_s, o_s, h_s,
                     heads=heads, final_norm=final_norm),
        _mix_stage(xa_ref, mix_prm, xr_hist, c_hist, hcar, gr_s, xc_s, p_s, q_s, yc_s, xn_s, cat_s,
                   tile=tile, c_w=c_w),
        _sample_attn_stage(sq_ref, sk_ref, sv_ref, so_ref, heads=heads, pad_t=pad_t))

    @pl.when(pos == tiles_per_seq - 1)
    def _():
        h_ref[...] = hcar[0:1, :]
        rg_ref[...] = xr_hist[pl.ds(RG_HIST + tile - (RG_CONV_W - 1), RG_CONV_W - 1), :]
        cf_ref[...] = c_hist[pl.ds(CF_HIST + tile - (CF_CONV_W - 1), CF_CONV_W - 1), :]

    xr_hist[pl.ds(0, RG_HIST), :] = xr_hist[pl.ds(tile, RG_HIST), :]
    c_hist[pl.ds(0, CF_HIST), :] = c_hist[pl.ds(tile, CF_HIST), :]


def _layer_prompt(x2d, kb, vb, sq_pad, cache_k, cache_v, lw, gf, layer,
                  *, batch, seq, n_mem, heads, final_norm, pad_t):
    rows, d = x2d.shape
    tile = PROMPT_TILE
    c_w = lw["vec"].shape[-1]
    d_ff = lw["w_gate"].shape[-1]
    tps = seq // tile
    n_tiles = batch * tps
    dec_batch = cache_k.shape[1]
    nb = dec_batch // n_tiles
    assert nb * n_tiles == dec_batch and sq_pad.shape[0] == dec_batch * pad_t

    def mix_tile(g):
        return jnp.minimum(g, n_tiles - 1)

    def blk_tile(g):
        return jnp.maximum(g - 1, 0)

    kv_spec = pl.BlockSpec((None, n_mem, d), lambda g: (layer, blk_tile(g) // tps, 0))
    sq_spec = pl.BlockSpec((nb * pad_t, d), lambda g: (mix_tile(g), 0))
    cache_spec = pl.BlockSpec((None, nb, n_mem, heads, d // heads), lambda g: (layer, mix_tile(g), 0, 0, 0))
    params = [lw[name] for name in MIX_PARAMS + BLOCK_PARAMS]
    kern = functools.partial(_layer_prompt_kernel, tile=tile, tiles_per_seq=tps, c_w=c_w,
                             heads=heads, final_norm=final_norm, pad_t=pad_t)
    return pl.pallas_call(
        kern,
        grid=(n_tiles + 1,),
        in_specs=[
            pl.BlockSpec((tile, d), lambda g: (mix_tile(g), 0)),
            pl.BlockSpec((tile, d), lambda g: (blk_tile(g), 0)),
            kv_spec, kv_spec,
            sq_spec, cache_spec, cache_spec,
            *[_layer_spec(p, layer) for p in params],
            _full(gf.shape),
        ],
        out_specs=[
            pl.BlockSpec((tile, d), lambda g: (blk_tile(g), 0)),
            pl.BlockSpec((None, 1, c_w), lambda g: (mix_tile(g) // tps, 0, 0)),
            pl.BlockSpec((None, RG_CONV_W - 1, c_w), lambda g: (mix_tile(g) // tps, 0, 0)),
            pl.BlockSpec((None, CF_CONV_W - 1, c_w), lambda g: (mix_tile(g) // tps, 0, 0)),
            sq_spec,
        ],
        out_shape=[
            jax.ShapeDtypeStruct((rows, d), F32),
            jax.ShapeDtypeStruct((batch, 1, c_w), F32),
            jax.ShapeDtypeStruct((batch, RG_CONV_W - 1, c_w), F32),
            jax.ShapeDtypeStruct((batch, CF_CONV_W - 1, c_w), F32),
            jax.ShapeDtypeStruct(sq_pad.shape, F32),
        ],
        scratch_shapes=[
            pltpu.VMEM((RG_HIST + tile, c_w), F32),
            pltpu.VMEM((CF_HIST + tile + SUBLANES, c_w), F32),
            pltpu.VMEM((SUBLANES, c_w), F32),
            pltpu.VMEM((tile, c_w), F32),
            pltpu.VMEM((tile, c_w), F32),
            pltpu.VMEM((tile, c_w), F32),
            pltpu.VMEM((tile, c_w), F32),
            pltpu.VMEM((tile + SUBLANES, c_w), F32),
            pltpu.VMEM((tile, d), BF16),
            pltpu.VMEM((tile, 2 * c_w), BF16),
            pltpu.VMEM((tile, d), F32),
            pltpu.VMEM((tile, d), BF16),
            pltpu.VMEM((tile, d), BF16),
            pltpu.VMEM((tile, d), BF16),
            pltpu.VMEM((tile, d_ff), BF16),
        ],
        compiler_params=pltpu.CompilerParams(
            dimension_semantics=("arbitrary",), vmem_limit_bytes=VMEM_LIMIT),
        name="layer_prompt",
    )(x2d, x2d, kb, vb, sq_pad, cache_k, cache_v, *params, gf)


def _mix_sample_kernel(x_ref, h0_ref, rb_ref, cb_ref, g_ref, win_ref, wa_ref, wx_ref, vec_ref, wout_ref,
                       ga_ref, wq_ref,
                       xo_ref, h_ref, xr_ref, c_ref, qo_ref,
                       gr_s, xc_s, p_s, q_s, cat_s, *, steps, batch, d_rnn, heads):
    c_w = d_rnn
    vec = _unpack_vec(vec_ref)
    w4_ref, b4_ref, ba_ref, bx_ref, lam_ref = vec["w4"], vec["b4"], vec["ba"], vec["bx"], vec["lam"]
    w31_ref, b31_ref, lng_ref, lnb_ref = vec["w31"], vec["b31"], vec["ln_g"], vec["ln_b"]
    xn = _rms(x_ref[...], g_ref[...]).astype(BF16)
    xr_ref[...] = _dot(xn, win_ref[:, 0:c_w])
    gr_s[...] = _dot(xn, win_ref[:, c_w:2 * c_w])
    c_ref[...] = (
        _dot(xn, win_ref[:, 2 * c_w:3 * c_w]) * _sigmoid(_dot(xn, win_ref[:, 3 * c_w:4 * c_w])))

    def rows_of(t, s):
        return pl.ds(pl.multiple_of(t * batch + s, SUBLANES), SUBLANES)

    def conv4_blk(bi, carry):
        s = bi * SUBLANES
        for t in range(steps):
            acc = b4_ref[...]
            for k in range(RG_CONV_W):
                jj = t + k
                if jj < RG_CONV_W - 1:
                    src = rb_ref[jj, pl.ds(pl.multiple_of(s, SUBLANES), SUBLANES), :]
                else:
                    src = xr_ref[rows_of(jj - (RG_CONV_W - 1), s), :]
                acc = acc + src * w4_ref[k]
            xc_s[rows_of(t, s), :] = acc
        return carry

    lax.fori_loop(0, batch // SUBLANES, conv4_blk, 0)

    _gate_products(xc_s[...].astype(BF16), wa_ref, wx_ref, p_s, q_s)
    half_nsp = (-0.5 * LRU_C) * _softplus(-lam_ref[...])

    def scan_blk(bi, carry):
        s = bi * SUBLANES
        b_rows = pl.ds(pl.multiple_of(s, SUBLANES), SUBLANES)
        h = h0_ref[b_rows, :]
        for t in range(steps):
            rows = rows_of(t, s)
            a, b = _lru_coeffs(p_s[rows, :], q_s[rows, :], xc_s[rows, :], ba_ref[...], bx_ref[...], half_nsp)
            h = a * h + b
            cat_s[rows, 0:c_w] = h * _gelu_tanh(gr_s[rows, :])
        h_ref[b_rows, :] = h
        return carry

    lax.fori_loop(0, batch // SUBLANES, scan_blk, 0)

    def conv31_blk(bi, carry):
        s = bi * SUBLANES
        for t in range(steps):
            acc = b31_ref[...]
            for k in range(CF_CONV_W):
                jj = t + k
                if jj < CF_CONV_W - 1:
                    src = cb_ref[jj, pl.ds(pl.multiple_of(s, SUBLANES), SUBLANES), :]
                else:
                    src = c_ref[rows_of(jj - (CF_CONV_W - 1), s), :]
                acc = acc + src * w31_ref[k]
            cat_s[rows_of(t, s), c_w:2 * c_w] = _ln_silu(acc, lng_ref[...], lnb_ref[...])
        return carry

    lax.fori_loop(0, batch // SUBLANES, conv31_blk, 0)

    xo = x_ref[...] + _dot(cat_s[...].astype(BF16), wout_ref[...])
    xo_ref[...] = xo
    head_dim = xo.shape[1] // heads
    qo_ref[...] = _dot(_rms(xo, ga_ref[...]).astype(BF16), wq_ref[...]) * _query_scale(head_dim)


def _mix_sample(x_tm, h0, rb_tm, cb_tm, lw, layer, *, steps, batch, heads):
    rows, d = x_tm.shape
    c_w = lw["vec"].shape[-1]
    kern = functools.partial(_mix_sample_kernel, steps=steps, batch=batch, d_rnn=c_w, heads=heads)
    params = [lw[name] for name in MIX_PARAMS + ("w_out", "g_attn", "w_q")]
    return pl.pallas_call(
        kern,
        grid=(1,),
        in_specs=[_full(x_tm.shape), _layer_spec(h0, layer), _layer_spec(rb_tm, layer),
                  _layer_spec(cb_tm, layer),
                  *[_layer_spec(p, layer) for p in params]],
        out_specs=[_full((rows, d)), _full((batch, c_w)), _full((rows, c_w)), _full((rows, c_w)),
                   _full((rows, d))],
        out_shape=[
            jax.ShapeDtypeStruct((rows, d), F32),
            jax.ShapeDtypeStruct((batch, c_w), F32),
            jax.ShapeDtypeStruct((rows, c_w), F32),
            jax.ShapeDtypeStruct((rows, c_w), F32),
            jax.ShapeDtypeStruct((rows, d), F32),
        ],
        scratch_shapes=[
            pltpu.VMEM((rows, c_w), F32),
            pltpu.VMEM((rows, c_w), F32),
            pltpu.VMEM((rows, c_w), F32),
            pltpu.VMEM((rows, c_w), F32),
            pltpu.VMEM((rows, 2 * c_w), F32),
        ],
        compiler_params=pltpu.CompilerParams(
            dimension_semantics=("arbitrary",), vmem_limit_bytes=VMEM_LIMIT),
        name="mix_sample",
    )(x_tm, h0, rb_tm, cb_tm, *params)


def _sample_attn_stage(q_ref, k_ref, v_ref, o_ref, *, heads, pad_t):
    nb, n_mem, _, head_dim = k_ref.shape
    n_kv = n_mem * heads
    col_head = lax.broadcasted_iota(jnp.int32, (heads * pad_t, n_kv), 1) % heads
    row_head = lax.broadcasted_iota(jnp.int32, (heads * pad_t, n_kv), 0) // pad_t
    own_head = col_head == row_head
    for bb in range(nb):
        yield 2 * _mxu_cost(heads * pad_t, head_dim, n_kv) + n_kv // 2, n_kv // 2
        rows = pl.ds(bb * pad_t, pad_t)
        q = q_ref[rows, :]
        qs = jnp.concatenate([q[:, h * head_dim:(h + 1) * head_dim] for h in range(heads)], axis=0)
        k2 = k_ref[bb].reshape(n_kv, head_dim).astype(BF16)
        v2 = v_ref[bb].reshape(n_kv, head_dim).astype(BF16)
        s = lax.dot_general(qs.astype(BF16), k2, (((1,), (1,)), ((), ())), preferred_element_type=F32)
        s = jnp.where(own_head, s, -1e30)
        p = jnp.exp2(s - jnp.max(s, axis=-1, keepdims=True))
        inv = 1.0 / jnp.sum(p, axis=-1, keepdims=True)
        o = _dot(p.astype(BF16), v2) * inv
        for h in range(heads):
            o_ref[rows, h * head_dim:(h + 1) * head_dim] = o[h * pad_t:(h + 1) * pad_t, :]


def _ffn_kernel(x_ref, o_ref, wo_ref, g_ref, wg_ref, wu_ref, wd_ref, gf_ref, xo_ref, z_s, acc_s, *, final_norm):
    c = pl.program_id(0)

    @pl.when(c == 0)
    def _():
        x = x_ref[...] + _dot(o_ref[...].astype(BF16), wo_ref[...])
        z_s[...] = _rms(x, g_ref[...]).astype(BF16)
        acc_s[...] = x

    h = (_silu(_dot(z_s[...], wg_ref[...])) * _dot(z_s[...], wu_ref[...])).astype(BF16)
    acc_s[...] += _dot(h, wd_ref[...])

    @pl.when(c == pl.num_programs(0) - 1)
    def _():
        y = acc_s[...]
        if final_norm:
            y = _rms(y, gf_ref[...])
        xo_ref[...] = y


def _ffn(x2d, o2d, lw, gf, layer, *, final_norm):
    rows, d = x2d.shape
    d_ff = lw["w_gate"].shape[-1]
    chunk = d_ff // 2 if d_ff % (2 * 128) == 0 else MXU_DIM
    assert d_ff % chunk == 0
    kern = functools.partial(_ffn_kernel, final_norm=final_norm)
    return pl.pallas_call(
        kern,
        grid=(d_ff // chunk,),
        in_specs=[_full((rows, d)), _full((rows, d)), _layer_spec(lw["w_o"], layer),
                  _layer_spec(lw["g_ffn"], layer),
                  pl.BlockSpec((None, d, chunk), lambda c: (layer, 0, c)),
                  pl.BlockSpec((None, d, chunk), lambda c: (layer, 0, c)),
                  pl.BlockSpec((None, chunk, d), lambda c: (layer, c, 0)),
                  _full(gf.shape)],
        out_specs=_full((rows, d)),
        out_shape=jax.ShapeDtypeStruct((rows, d), F32),
        scratch_shapes=[pltpu.VMEM((rows, d), BF16), pltpu.VMEM((rows, d), F32)],
        compiler_params=pltpu.CompilerParams(
            dimension_semantics=("arbitrary",), vmem_limit_bytes=VMEM_LIMIT),
        name="ffn",
    )(x2d, o2d, lw["w_o"], lw["g_ffn"], lw["w_gate"], lw["w_up"], lw["w_down"], gf)


def _rows8(v):
    return jnp.broadcast_to(v[..., None, :], v.shape[:-1] + (SUBLANES, v.shape[-1]))


def _block_diag_tiles(w, width):
    l, h, i, _ = w.shape
    per_tile = width // i
    eye = jnp.eye(per_tile, dtype=w.dtype)
    tiles = jnp.einsum("lthij,hg->lthigj", w.reshape(l, h // per_tile, per_tile, i, i), eye)
    return tiles.reshape(l, h // per_tile, width, width)


def kernel(x_prompt, x_sample, state_rglru_h, state_rglru_conv, state_conf_conv, cache_mem_k, cache_mem_v, mem_prompt, norm_mix_g, w_in, rg_conv_w, rg_conv_b, rg_wa, rg_ba, rg_wx, rg_bx, rg_lambda, cf_conv_w, cf_conv_b, cf_ln_g, cf_ln_b, w_out, norm_attn_g, norm_mem_g, w_q, w_k, w_v, w_o, norm_ffn_g, w_gate, w_up, w_down, norm_final_g):
    batch, seq, d = x_prompt.shape
    dec_batch, dec_seq, _ = x_sample.shape
    depth = w_in.shape[0]
    n_mem = mem_prompt.shape[1]
    heads = cache_mem_k.shape[3]
    pad_t = SUBLANES

    lw = {
        "g_mix": norm_mix_g[:, None, :],
        "wa": _block_diag_tiles(rg_wa, MXU_DIM).astype(BF16),
        "wx": _block_diag_tiles(rg_wx, MXU_DIM).astype(BF16),
        "vec": _rows8(jnp.concatenate(
            [v if v.ndim == 3 else v[:, None, :] for v in
             (rg_conv_w, rg_conv_b, rg_ba, rg_bx, rg_lambda, cf_conv_w, cf_conv_b, cf_ln_g, cf_ln_b)],
            axis=1)),
        "g_attn": norm_attn_g[:, None, :],
        "g_ffn": norm_ffn_g[:, None, :],
    }
    gf = norm_final_g[None, :]

    big = {"w_in": w_in, "w_out": w_out, "w_q": w_q, "w_o": w_o,
           "w_gate": w_gate, "w_up": w_up, "w_down": w_down}
    mem_k, mem_v, mem_kb, mem_vb, big_bf16 = _memkv(
        mem_prompt.reshape(batch * n_mem, d), norm_mem_g[:, None, :],
        w_k, w_v, tuple(big.values()), tile=n_mem, n_mem=n_mem, heads=heads)
    lw.update(zip(big, big_bf16))

    rb_tm = jnp.transpose(state_rglru_conv, (0, 2, 1, 3))
    cb_tm = jnp.transpose(state_conf_conv, (0, 2, 1, 3))

    xp = x_prompt.reshape(batch * seq, d)
    xs_tm = jnp.transpose(x_sample, (1, 0, 2)).reshape(dec_seq * dec_batch, d)
    p_h, p_rg, p_cf, s_h, s_xr, s_c = [], [], [], [], [], []
    for l in range(depth):
        last = l == depth - 1

        xs_tm, hs, xr_tm, c_tm, sq_tm = _mix_sample(xs_tm, state_rglru_h, rb_tm, cb_tm, lw, l,
                                                    steps=dec_seq, batch=dec_batch, heads=heads)
        s_h.append(hs); s_xr.append(xr_tm); s_c.append(c_tm)
        sq_pad = jnp.pad(jnp.transpose(sq_tm.reshape(dec_seq, dec_batch, d), (1, 0, 2)),
                         ((0, 0), (0, pad_t - dec_seq), (0, 0))).reshape(dec_batch * pad_t, d)

        xp, hp, rgp, cfp, so_pad = _layer_prompt(
            xp, mem_kb, mem_vb, sq_pad, cache_mem_k, cache_mem_v, lw, gf, l, batch=batch, seq=seq,
            n_mem=n_mem, heads=heads, final_norm=last, pad_t=pad_t)
        p_h.append(hp[:, 0, :]); p_rg.append(rgp); p_cf.append(cfp)

        so_tm = jnp.transpose(so_pad.reshape(dec_batch, pad_t, d)[:, :dec_seq], (1, 0, 2)).reshape(
            dec_seq * dec_batch, d)
        xs_tm = _ffn(xs_tm, so_tm, lw, gf, l, final_norm=last)

    def new_conv_state(old, fresh_tm, width):
        fresh = jnp.transpose(jnp.stack(fresh_tm).reshape(depth, dec_seq, dec_batch, -1), (0, 2, 1, 3))
        return jnp.concatenate([old, fresh], axis=2)[:, :, -(width - 1):]

    y_sample = jnp.transpose(xs_tm.reshape(dec_seq, dec_batch, d), (1, 0, 2))
    return (xp.reshape(batch, seq, d), y_sample,
            jnp.stack(p_h), jnp.stack(p_rg), jnp.stack(p_cf),
            mem_k, mem_v,
            jnp.stack(s_h),
            new_conv_state(state_rglru_conv, s_xr, RG_CONV_W),
            new_conv_state(state_conf_conv, s_c, CF_CONV_W))
```
